```python
import math
import jax, jax.numpy as jnp
from jax import lax
import numpy as np

D_MODEL = 1024
BATCH = 1
SEQ = 16384
DEPTH = 1
DEC_BATCH = 16
DEC_SEQ = 2048
PAST_LEN = 128

ATTN_HEADS = 8
ATTN_KV_HEADS = 2
ATTN_HEAD_DIM = 64
WINDOW = 128
BLOCK = 128
N_BUCKETS = 32
MAX_DISTANCE = 128
GLA_HEADS = 4
GLA_KEY_DIM = 64
GLA_VAL_DIM = 128
GLA_GATE_RANK = 16
GLA_GATE_NORM = 16.0
GLA_CHUNK = 64
PEER_HEADS = 8
N_KEYS = 128
N_EXPERTS = N_KEYS * N_KEYS
PEER_KEY_DIM = 256
PEER_TOPK = 16
PEER_TOKEN_BLOCK = 128
EPS = 1e-6

ATTN_Q = ATTN_HEADS * ATTN_HEAD_DIM
ATTN_KV = ATTN_KV_HEADS * ATTN_HEAD_DIM
GLA_QK = GLA_HEADS * GLA_KEY_DIM
GLA_V = GLA_HEADS * GLA_VAL_DIM
MIX_WIDTH = ATTN_Q + GLA_V
D_IN = ATTN_Q + 2 * ATTN_KV + 2 * GLA_QK + 2 * GLA_V + 2 * GLA_GATE_RANK

kernel_name = 'hymba_swa_gla_peer_adaln_encoder'


def _rms_norm(x, g):
    xf = x.astype(jnp.float32)
    y = xf * lax.rsqrt(jnp.mean(xf * xf, axis=-1, keepdims=True) + EPS)
    return (y * g.astype(jnp.float32)).astype(x.dtype)


def _t5_buckets(rel):
    nb = N_BUCKETS // 2
    ret = (rel > 0).astype(np.int32) * nb
    n = np.abs(rel)
    max_exact = nb // 2
    large = max_exact + (np.log(np.maximum(n, 1) / max_exact) / math.log(MAX_DISTANCE / max_exact)
                         * (nb - max_exact)).astype(np.int32)
    large = np.minimum(large, nb - 1)
    return (ret + np.where(n < max_exact, n, large)).astype(np.int32)


def _windowed_attention(q, k, v, q_norm_g, k_norm_g, rel_bias, sink):
    B, S = q.shape[:2]
    nb = S // BLOCK
    G = ATTN_HEADS // ATTN_KV_HEADS
    q = _rms_norm(q, q_norm_g)
    k = _rms_norm(k, k_norm_g)
    qb = q.reshape(B, nb, BLOCK, ATTN_KV_HEADS, G, ATTN_HEAD_DIM)
    pad = ((0, 0), (BLOCK, BLOCK), (0, 0), (0, 0))
    kp = jnp.pad(k, pad).reshape(B, nb + 2, BLOCK, ATTN_KV_HEADS, ATTN_HEAD_DIM)
    vp = jnp.pad(v, pad).reshape(B, nb + 2, BLOCK, ATTN_KV_HEADS, ATTN_HEAD_DIM)
    kb = jnp.concatenate([kp[:, :-2], kp[:, 1:-1], kp[:, 2:]], axis=2)
    vb = jnp.concatenate([vp[:, :-2], vp[:, 1:-1], vp[:, 2:]], axis=2)
    scores = jnp.einsum('bnihgd,bnjhd->bnhgij', qb, kb).astype(jnp.float32) * (ATTN_HEAD_DIM ** -0.5)
    rel = np.arange(3 * BLOCK)[None, :] - BLOCK - np.arange(BLOCK)[:, None]
    bias = rel_bias[_t5_buckets(rel)]
    bias = jnp.transpose(bias, (2, 0, 1)).reshape(ATTN_KV_HEADS, G, BLOCK, 3 * BLOCK).astype(jnp.float32)
    band = np.abs(rel) <= WINDOW
    key_pos = np.arange(nb)[:, None] * BLOCK - BLOCK + np.arange(3 * BLOCK)[None, :]
    valid = (key_pos >= 0) & (key_pos < S)
    mask = jnp.asarray(band[None] & valid[:, None, :])
    s = jnp.where(mask[None, :, None, None], scores + bias, -1e30)
    sink_b = sink.astype(jnp.float32).reshape(ATTN_KV_HEADS, G)[None, None, :, :, None, None]
    m = jnp.maximum(jnp.max(s, axis=-1, keepdims=True), sink_b)
    p = jnp.exp(s - m)
    p = p / (jnp.sum(p, axis=-1, keepdims=True) + jnp.exp(sink_b - m))
    out = jnp.einsum('bnhgij,bnjhd->bnihgd', p.astype(v.dtype), vb)
    return out.reshape(B, S, ATTN_HEADS, ATTN_HEAD_DIM)


def _gla_direction(q, k, v, log_a):
    B, S, H, dk = q.shape
    dv = v.shape[-1]
    C = GLA_CHUNK
    nc = S // C
    q, k, log_a = [t.reshape(B, nc, C, H, dk) for t in (q, k, log_a)]
    v = v.reshape(B, nc, C, H, dv)
    b = jnp.cumsum(log_a, axis=2)
    b_last = b[:, :, -1:]
    q_dec = q * jnp.exp(b)
    k_dec = k * jnp.exp(-b)
    k_end = k * jnp.exp(b_last - b)
    causal = jnp.tril(jnp.ones((C, C), dtype=bool))
    a = jnp.where(causal, jnp.einsum('bnihd,bnjhd->bnhij', q_dec, k_dec), 0.0)
    o_intra = jnp.einsum('bnhij,bnjhv->bnihv', a, v)
    chunk_kv = jnp.einsum('bnjhd,bnjhv->bnhdv', k_end, v)
    chunk_decay = jnp.exp(b_last[:, :, 0])

    def step(state, inp):
        decay, kv = inp
        return decay[..., None] * state + kv, state

    init = jnp.zeros((B, H, dk, dv), jnp.float32)
    _, states = lax.scan(step, init, (jnp.moveaxis(chunk_decay, 1, 0), jnp.moveaxis(chunk_kv, 1, 0)))
    states = jnp.moveaxis(states, 0, 1)
    o_inter = jnp.einsum('bnihd,bnhdv->bnihv', q_dec, states)
    return (o_intra + o_inter).reshape(B, S, H, dv)


def _mixer(h, w_in, q_norm_g, k_norm_g, rel_bias, sink, gk_up_f, gk_b_f, gk_up_b, gk_b_b,
           beta_attn, beta_gla, w_out):
    B, S, _ = h.shape
    proj = h @ w_in
    sizes = [ATTN_Q, ATTN_KV, ATTN_KV, GLA_QK, GLA_QK, GLA_V, GLA_V, GLA_GATE_RANK]
    cuts = [int(c) for c in np.cumsum(sizes)]
    qa, ka, va, qg, kg, vg, og, gr_f, gr_b = jnp.split(proj, cuts, axis=-1)
    att = _windowed_attention(qa.reshape(B, S, ATTN_HEADS, ATTN_HEAD_DIM),
                              ka.reshape(B, S, ATTN_KV_HEADS, ATTN_HEAD_DIM),
                              va.reshape(B, S, ATTN_KV_HEADS, ATTN_HEAD_DIM),
                              q_norm_g, k_norm_g, rel_bias, sink)
    att = _rms_norm(att, beta_attn.reshape(ATTN_HEADS, ATTN_HEAD_DIM)).reshape(B, S, ATTN_Q)
    f32 = jnp.float32
    qg = qg.reshape(B, S, GLA_HEADS, GLA_KEY_DIM).astype(f32) * (GLA_KEY_DIM ** -0.5)
    kg = kg.reshape(B, S, GLA_HEADS, GLA_KEY_DIM).astype(f32)
    vg = vg.reshape(B, S, GLA_HEADS, GLA_VAL_DIM).astype(f32)
    log_a_f = jax.nn.log_sigmoid((gr_f @ gk_up_f + gk_b_f).astype(f32)).reshape(B, S, GLA_HEADS, GLA_KEY_DIM) / GLA_GATE_NORM
    log_a_b = jax.nn.log_sigmoid((gr_b @ gk_up_b + gk_b_b).astype(f32)).reshape(B, S, GLA_HEADS, GLA_KEY_DIM) / GLA_GATE_NORM
    o_f = _gla_direction(qg, kg, vg, log_a_f)
    o_b = jnp.flip(_gla_direction(jnp.flip(qg, 1), jnp.flip(kg, 1), jnp.flip(vg, 1), jnp.flip(log_a_b, 1)), 1)
    gla = _rms_norm(o_f + o_b, beta_gla.reshape(GLA_HEADS, GLA_VAL_DIM)).astype(h.dtype).reshape(B, S, GLA_V)
    gla = gla * jax.nn.silu(og)
    return jnp.concatenate([att.astype(h.dtype), gla], axis=-1) @ w_out


def _peer(h, w_query, sub_keys, peer_u, peer_v):
    B, S, D = h.shape
    T = B * S
    xt = h.reshape(T, D)
    qr = (xt @ w_query).reshape(T, PEER_HEADS, 2, PEER_KEY_DIM // 2)
    sc = jnp.einsum('thpd,hpnd->thpn', qr, sub_keys).astype(jnp.float32)
    s1, i1 = lax.top_k(sc[:, :, 0], PEER_TOPK)
    s2, i2 = lax.top_k(sc[:, :, 1], PEER_TOPK)
    cand_s = (s1[..., :, None] + s2[..., None, :]).reshape(T, PEER_HEADS, PEER_TOPK * PEER_TOPK)
    cand_i = (i1[..., :, None] * N_KEYS + i2[..., None, :]).reshape(T, PEER_HEADS, PEER_TOPK * PEER_TOPK)
    top_s, pos = lax.top_k(cand_s, PEER_TOPK)
    idx = jnp.take_along_axis(cand_i, pos, axis=-1)
    g = jax.nn.softmax(top_s, axis=-1)
    nblk = T // PEER_TOKEN_BLOCK
    HK = PEER_HEADS * PEER_TOPK

    def block(args):
        xb, ib, gb = args
        act = jnp.einsum('td,tkd->tk', xb, peer_u[ib]).astype(jnp.float32)
        w = (gb * jax.nn.gelu(act, approximate=False)).astype(xb.dtype)
        return jnp.einsum('tk,tkd->td', w, peer_v[ib])

    out = lax.map(block, (xt.reshape(nblk, PEER_TOKEN_BLOCK, D),
                          idx.reshape(nblk, PEER_TOKEN_BLOCK, HK),
                          g.reshape(nblk, PEER_TOKEN_BLOCK, HK)))
    return out.reshape(B, S, D)


def _layer(x, c, w_ada, b_ada, norm1_g, norm2_g, w_in, q_norm_g, k_norm_g, rel_bias, sink,
           gk_up_f, gk_b_f, gk_up_b, gk_b_b, beta_attn, beta_gla, w_out,
           peer_query, peer_subkeys, peer_u, peer_v):
    mod = (jax.nn.silu(c) @ w_ada + b_ada)[:, None, :]
    shift1, scale1, gate1, shift2, scale2, gate2 = jnp.split(mod, 6, axis=-1)
    h = _rms_norm(x, norm1_g) * (1 + scale1) + shift1
    x = x + gate1 * _mixer(h, w_in, q_norm_g, k_norm_g, rel_bias, sink, gk_up_f, gk_b_f, gk_up_b, gk_b_b,
                           beta_attn, beta_gla, w_out)
    h = _rms_norm(x, norm2_g) * (1 + scale2) + shift2
    return x + gate2 * _peer(h, peer_query, peer_subkeys, peer_u, peer_v)


def setup_inputs(seed: int = 0) -> dict:
    key = jax.random.key(seed)
    ks = jax.random.split(key, 26)
    f32 = jnp.float32
    D = D_MODEL
    nrm = lambda k, shape, s: jax.random.normal(k, shape, f32) * s
    return {
        'x_prompt': nrm(ks[0], (BATCH, SEQ, D), 1.0),
        'x_sample': nrm(ks[1], (DEC_BATCH, DEC_SEQ, D), 1.0),
        'c_prompt': nrm(ks[2], (BATCH, D), 1.0),
        'c_sample': nrm(ks[3], (DEC_BATCH, D), 1.0),
        'w_ada': nrm(ks[4], (DEPTH, D, 6 * D), 0.5 * D ** -0.5),
        'b_ada': nrm(ks[5], (DEPTH, 6 * D), 0.02),
        'norm1_g': 1.0 + nrm(ks[6], (DEPTH, D), 0.05),
        'norm2_g': 1.0 + nrm(ks[7], (DEPTH, D), 0.05),
        'w_in': nrm(ks[8], (DEPTH, D, D_IN), D ** -0.5),
        'q_norm_g': 1.0 + nrm(ks[9], (DEPTH, ATTN_HEAD_DIM), 0.05),
        'k_norm_g': 1.0 + nrm(ks[10], (DEPTH, ATTN_HEAD_DIM), 0.05),
        'rel_bias': nrm(ks[11], (N_BUCKETS, ATTN_HEADS), 0.5),
        'sink': nrm(ks[12], (DEPTH, ATTN_HEADS), 0.5),
        'gk_up_fwd': nrm(ks[13], (DEPTH, GLA_GATE_RANK, GLA_QK), GLA_GATE_RANK ** -0.5),
        'gk_bias_fwd': nrm(ks[14], (DEPTH, GLA_QK), 0.1),
        'gk_up_bwd': nrm(ks[15], (DEPTH, GLA_GATE_RANK, GLA_QK), GLA_GATE_RANK ** -0.5),
        'gk_bias_bwd': nrm(ks[16], (DEPTH, GLA_QK), 0.1),
        'beta_attn': 1.0 + nrm(ks[17], (DEPTH, ATTN_Q), 0.05),
        'beta_gla': 1.0 + nrm(ks[18], (DEPTH, GLA_V), 0.05),
        'w_out': nrm(ks[19], (DEPTH, MIX_WIDTH, D), MIX_WIDTH ** -0.5),
        'peer_query': nrm(ks[20], (DEPTH, D, PEER_HEADS * PEER_KEY_DIM), D ** -0.5),
        'peer_subkeys': nrm(ks[21], (DEPTH, PEER_HEADS, 2, N_KEYS, PEER_KEY_DIM // 2), (PEER_KEY_DIM // 2) ** -0.5),
        'peer_u': nrm(ks[22], (DEPTH, N_EXPERTS, D), D ** -0.5),
        'peer_v': nrm(ks[23], (DEPTH, N_EXPERTS, D), PEER_HEADS ** -0.5),
    }


def reference(x_prompt, x_sample, c_prompt, c_sample, w_ada, b_ada, norm1_g, norm2_g, w_in,
              q_norm_g, k_norm_g, rel_bias, sink, gk_up_fwd, gk_bias_fwd, gk_up_bwd, gk_bias_bwd,
              beta_attn, beta_gla, w_out, peer_query, peer_subkeys, peer_u, peer_v):
    y_prompt = x_prompt
    y_sample = x_sample
    for l in range(DEPTH):
        params = (w_ada[l], b_ada[l], norm1_g[l], norm2_g[l], w_in[l], q_norm_g[l], k_norm_g[l],
                  rel_bias, sink[l], gk_up_fwd[l], gk_bias_fwd[l], gk_up_bwd[l], gk_bias_bwd[l],
                  beta_attn[l], beta_gla[l], w_out[l], peer_query[l], peer_subkeys[l], peer_u[l], peer_v[l])
        y_prompt = _layer(y_prompt, c_prompt, *params)
        y_sample = _layer(y_sample, c_sample, *params)
    return (y_prompt, y_sample)
```

```python
import functools
import math

import numpy as np
import jax
import jax.numpy as jnp
from jax import lax
from jax.experimental import pallas as pl
from jax.experimental.pallas import tpu as pltpu

F32 = jnp.float32
BF16 = jnp.bfloat16

D_MODEL = 1024
ATTN_HEADS = 8
ATTN_KV_HEADS = 2
ATTN_HEAD_DIM = 64
ATTN_GROUP = ATTN_HEADS // ATTN_KV_HEADS
WINDOW = 128
BLOCK = 128
N_BUCKETS = 32
MAX_DISTANCE = 128
GLA_HEADS = 4
GLA_KEY_DIM = 64
GLA_VAL_DIM = 128
GLA_GATE_RANK = 16
GLA_GATE_NORM = 16.0
GLA_CHUNK = 64
PEER_HEADS = 8
N_KEYS = 128
PEER_KEY_DIM = 256
PEER_HALF = PEER_KEY_DIM // 2
PEER_TOPK = 16
EPS = 1e-6

ATTN_Q = ATTN_HEADS * ATTN_HEAD_DIM
ATTN_KV = ATTN_KV_HEADS * ATTN_HEAD_DIM
GLA_QK = GLA_HEADS * GLA_KEY_DIM
GLA_V = GLA_HEADS * GLA_VAL_DIM
ATTN_COLS = ATTN_Q + 2 * ATTN_KV
GLA_COLS = 2 * GLA_QK + 2 * GLA_V
GATE_COLS = 2 * GLA_GATE_RANK
D_IN = ATTN_COLS + GLA_COLS + GATE_COLS

V7X_LANES = 128
V7X_SUBLANES = 8
V7X_VMEM_LIMIT_CAP = 56 * 1024 * 1024

NEG_BIG = -1e30

TOK_TILE = 512
GLA_TILE = 512
EXPERT_TILE = 1024
ROWS_PER_TILE = EXPERT_TILE // N_KEYS
N_RANKS = PEER_TOPK + 1
CAND_PAIRS = tuple((i, j) for i in range(N_RANKS) for j in range(N_RANKS) if (i + 1) * (j + 1) <= N_RANKS)


def _vmem_limit(block_bytes):
    return int(min(2 * block_bytes + (16 << 20), V7X_VMEM_LIMIT_CAP))


def _params(semantics, block_bytes):
    return pltpu.CompilerParams(dimension_semantics=semantics, vmem_limit_bytes=_vmem_limit(block_bytes))


def _nt_dot(a, b):
    return lax.dot_general(a, b, (((1,), (1,)), ((), ())), preferred_element_type=F32)


def _dot(a, b):
    return jnp.dot(a, b, preferred_element_type=F32)


def _rms(x, g):
    return x * lax.rsqrt(jnp.mean(x * x, axis=-1, keepdims=True) + EPS) * g


def _mod_kernel(c_ref, w_ref, b_ref, o_ref):
    c = c_ref[...]
    s = c * jax.nn.sigmoid(c)
    o_ref[...] = _dot(s.astype(BF16), w_ref[...].astype(BF16)) + b_ref[...]


def _modulation(c, w_ada, b_ada):
    bsz = c.shape[0]
    rows = -(-bsz // V7X_SUBLANES) * V7X_SUBLANES
    cp = jnp.zeros((rows, D_MODEL), F32).at[:bsz].set(c)
    n_out = w_ada.shape[1]
    tile = D_MODEL
    out = pl.pallas_call(
        _mod_kernel,
        grid=(n_out // tile,),
        in_specs=[pl.BlockSpec((rows, D_MODEL), lambda j: (0, 0)),
                  pl.BlockSpec((D_MODEL, tile), lambda j: (0, j)),
                  pl.BlockSpec((1, tile), lambda j: (0, j))],
        out_specs=pl.BlockSpec((rows, tile), lambda j: (0, j)),
        out_shape=jax.ShapeDtypeStruct((rows, n_out), F32),
        compiler_params=_params(("parallel",), D_MODEL * tile * 4),
        name="adaln_modulation",
    )(cp, w_ada, b_ada.reshape(1, n_out))
    return out[:bsz].reshape(bsz, 1, n_out)


def _inproj_kernel(x_ref, mod_ref, g_ref, w_ref, oa_ref, og_ref, or_ref):
    h = _rms(x_ref[...], g_ref[...])
    h = h * (1.0 + mod_ref[:, D_MODEL:2 * D_MODEL]) + mod_ref[:, 0:D_MODEL]
    p = _dot(h.astype(BF16), w_ref[...])
    oa_ref[...] = p[:, :ATTN_COLS]
    og_ref[...] = p[:, ATTN_COLS:ATTN_COLS + GLA_COLS]
    or_ref[...] = p[:, ATTN_COLS + GLA_COLS:]


def _in_projection(x, mod, norm_g, w_in_bf16):
    bsz, seq, _ = x.shape
    tb = TOK_TILE
    tok = lambda cols: pl.BlockSpec((None, tb, cols), lambda b, i: (b, i, 0))
    return pl.pallas_call(
        _inproj_kernel,
        grid=(bsz, seq // tb),
        in_specs=[tok(D_MODEL),
                  pl.BlockSpec((None, 1, 6 * D_MODEL), lambda b, i: (b, 0, 0)),
                  pl.BlockSpec((1, D_MODEL), lambda b, i: (0, 0)),
                  pl.BlockSpec((D_MODEL, D_IN), lambda b, i: (0, 0))],
        out_specs=[tok(ATTN_COLS), tok(GLA_COLS), tok(GATE_COLS)],
        out_shape=[jax.ShapeDtypeStruct((bsz, seq, ATTN_COLS), F32),
                   jax.ShapeDtypeStruct((bsz, seq, GLA_COLS), F32),
                   jax.ShapeDtypeStruct((bsz, seq, GATE_COLS), F32)],
        compiler_params=_params(("parallel", "parallel"),
                                tb * D_MODEL * 4 + D_MODEL * D_IN * 2 + 2 * tb * D_IN * 4),
        name="norm1_in_projection",
    )(x, mod, norm_g.reshape(1, D_MODEL), w_in_bf16)


def _t5_buckets(rel):
    nb = N_BUCKETS // 2
    ret = (rel > 0).astype(np.int32) * nb
    n = np.abs(rel)
    max_exact = nb // 2
    large = max_exact + (np.log(np.maximum(n, 1) / max_exact) / math.log(MAX_DISTANCE / max_exact)
                         * (nb - max_exact)).astype(np.int32)
    large = np.minimum(large, nb - 1)
    return (ret + np.where(n < max_exact, n, large)).astype(np.int32)


def _band_bias(rel_bias):
    rel = np.arange(3 * BLOCK)[None, :] - BLOCK - np.arange(BLOCK)[:, None]
    bias = jnp.transpose(rel_bias[_t5_buckets(rel)], (2, 0, 1)).astype(F32)
    band = jnp.asarray(np.abs(rel) <= WINDOW)
    return jnp.where(band[None], bias, NEG_BIG)


def _attn_kernel(sink_ref, q_ref, kvp_ref, kvc_ref, kvn_ref, bias_ref, qg_ref, kg_ref, beta_ref, o_ref):
    n = pl.program_id(1)
    last = pl.num_programs(1) - 1
    q = q_ref[...]
    kv = jnp.concatenate([kvp_ref[...], kvc_ref[...], kvn_ref[...]], axis=0)
    j = lax.broadcasted_iota(jnp.int32, (BLOCK, 3 * BLOCK), 1)
    valid = jnp.logical_and(jnp.logical_or(j >= BLOCK, n > 0), jnp.logical_or(j < 2 * BLOCK, n < last))
    scale = ATTN_HEAD_DIM ** -0.5
    outs = []
    for hk in range(ATTN_KV_HEADS):
        kh = _rms(kv[:, hk * ATTN_HEAD_DIM:(hk + 1) * ATTN_HEAD_DIM], kg_ref[...]).astype(BF16)
        vh = kv[:, ATTN_KV + hk * ATTN_HEAD_DIM:ATTN_KV + (hk + 1) * ATTN_HEAD_DIM].astype(BF16)
        for g in range(ATTN_GROUP):
            h = hk * ATTN_GROUP + g
            qh = _rms(q[:, h * ATTN_HEAD_DIM:(h + 1) * ATTN_HEAD_DIM], qg_ref[...]).astype(BF16)
            s = _nt_dot(qh, kh) * scale
            s = jnp.where(valid, s + bias_ref[h], NEG_BIG)
            sink = sink_ref[h]
            m = jnp.maximum(jnp.max(s, axis=-1, keepdims=True), sink)
            p = jnp.exp(s - m)
            den = jnp.sum(p, axis=-1, keepdims=True) + jnp.exp(sink - m)
            o = _dot(p.astype(BF16), vh) / den
            outs.append(_rms(o, beta_ref[:, h * ATTN_HEAD_DIM:(h + 1) * ATTN_HEAD_DIM]))
    o_ref[...] = jnp.concatenate(outs, axis=-1)


def _windowed_attention(qkv, bias, sink, q_norm_g, k_norm_g, beta_attn):
    bsz, seq, _ = qkv.shape
    nb = seq // BLOCK
    kv_cols = 2 * ATTN_KV
    kv_blk = ATTN_Q // kv_cols
    kv_spec = lambda f: pl.BlockSpec((None, BLOCK, kv_cols), lambda b, n: (b, f(n), kv_blk))
    const = lambda shape: pl.BlockSpec(shape, lambda b, n: tuple(0 for _ in shape))
    return pl.pallas_call(
        _attn_kernel,
        grid=(bsz, nb),
        in_specs=[pl.BlockSpec(memory_space=pltpu.SMEM),
                  pl.BlockSpec((None, BLOCK, ATTN_Q), lambda b, n: (b, n, 0)),
                  kv_spec(lambda n: jnp.maximum(n - 1, 0)),
                  kv_spec(lambda n: n),
                  kv_spec(lambda n: jnp.minimum(n + 1, nb - 1)),
                  const((ATTN_HEADS, BLOCK, 3 * BLOCK)),
                  const((1, ATTN_HEAD_DIM)), const((1, ATTN_HEAD_DIM)), const((1, ATTN_Q))],
        out_specs=pl.BlockSpec((None, BLOCK, ATTN_Q), lambda b, n: (b, n, 0)),
        out_shape=jax.ShapeDtypeStruct((bsz, seq, ATTN_Q), F32),
        compiler_params=_params(("parallel", "parallel"), ATTN_HEADS * BLOCK * 3 * BLOCK * 4 + 8 * BLOCK * ATTN_Q * 4),
        name="windowed_attention",
    )(sink, qkv, qkv, qkv, qkv, bias, q_norm_g.reshape(1, -1), k_norm_g.reshape(1, -1), beta_attn.reshape(1, -1))


def _split3(x):
    hi = x.astype(BF16)
    r = x - hi.astype(F32)
    mid = r.astype(BF16)
    lo = (r - mid.astype(F32)).astype(BF16)
    return hi, mid, lo


def _gla_chunk(q, k, v, gr, up, gbias, tri, eye, st_ref, reverse):
    c = GLA_CHUNK
    z = _dot(gr.astype(BF16), up) + gbias
    log_a = (jnp.minimum(z, 0.0) - jnp.log1p(jnp.exp(-jnp.abs(z)))) / GLA_GATE_NORM
    hi, mid, lo = _split3(log_a)
    b = _dot(tri, hi) + _dot(tri, mid) + _dot(tri, lo)
    b_last = b[0:1, :] if reverse else b[c - 1:c, :]
    q_dec = q * ((GLA_KEY_DIM ** -0.5) * jnp.exp(b))
    k_dec = (k * jnp.exp(-b)).astype(BF16)
    k_end = k * jnp.exp(b_last - b)
    decay = jnp.exp(b_last)
    row = lax.broadcasted_iota(jnp.int32, (c, c), 0)
    col = lax.broadcasted_iota(jnp.int32, (c, c), 1)
    causal = (col >= row) if reverse else (col <= row)
    lane = lax.broadcasted_iota(jnp.int32, (c, GLA_QK), 1)
    outs = []
    for h in range(GLA_HEADS):
        head = jnp.logical_and(lane >= h * GLA_KEY_DIM, lane < (h + 1) * GLA_KEY_DIM)
        qh = jnp.where(head, q_dec, 0.0).astype(BF16)
        keh = jnp.where(head, k_end, 0.0).astype(BF16)
        vh = v[:, h * GLA_VAL_DIM:(h + 1) * GLA_VAL_DIM].astype(BF16)
        a = jnp.where(causal, _nt_dot(qh, k_dec), 0.0)
        st = st_ref[h]
        outs.append(_dot(a.astype(BF16), vh) + _nt_dot(qh, st.astype(BF16)))
        vt = _nt_dot(eye, vh).astype(BF16)
        st_ref[h] = st * decay + _dot(vt, keh)
    return jnp.concatenate(outs, axis=-1)


def _gla_kernel(qkf_ref, vf_ref, grf_ref, qkb_ref, vb_ref, grb_ref, upf_ref, bf_ref, upb_ref, bb_ref,
                trif_ref, trib_ref, eye_ref, of_ref, ob_ref, stf_ref, stb_ref):
    @pl.when(pl.program_id(1) == 0)
    def _():
        stf_ref[...] = jnp.zeros_like(stf_ref)
        stb_ref[...] = jnp.zeros_like(stb_ref)

    nch = GLA_TILE // GLA_CHUNK

    def body(ci, carry):
        f0 = pl.multiple_of(ci * GLA_CHUNK, GLA_CHUNK)
        rows = pl.ds(f0, GLA_CHUNK)
        of_ref[rows, :] = _gla_chunk(qkf_ref[rows, 0:GLA_QK], qkf_ref[rows, GLA_QK:2 * GLA_QK], vf_ref[rows, :],
                                     grf_ref[rows, 0:GLA_GATE_RANK], upf_ref[...], bf_ref[...],
                                     trif_ref[...], eye_ref[...], stf_ref, False)
        r0 = pl.multiple_of((nch - 1 - ci) * GLA_CHUNK, GLA_CHUNK)
        rows = pl.ds(r0, GLA_CHUNK)
        ob_ref[rows, :] = _gla_chunk(qkb_ref[rows, 0:GLA_QK], qkb_ref[rows, GLA_QK:2 * GLA_QK], vb_ref[rows, :],
                                     grb_ref[rows, GLA_GATE_RANK:2 * GLA_GATE_RANK], upb_ref[...], bb_ref[...],
                                     trib_ref[...], eye_ref[...], stb_ref, True)
        return carry

    lax.fori_loop(0, nch, body, 0)


def _gla(gla_in, gates, up_f, b_f, up_b, b_b):
    bsz, seq, _ = gla_in.shape
    tb = GLA_TILE
    nblk = seq // tb
    fwd = lambda cols, cb: pl.BlockSpec((None, tb, cols), lambda b, i: (b, i, cb))
    bwd = lambda cols, cb: pl.BlockSpec((None, tb, cols), lambda b, i: (b, nblk - 1 - i, cb))
    const = lambda shape: pl.BlockSpec(shape, lambda b, i: tuple(0 for _ in shape))
    c = GLA_CHUNK
    tri_f = jnp.asarray(np.tril(np.ones((c, c), np.float32)), BF16)
    tri_b = jnp.asarray(np.triu(np.ones((c, c), np.float32)), BF16)
    eye = jnp.asarray(np.eye(GLA_VAL_DIM, dtype=np.float32), BF16)
    state = pltpu.VMEM((GLA_HEADS, GLA_VAL_DIM, GLA_QK), F32)
    return pl.pallas_call(
        _gla_kernel,
        grid=(bsz, nblk),
        in_specs=[fwd(2 * GLA_QK, 0), fwd(GLA_V, 1), fwd(GATE_COLS, 0),
                  bwd(2 * GLA_QK, 0), bwd(GLA_V, 1), bwd(GATE_COLS, 0),
                  const((GLA_GATE_RANK, GLA_QK)), const((1, GLA_QK)),
                  const((GLA_GATE_RANK, GLA_QK)), const((1, GLA_QK)),
                  const((c, c)), const((c, c)), const((GLA_VAL_DIM, GLA_VAL_DIM))],
        out_specs=[pl.BlockSpec((None, tb, GLA_V), lambda b, i: (b, i, 0)),
                   pl.BlockSpec((None, tb, GLA_V), lambda b, i: (b, nblk - 1 - i, 0))],
        out_shape=[jax.ShapeDtypeStruct((bsz, seq, GLA_V), F32)] * 2,
        scratch_shapes=[state, state],
        compiler_params=_params(("parallel", "arbitrary"), 2 * tb * (2 * GLA_QK + 2 * GLA_V + V7X_LANES) * 4),
        name="gla_bidirectional",
    )(gla_in, gla_in, gates, gla_in, gla_in, gates,
      up_f.astype(BF16), b_f.reshape(1, -1), up_b.astype(BF16), b_b.reshape(1, -1), tri_f, tri_b, eye)


def _outproj_kernel(att_ref, of_ref, ob_ref, og_ref, x_ref, mod_ref, beta_ref, w_ref, g2_ref, x1_ref, h2_ref):
    o = of_ref[...] + ob_ref[...]
    gate = og_ref[...]
    gate = gate * jax.nn.sigmoid(gate)
    parts = [att_ref[...].astype(BF16)]
    for h in range(GLA_HEADS):
        sl = slice(h * GLA_VAL_DIM, (h + 1) * GLA_VAL_DIM)
        parts.append((_rms(o[:, sl], beta_ref[:, sl]) * gate[:, sl]).astype(BF16))
    mix = _dot(jnp.concatenate(parts, axis=-1), w_ref[...])
    d = D_MODEL
    x1 = x_ref[...] + mod_ref[:, 2 * d:3 * d] * mix
    x1_ref[...] = x1
    h2 = _rms(x1, g2_ref[...]) * (1.0 + mod_ref[:, 4 * d:5 * d]) + mod_ref[:, 3 * d:4 * d]
    h2_ref[...] = h2.astype(BF16)


def _out_projection(att, o_f, o_b, gla_in, x, mod, beta_gla, w_out_bf16, norm2_g):
    bsz, seq, _ = x.shape
    tb = TOK_TILE
    tok = lambda cols, cb=0: pl.BlockSpec((None, tb, cols), lambda b, i: (b, i, cb))
    const = lambda shape: pl.BlockSpec(shape, lambda b, i: tuple(0 for _ in shape))
    return pl.pallas_call(
        _outproj_kernel,
        grid=(bsz, seq // tb),
        in_specs=[tok(ATTN_Q), tok(GLA_V), tok(GLA_V), tok(GLA_V, (2 * GLA_QK + GLA_V) // GLA_V), tok(D_MODEL),
                  pl.BlockSpec((None, 1, 6 * D_MODEL), lambda b, i: (b, 0, 0)),
                  const((1, GLA_V)), const((D_MODEL, D_MODEL)), const((1, D_MODEL))],
        out_specs=[tok(D_MODEL), tok(D_MODEL)],
        out_shape=[jax.ShapeDtypeStruct((bsz, seq, D_MODEL), F32),
                   jax.ShapeDtypeStruct((bsz, seq, D_MODEL), BF16)],
        compiler_params=_params(("parallel", "parallel"), tb * (4 * GLA_V + 3 * D_MODEL) * 4 + D_MODEL * D_MODEL * 2),
        name="out_projection_norm2",
    )(att, o_f, o_b, gla_in, x, mod, beta_gla.reshape(1, -1), w_out_bf16, norm2_g.reshape(1, -1))


def _top_values(x, n, out_ref, head):
    for r in range(n):
        m = jnp.max(x, axis=0, keepdims=True)
        out_ref[r, pl.ds(head, 1), :] = m
        x = jnp.where(x == m, -jnp.inf, x)


def _peer_route_kernel(h2_ref, wq_ref, keys_ref, h2t_ref, s2_ref, e2_ref, thr_ref, c_ref, sc_ref, a_ref, b_ref):
    h2 = h2_ref[...]
    h2t_ref[...] = h2.astype(F32).T.astype(BF16)
    q = _dot(h2, wq_ref[...]).astype(BF16)
    for h in range(PEER_HEADS):
        for p in range(2):
            blk = h * 2 + p
            s = _nt_dot(keys_ref[h, p], q[:, blk * PEER_HALF:(blk + 1) * PEER_HALF])
            sc_ref[blk] = s
            _top_values(s, N_RANKS, a_ref if p == 0 else b_ref, h)
    cand = [a_ref[i] + b_ref[j] for (i, j) in CAND_PAIRS]
    vals = []
    for r in range(N_RANKS):
        m = functools.reduce(jnp.maximum, cand)
        vals.append(m)
        cand = [jnp.where(x == m, -jnp.inf, x) for x in cand]
    top = vals[0]
    thr = 0.5 * (vals[PEER_TOPK - 1] + vals[PEER_TOPK])
    z = functools.reduce(jnp.add, [jnp.exp(v - top) for v in vals[:PEER_TOPK]])
    inv_z = 1.0 / z
    a0 = a_ref[0]
    b0 = b_ref[0]
    for h in range(PEER_HEADS):
        s1 = sc_ref[2 * h]
        s2 = sc_ref[2 * h + 1]
        thr_ref[h] = thr[h:h + 1, :] - s1
        c_ref[h] = jnp.exp(s1 - a0[h:h + 1, :]) * inv_z[h:h + 1, :]
        s2_ref[h] = s2
        e2_ref[h] = jnp.exp(s2 - b0[h:h + 1, :])


def _peer_route(h2, wq_bf16, keys_bf16):
    bsz, seq, _ = h2.shape
    tb = TOK_TILE
    per_head = lambda: pl.BlockSpec((None, PEER_HEADS, N_KEYS, tb), lambda b, i: (b, 0, 0, i))
    head_shape = jax.ShapeDtypeStruct((bsz, PEER_HEADS, N_KEYS, seq), F32)
    qcols = PEER_HEADS * PEER_KEY_DIM
    return pl.pallas_call(
        _peer_route_kernel,
        grid=(bsz, seq // tb),
        in_specs=[pl.BlockSpec((None, tb, D_MODEL), lambda b, i: (b, i, 0)),
                  pl.BlockSpec((D_MODEL, qcols), lambda b, i: (0, 0)),
                  pl.BlockSpec((PEER_HEADS, 2, N_KEYS, PEER_HALF), lambda b, i: (0, 0, 0, 0))],
        out_specs=[pl.BlockSpec((None, D_MODEL, tb), lambda b, i: (b, 0, i)),
                   per_head(), per_head(), per_head(), per_head()],
        out_shape=[jax.ShapeDtypeStruct((bsz, D_MODEL, seq), BF16), head_shape, head_shape, head_shape, head_shape],
        scratch_shapes=[pltpu.VMEM((2 * PEER_HEADS, N_KEYS, tb), F32),
                        pltpu.VMEM((N_RANKS, PEER_HEADS, tb), F32),
                        pltpu.VMEM((N_RANKS, PEER_HEADS, tb), F32)],
        compiler_params=_params(("parallel", "parallel"),
                                D_MODEL * qcols * 2 + 5 * PEER_HEADS * N_KEYS * tb * 4 + tb * qcols * 4),
        name="peer_routing",
    )(h2, wq_bf16, keys_bf16)


def _gelu(x):
    return 0.5 * x * (1.0 + lax.erf(x * (2.0 ** -0.5)))


def _peer_dense_kernel(h2t_ref, u_ref, vt_ref, s2_ref, e2_ref, thr_ref, c_ref, x1_ref, mod_ref, y_ref,
                       act_ref, wg_ref, acc_ref):
    j = pl.program_id(2)

    @pl.when(j == 0)
    def _():
        acc_ref[...] = jnp.zeros_like(acc_ref)

    act_ref[...] = _dot(u_ref[...], h2t_ref[...])
    tb = act_ref.shape[1]
    for r in range(ROWS_PER_TILE):
        rows = slice(r * N_KEYS, (r + 1) * N_KEYS)
        for t in range(tb // V7X_LANES):
            cols = slice(t * V7X_LANES, (t + 1) * V7X_LANES)
            w = jnp.zeros((N_KEYS, V7X_LANES), F32)
            for h in range(PEER_HEADS):
                thr = thr_ref[h, r:r + 1, cols]
                coef = c_ref[h, r:r + 1, cols]
                w = w + jnp.where(s2_ref[h, :, cols] >= thr, e2_ref[h, :, cols] * coef, 0.0)
            wg_ref[rows, cols] = (w * _gelu(act_ref[rows, cols])).astype(BF16)
    acc_ref[...] += _dot(vt_ref[...], wg_ref[...])

    @pl.when(j == pl.num_programs(2) - 1)
    def _():
        y_ref[...] = x1_ref[...] + mod_ref[:, 5 * D_MODEL:6 * D_MODEL] * acc_ref[...].T


def _peer_dense(h2t, u_bf16, vt_bf16, s2, e2, thr, coef, x1, mod):
    bsz, seq, _ = x1.shape
    tb = TOK_TILE
    n_exp = u_bf16.shape[0]
    per_head = lambda: pl.BlockSpec((None, PEER_HEADS, N_KEYS, tb), lambda b, i, j: (b, 0, 0, i))
    per_row = lambda: pl.BlockSpec((None, PEER_HEADS, ROWS_PER_TILE, tb), lambda b, i, j: (b, 0, j, i))
    return pl.pallas_call(
        _peer_dense_kernel,
        grid=(bsz, seq // tb, n_exp // EXPERT_TILE),
        in_specs=[pl.BlockSpec((None, D_MODEL, tb), lambda b, i, j: (b, 0, i)),
                  pl.BlockSpec((EXPERT_TILE, D_MODEL), lambda b, i, j: (j, 0)),
                  pl.BlockSpec((D_MODEL, EXPERT_TILE), lambda b, i, j: (0, j)),
                  per_head(), per_head(), per_row(), per_row(),
                  pl.BlockSpec((None, tb, D_MODEL), lambda b, i, j: (b, i, 0)),
                  pl.BlockSpec((None, 1, 6 * D_MODEL), lambda b, i, j: (b, 0, 0))],
        out_specs=pl.BlockSpec((None, tb, D_MODEL), lambda b, i, j: (b, i, 0)),
        out_shape=jax.ShapeDtypeStruct((bsz, seq, D_MODEL), F32),
        scratch_shapes=[pltpu.VMEM((EXPERT_TILE, tb), F32),
                        pltpu.VMEM((EXPERT_TILE, tb), BF16),
                        pltpu.VMEM((D_MODEL, tb), F32)],
        compiler_params=_params(("parallel", "parallel", "arbitrary"),
                                4 * PEER_HEADS * N_KEYS * tb * 4 + 2 * EXPERT_TILE * D_MODEL * 2
                                + 2 * tb * D_MODEL * 4 + 3 * EXPERT_TILE * tb * 4),
        name="peer_dense_experts",
    )(h2t, u_bf16, vt_bf16, s2, e2, thr, coef, x1, mod)


def _layer(x, c, w_ada, b_ada, norm1_g, norm2_g, w_in, q_norm_g, k_norm_g, bias, sink,
           up_f, b_f, up_b, b_b, beta_attn, beta_gla, w_out, wq, keys, u, vt):
    mod = _modulation(c, w_ada, b_ada)
    qkv, gla_in, gates = _in_projection(x, mod, norm1_g, w_in)
    att = _windowed_attention(qkv, bias, sink, q_norm_g, k_norm_g, beta_attn)
    o_f, o_b = _gla(gla_in, gates, up_f, b_f, up_b, b_b)
    x1, h2 = _out_projection(att, o_f, o_b, gla_in, x, mod, beta_gla, w_out, norm2_g)
    h2t, s2, e2, thr, coef = _peer_route(h2, wq, keys)
    return _peer_dense(h2t, u, vt, s2, e2, thr, coef, x1, mod)


def kernel(x_prompt, x_sample, c_prompt, c_sample, w_ada, b_ada, norm1_g, norm2_g, w_in, q_norm_g, k_norm_g,
           rel_bias, sink, gk_up_fwd, gk_bias_fwd, gk_up_bwd, gk_bias_bwd, beta_attn, beta_gla, w_out,
           peer_query, peer_subkeys, peer_u, peer_v):
    depth = w_ada.shape[0]
    bias = _band_bias(rel_bias)
    y_prompt, y_sample = x_prompt, x_sample
    for l in range(depth):
        params = (w_ada[l], b_ada[l], norm1_g[l], norm2_g[l], w_in[l].astype(BF16), q_norm_g[l], k_norm_g[l],
                  bias, sink[l], gk_up_fwd[l], gk_bias_fwd[l], gk_up_bwd[l], gk_bias_bwd[l],
                  beta_attn[l], beta_gla[l], w_out[l].astype(BF16), peer_query[l].astype(BF16),
                  peer_subkeys[l].astype(BF16), peer_u[l].astype(BF16), peer_v[l].T.astype(BF16))
        y_prompt = _layer(y_prompt, c_prompt, *params)
        y_sample = _layer(y_sample, c_sample, *params)
    return (y_prompt, y_sample)
```

```python
import functools
import math

import numpy as np
import jax
import jax.numpy as jnp
from jax import lax
from jax.experimental import pallas as pl
from jax.experimental.pallas import tpu as pltpu

F32 = jnp.float32
BF16 = jnp.bfloat16

D_MODEL = 1024
ATTN_HEADS = 8
ATTN_KV_HEADS = 2
ATTN_HEAD_DIM = 64
ATTN_GROUP = ATTN_HEADS // ATTN_KV_HEADS
WINDOW = 128
BLOCK = 128
N_BUCKETS = 32
MAX_DISTANCE = 128
GLA_HEADS = 4
GLA_KEY_DIM = 64
GLA_VAL_DIM = 128
GLA_GATE_RANK = 16
GLA_GATE_NORM = 16.0
GLA_CHUNK = 64
PEER_HEADS = 8
N_KEYS = 128
PEER_KEY_DIM = 256
PEER_HALF = PEER_KEY_DIM // 2
PEER_TOPK = 16
EPS = 1e-6

ATTN_Q = ATTN_HEADS * ATTN_HEAD_DIM
ATTN_KV = ATTN_KV_HEADS * ATTN_HEAD_DIM
GLA_QK = GLA_HEADS * GLA_KEY_DIM
GLA_V = GLA_HEADS * GLA_VAL_DIM
ATTN_COLS = ATTN_Q + 2 * ATTN_KV
GLA_COLS = 2 * GLA_QK + 2 * GLA_V
GATE_COLS = 2 * GLA_GATE_RANK
D_IN = ATTN_COLS + GLA_COLS + GATE_COLS

V7X_LANES = 128
V7X_SUBLANES = 8
V7X_VMEM_LIMIT_CAP = 56 * 1024 * 1024

NEG_BIG = -1e30

TOK_TILE = 512
GLA_TILE = 512
EXPERT_TILE = 1024
ROWS_PER_TILE = EXPERT_TILE // N_KEYS
KEY_CHUNK = 32
CHUNKS_PER_TILE = N_KEYS // KEY_CHUNK
N_RANKS = PEER_TOPK + 1
CAND_PAIRS = tuple((i, j) for i in range(N_RANKS) for j in range(N_RANKS) if (i + 1) * (j + 1) <= N_RANKS)


def _vmem_limit(block_bytes):
    return int(min(2 * block_bytes + (16 << 20), V7X_VMEM_LIMIT_CAP))


def _params(semantics, block_bytes, flags=None):
    return pltpu.CompilerParams(dimension_semantics=semantics, vmem_limit_bytes=_vmem_limit(block_bytes),
                                flags=flags)


def _nt_dot(a, b):
    return lax.dot_general(a, b, (((1,), (1,)), ((), ())), preferred_element_type=F32)


def _dot(a, b):
    return jnp.dot(a, b, preferred_element_type=F32)


def _rms(x, g):
    return x * lax.rsqrt(jnp.mean(x * x, axis=-1, keepdims=True) + EPS) * g


def _mod_kernel(c_ref, w_ref, b_ref, o_ref):
    c = c_ref[...]
    s = c * jax.nn.sigmoid(c)
    o_ref[...] = _dot(s.astype(BF16), w_ref[...].astype(BF16)) + b_ref[...]


def _modulation(c, w_ada, b_ada):
    bsz = c.shape[0]
    rows = -(-bsz // V7X_SUBLANES) * V7X_SUBLANES
    cp = jnp.zeros((rows, D_MODEL), F32).at[:bsz].set(c)
    n_out = w_ada.shape[1]
    tile = D_MODEL
    out = pl.pallas_call(
        _mod_kernel,
        grid=(n_out // tile,),
        in_specs=[pl.BlockSpec((rows, D_MODEL), lambda j: (0, 0)),
                  pl.BlockSpec((D_MODEL, tile), lambda j: (0, j)),
                  pl.BlockSpec((1, tile), lambda j: (0, j))],
        out_specs=pl.BlockSpec((rows, tile), lambda j: (0, j)),
        out_shape=jax.ShapeDtypeStruct((rows, n_out), F32),
        compiler_params=_params(("parallel",), D_MODEL * tile * 4),
        name="adaln_modulation",
    )(cp, w_ada, b_ada.reshape(1, n_out))
    return out[:bsz].reshape(bsz, 1, n_out)


def _inproj_kernel(x_ref, mod_ref, g_ref, w_ref, oa_ref, og_ref, or_ref):
    h = _rms(x_ref[...], g_ref[...])
    h = h * (1.0 + mod_ref[:, D_MODEL:2 * D_MODEL]) + mod_ref[:, 0:D_MODEL]
    p = _dot(h.astype(BF16), w_ref[...])
    oa_ref[...] = p[:, :ATTN_COLS]
    og_ref[...] = p[:, ATTN_COLS:ATTN_COLS + GLA_COLS]
    or_ref[...] = p[:, ATTN_COLS + GLA_COLS:]


def _in_projection(x, mod, norm_g, w_in_bf16):
    bsz, seq, _ = x.shape
    tb = TOK_TILE
    tok = lambda cols: pl.BlockSpec((None, tb, cols), lambda b, i: (b, i, 0))
    return pl.pallas_call(
        _inproj_kernel,
        grid=(bsz, seq // tb),
        in_specs=[tok(D_MODEL),
                  pl.BlockSpec((None, 1, 6 * D_MODEL), lambda b, i: (b, 0, 0)),
                  pl.BlockSpec((1, D_MODEL), lambda b, i: (0, 0)),
                  pl.BlockSpec((D_MODEL, D_IN), lambda b, i: (0, 0))],
        out_specs=[tok(ATTN_COLS), tok(GLA_COLS), tok(GATE_COLS)],
        out_shape=[jax.ShapeDtypeStruct((bsz, seq, ATTN_COLS), F32),
                   jax.ShapeDtypeStruct((bsz, seq, GLA_COLS), F32),
                   jax.ShapeDtypeStruct((bsz, seq, GATE_COLS), F32)],
        compiler_params=_params(("parallel", "parallel"),
                                tb * D_MODEL * 4 + D_MODEL * D_IN * 2 + 2 * tb * D_IN * 4),
        name="norm1_in_projection",
    )(x, mod, norm_g.reshape(1, D_MODEL), w_in_bf16)


def _t5_buckets(rel):
    nb = N_BUCKETS // 2
    ret = (rel > 0).astype(np.int32) * nb
    n = np.abs(rel)
    max_exact = nb // 2
    large = max_exact + (np.log(np.maximum(n, 1) / max_exact) / math.log(MAX_DISTANCE / max_exact)
                         * (nb - max_exact)).astype(np.int32)
    large = np.minimum(large, nb - 1)
    return (ret + np.where(n < max_exact, n, large)).astype(np.int32)


def _band_bias(rel_bias):
    span = 4 * BLOCK
    offsets = np.arange(span) - (2 * BLOCK - 1)
    per_offset = rel_bias[_t5_buckets(offsets)].astype(F32).T
    shifted = jnp.roll(per_offset, -(BLOCK - 1), axis=1)
    skew = jnp.tile(shifted, (1, BLOCK))[:, :BLOCK * (span - 1)].reshape(-1, BLOCK, span - 1)
    rel = np.arange(3 * BLOCK)[None, :] - BLOCK - np.arange(BLOCK)[:, None]
    band = jnp.asarray(np.abs(rel) <= WINDOW)
    return jnp.where(band[None], skew[:, :, :3 * BLOCK], NEG_BIG)


def _attn_kernel(sink_ref, q_ref, kvp_ref, kvc_ref, kvn_ref, bias_ref, qg_ref, kg_ref, beta_ref, o_ref):
    n = pl.program_id(1)
    last = pl.num_programs(1) - 1
    q = q_ref[...]
    kv = jnp.concatenate([kvp_ref[...], kvc_ref[...], kvn_ref[...]], axis=0)
    j = lax.broadcasted_iota(jnp.int32, (BLOCK, 3 * BLOCK), 1)
    valid = jnp.logical_and(jnp.logical_or(j >= BLOCK, n > 0), jnp.logical_or(j < 2 * BLOCK, n < last))
    scale = ATTN_HEAD_DIM ** -0.5
    outs = []
    for hk in range(ATTN_KV_HEADS):
        kh = _rms(kv[:, hk * ATTN_HEAD_DIM:(hk + 1) * ATTN_HEAD_DIM], kg_ref[...]).astype(BF16)
        vh = kv[:, ATTN_KV + hk * ATTN_HEAD_DIM:ATTN_KV + (hk + 1) * ATTN_HEAD_DIM].astype(BF16)
        for g in range(ATTN_GROUP):
            h = hk * ATTN_GROUP + g
            qh = _rms(q[:, h * ATTN_HEAD_DIM:(h + 1) * ATTN_HEAD_DIM], qg_ref[...]).astype(BF16)
            s = _nt_dot(qh, kh) * scale
            s = jnp.where(valid, s + bias_ref[h], NEG_BIG)
            sink = sink_ref[h]
            m = jnp.maximum(jnp.max(s, axis=-1, keepdims=True), sink)
            p = jnp.exp(s - m)
            den = jnp.sum(p, axis=-1, keepdims=True) + jnp.exp(sink - m)
            o = _dot(p.astype(BF16), vh) / den
            outs.append(_rms(o, beta_ref[:, h * ATTN_HEAD_DIM:(h + 1) * ATTN_HEAD_DIM]))
    o_ref[...] = jnp.concatenate(outs, axis=-1)


def _windowed_attention(qkv, bias, sink, q_norm_g, k_norm_g, beta_attn):
    bsz, seq, _ = qkv.shape
    nb = seq // BLOCK
    kv_cols = 2 * ATTN_KV
    kv_blk = ATTN_Q // kv_cols
    kv_spec = lambda f: pl.BlockSpec((None, BLOCK, kv_cols), lambda b, n: (b, f(n), kv_blk))
    const = lambda shape: pl.BlockSpec(shape, lambda b, n: tuple(0 for _ in shape))
    return pl.pallas_call(
        _attn_kernel,
        grid=(bsz, nb),
        in_specs=[pl.BlockSpec(memory_space=pltpu.SMEM),
                  pl.BlockSpec((None, BLOCK, ATTN_Q), lambda b, n: (b, n, 0)),
                  kv_spec(lambda n: jnp.maximum(n - 1, 0)),
                  kv_spec(lambda n: n),
                  kv_spec(lambda n: jnp.minimum(n + 1, nb - 1)),
                  const((ATTN_HEADS, BLOCK, 3 * BLOCK)),
                  const((1, ATTN_HEAD_DIM)), const((1, ATTN_HEAD_DIM)), const((1, ATTN_Q))],
        out_specs=pl.BlockSpec((None, BLOCK, ATTN_Q), lambda b, n: (b, n, 0)),
        out_shape=jax.ShapeDtypeStruct((bsz, seq, ATTN_Q), F32),
        compiler_params=_params(("parallel", "parallel"), ATTN_HEADS * BLOCK * 3 * BLOCK * 4 + 8 * BLOCK * ATTN_Q * 4),
        name="windowed_attention",
    )(sink, qkv, qkv, qkv, qkv, bias, q_norm_g.reshape(1, -1), k_norm_g.reshape(1, -1), beta_attn.reshape(1, -1))


def _split3(x):
    hi = x.astype(BF16)
    r = x - hi.astype(F32)
    mid = r.astype(BF16)
    lo = (r - mid.astype(F32)).astype(BF16)
    return hi, mid, lo


def _gla_chunk(q, k, v, gr, up, gbias, tri, eye, st_ref, reverse):
    c = GLA_CHUNK
    z = _dot(gr.astype(BF16), up) + gbias
    log_a = (jnp.minimum(z, 0.0) - jnp.log1p(jnp.exp(-jnp.abs(z)))) / GLA_GATE_NORM
    hi, mid, lo = _split3(log_a)
    b = _dot(tri, hi) + _dot(tri, mid) + _dot(tri, lo)
    b_last = b[0:1, :] if reverse else b[c - 1:c, :]
    q_dec = q * ((GLA_KEY_DIM ** -0.5) * jnp.exp(b))
    k_dec = (k * jnp.exp(-b)).astype(BF16)
    k_end = k * jnp.exp(b_last - b)
    decay = jnp.exp(b_last)
    row = lax.broadcasted_iota(jnp.int32, (c, c), 0)
    col = lax.broadcasted_iota(jnp.int32, (c, c), 1)
    causal = (col >= row) if reverse else (col <= row)
    lane = lax.broadcasted_iota(jnp.int32, (c, GLA_QK), 1)
    outs = []
    for h in range(GLA_HEADS):
        head = jnp.logical_and(lane >= h * GLA_KEY_DIM, lane < (h + 1) * GLA_KEY_DIM)
        qh = jnp.where(head, q_dec, 0.0).astype(BF16)
        keh = jnp.where(head, k_end, 0.0).astype(BF16)
        vh = v[:, h * GLA_VAL_DIM:(h + 1) * GLA_VAL_DIM].astype(BF16)
        a = jnp.where(causal, _nt_dot(qh, k_dec), 0.0)
        st = st_ref[h]
        outs.append(_dot(a.astype(BF16), vh) + _nt_dot(qh, st.astype(BF16)))
        vt = _nt_dot(eye, vh).astype(BF16)
        st_ref[h] = st * decay + _dot(vt, keh)
    return jnp.concatenate(outs, axis=-1)


def _gla_kernel(qkf_ref, vf_ref, grf_ref, qkb_ref, vb_ref, grb_ref, upf_ref, bf_ref, upb_ref, bb_ref,
                trif_ref, trib_ref, eye_ref, of_ref, ob_ref, stf_ref, stb_ref):
    @pl.when(pl.program_id(1) == 0)
    def _():
        stf_ref[...] = jnp.zeros_like(stf_ref)
        stb_ref[...] = jnp.zeros_like(stb_ref)

    nch = GLA_TILE // GLA_CHUNK

    for ci in range(nch):
        rows = slice(ci * GLA_CHUNK, (ci + 1) * GLA_CHUNK)
        of_ref[rows, :] = _gla_chunk(qkf_ref[rows, 0:GLA_QK], qkf_ref[rows, GLA_QK:2 * GLA_QK], vf_ref[rows, :],
                                     grf_ref[rows, 0:GLA_GATE_RANK], upf_ref[...], bf_ref[...],
                                     trif_ref[...], eye_ref[...], stf_ref, False)
        rows = slice((nch - 1 - ci) * GLA_CHUNK, (nch - ci) * GLA_CHUNK)
        ob_ref[rows, :] = _gla_chunk(qkb_ref[rows, 0:GLA_QK], qkb_ref[rows, GLA_QK:2 * GLA_QK], vb_ref[rows, :],
                                     grb_ref[rows, GLA_GATE_RANK:2 * GLA_GATE_RANK], upb_ref[...], bb_ref[...],
                                     trib_ref[...], eye_ref[...], stb_ref, True)


def _gla(gla_in, gates, up_f, b_f, up_b, b_b):
    bsz, seq, _ = gla_in.shape
    tb = GLA_TILE
    nblk = seq // tb
    fwd = lambda cols, cb: pl.BlockSpec((None, tb, cols), lambda b, i: (b, i, cb))
    bwd = lambda cols, cb: pl.BlockSpec((None, tb, cols), lambda b, i: (b, nblk - 1 - i, cb))
    const = lambda shape: pl.BlockSpec(shape, lambda b, i: tuple(0 for _ in shape))
    c = GLA_CHUNK
    tri_f = jnp.asarray(np.tril(np.ones((c, c), np.float32)), BF16)
    tri_b = jnp.asarray(np.triu(np.ones((c, c), np.float32)), BF16)
    eye = jnp.asarray(np.eye(GLA_VAL_DIM, dtype=np.float32), BF16)
    state = pltpu.VMEM((GLA_HEADS, GLA_VAL_DIM, GLA_QK), F32)
    return pl.pallas_call(
        _gla_kernel,
        grid=(bsz, nblk),
        in_specs=[fwd(2 * GLA_QK, 0), fwd(GLA_V, 1), fwd(GATE_COLS, 0),
                  bwd(2 * GLA_QK, 0), bwd(GLA_V, 1), bwd(GATE_COLS, 0),
                  const((GLA_GATE_RANK, GLA_QK)), const((1, GLA_QK)),
                  const((GLA_GATE_RANK, GLA_QK)), const((1, GLA_QK)),
                  const((c, c)), const((c, c)), const((GLA_VAL_DIM, GLA_VAL_DIM))],
        out_specs=[pl.BlockSpec((None, tb, GLA_V), lambda b, i: (b, i, 0)),
                   pl.BlockSpec((None, tb, GLA_V), lambda b, i: (b, nblk - 1 - i, 0))],
        out_shape=[jax.ShapeDtypeStruct((bsz, seq, GLA_V), F32)] * 2,
        scratch_shapes=[state, state],
        compiler_params=_params(("parallel", "arbitrary"), 2 * tb * (2 * GLA_QK + 2 * GLA_V + V7X_LANES) * 4),
        name="gla_bidirectional",
    )(gla_in, gla_in, gates, gla_in, gla_in, gates,
      up_f.astype(BF16), b_f.reshape(1, -1), up_b.astype(BF16), b_b.reshape(1, -1), tri_f, tri_b, eye)


def _outproj_kernel(att_ref, of_ref, ob_ref, og_ref, x_ref, mod_ref, beta_ref, w_ref, g2_ref, x1_ref, h2_ref):
    o = of_ref[...] + ob_ref[...]
    gate = og_ref[...]
    gate = gate * jax.nn.sigmoid(gate)
    parts = [att_ref[...].astype(BF16)]
    for h in range(GLA_HEADS):
        sl = slice(h * GLA_VAL_DIM, (h + 1) * GLA_VAL_DIM)
        parts.append((_rms(o[:, sl], beta_ref[:, sl]) * gate[:, sl]).astype(BF16))
    mix = _dot(jnp.concatenate(parts, axis=-1), w_ref[...])
    d = D_MODEL
    x1 = x_ref[...] + mod_ref[:, 2 * d:3 * d] * mix
    x1_ref[...] = x1
    h2 = _rms(x1, g2_ref[...]) * (1.0 + mod_ref[:, 4 * d:5 * d]) + mod_ref[:, 3 * d:4 * d]
    h2_ref[...] = h2.astype(BF16)


def _out_projection(att, o_f, o_b, gla_in, x, mod, beta_gla, w_out_bf16, norm2_g):
    bsz, seq, _ = x.shape
    tb = TOK_TILE
    tok = lambda cols, cb=0: pl.BlockSpec((None, tb, cols), lambda b, i: (b, i, cb))
    const = lambda shape: pl.BlockSpec(shape, lambda b, i: tuple(0 for _ in shape))
    return pl.pallas_call(
        _outproj_kernel,
        grid=(bsz, seq // tb),
        in_specs=[tok(ATTN_Q), tok(GLA_V), tok(GLA_V), tok(GLA_V, (2 * GLA_QK + GLA_V) // GLA_V), tok(D_MODEL),
                  pl.BlockSpec((None, 1, 6 * D_MODEL), lambda b, i: (b, 0, 0)),
                  const((1, GLA_V)), const((D_MODEL, D_MODEL)), const((1, D_MODEL))],
        out_specs=[tok(D_MODEL), tok(D_MODEL)],
        out_shape=[jax.ShapeDtypeStruct((bsz, seq, D_MODEL), F32),
                   jax.ShapeDtypeStruct((bsz, seq, D_MODEL), BF16)],
        compiler_params=_params(("parallel", "parallel"), tb * (4 * GLA_V + 3 * D_MODEL) * 4 + D_MODEL * D_MODEL * 2),
        name="out_projection_norm2",
    )(att, o_f, o_b, gla_in, x, mod, beta_gla.reshape(1, -1), w_out_bf16, norm2_g.reshape(1, -1))


def _top_values(x, n, out_ref, head):
    for r in range(n):
        m = jnp.max(x, axis=0, keepdims=True)
        out_ref[r, pl.ds(head, 1), :] = m
        x = jnp.where(x == m, -jnp.inf, x)


def _peer_route_kernel(h2_ref, wq_ref, keys_ref, h2t_ref, s2_ref, e2_ref, thr_ref, c_ref, sc_ref, a_ref, b_ref):
    h2 = h2_ref[...]
    h2t_ref[...] = h2.astype(F32).T.astype(BF16)
    q = _dot(h2, wq_ref[...]).astype(BF16)
    for h in range(PEER_HEADS):
        for p in range(2):
            blk = h * 2 + p
            s = _nt_dot(keys_ref[h, p], q[:, blk * PEER_HALF:(blk + 1) * PEER_HALF])
            sc_ref[blk] = s
            _top_values(s, N_RANKS, a_ref if p == 0 else b_ref, h)
    cand = [a_ref[i] + b_ref[j] for (i, j) in CAND_PAIRS]
    vals = []
    for r in range(N_RANKS):
        m = functools.reduce(jnp.maximum, cand)
        vals.append(m)
        cand = [jnp.where(x == m, -jnp.inf, x) for x in cand]
    top = vals[0]
    thr = 0.5 * (vals[PEER_TOPK - 1] + vals[PEER_TOPK])
    z = functools.reduce(jnp.add, [jnp.exp(v - top) for v in vals[:PEER_TOPK]])
    inv_z = 1.0 / z
    a0 = a_ref[0]
    b0 = b_ref[0]
    for h in range(PEER_HEADS):
        s1 = sc_ref[2 * h]
        s2 = sc_ref[2 * h + 1]
        thr_ref[h] = thr[h:h + 1, :] - s1
        c_ref[h] = jnp.exp(s1 - a0[h:h + 1, :]) * inv_z[h:h + 1, :]
        s2_ref[h] = s2
        e2_ref[h] = jnp.exp(s2 - b0[h:h + 1, :])


def _peer_route(h2, wq_bf16, keys_bf16):
    bsz, seq, _ = h2.shape
    tb = TOK_TILE
    per_head = lambda: pl.BlockSpec((None, PEER_HEADS, N_KEYS, tb), lambda b, i: (b, 0, 0, i))
    head_shape = jax.ShapeDtypeStruct((bsz, PEER_HEADS, N_KEYS, seq), F32)
    qcols = PEER_HEADS * PEER_KEY_DIM
    return pl.pallas_call(
        _peer_route_kernel,
        grid=(bsz, seq // tb),
        in_specs=[pl.BlockSpec((None, tb, D_MODEL), lambda b, i: (b, i, 0)),
                  pl.BlockSpec((D_MODEL, qcols), lambda b, i: (0, 0)),
                  pl.BlockSpec((PEER_HEADS, 2, N_KEYS, PEER_HALF), lambda b, i: (0, 0, 0, 0))],
        out_specs=[pl.BlockSpec((None, D_MODEL, tb), lambda b, i: (b, 0, i)),
                   per_head(), per_head(), per_head(), per_head()],
        out_shape=[jax.ShapeDtypeStruct((bsz, D_MODEL, seq), BF16), head_shape, head_shape, head_shape, head_shape],
        scratch_shapes=[pltpu.VMEM((2 * PEER_HEADS, N_KEYS, tb), F32),
                        pltpu.VMEM((N_RANKS, PEER_HEADS, tb), F32),
                        pltpu.VMEM((N_RANKS, PEER_HEADS, tb), F32)],
        compiler_params=_params(("parallel", "parallel"),
                                D_MODEL * qcols * 2 + 5 * PEER_HEADS * N_KEYS * tb * 4 + tb * qcols * 4),
        name="peer_routing",
    )(h2, wq_bf16, keys_bf16)


def _gelu(x):
    return 0.5 * x * (1.0 + lax.erf(x * (2.0 ** -0.5)))


def _routing_block(s2_ref, e2_ref, thr_ref, c_ref, act_ref, wg_ref, t, chunk):
    cols = pl.ds(pl.multiple_of(t * V7X_LANES, V7X_LANES), V7X_LANES)
    k0 = pl.multiple_of(chunk * KEY_CHUNK, KEY_CHUNK)
    keys = pl.ds(k0, KEY_CHUNK)
    w = [jnp.zeros((KEY_CHUNK, V7X_LANES), F32) for _ in range(ROWS_PER_TILE)]
    for h in range(PEER_HEADS):
        s2 = s2_ref[h, keys, cols]
        e2 = e2_ref[h, keys, cols]
        for r in range(ROWS_PER_TILE):
            thr = thr_ref[h, r:r + 1, cols]
            coef = c_ref[h, r:r + 1, cols]
            w[r] = w[r] + jnp.where(s2 >= thr, e2 * coef, 0.0)
    for r in range(ROWS_PER_TILE):
        rows = pl.ds(pl.multiple_of(r * N_KEYS + k0, KEY_CHUNK), KEY_CHUNK)
        wg_ref[rows, cols] = (w[r] * _gelu(act_ref[rows, cols])).astype(BF16)


def _peer_dense_kernel(h2t_ref, u_ref, vt_ref, s2_ref, e2_ref, thr_ref, c_ref, x1_ref, mod_ref, y_ref,
                       act_ref, wg_ref, acc_ref):
    j = pl.program_id(2)
    tb = acc_ref.shape[1]

    @pl.when(j == 0)
    def _():
        acc_ref[...] = jnp.zeros_like(acc_ref)

    act_ref[...] = _dot(u_ref[...], h2t_ref[...])

    def block(idx, carry):
        _routing_block(s2_ref, e2_ref, thr_ref, c_ref, act_ref, wg_ref, idx // CHUNKS_PER_TILE, idx % CHUNKS_PER_TILE)
        return carry

    lax.fori_loop(0, (tb // V7X_LANES) * CHUNKS_PER_TILE, block, 0)
    acc_ref[...] += _dot(vt_ref[...], wg_ref[...])

    @pl.when(j == pl.num_programs(2) - 1)
    def _():
        y_ref[...] = x1_ref[...] + mod_ref[:, 5 * D_MODEL:6 * D_MODEL] * acc_ref[...].T


def _peer_dense(h2t, u_bf16, vt_bf16, s2, e2, thr, coef, x1, mod):
    bsz, seq, _ = x1.shape
    tb = TOK_TILE
    n_exp = u_bf16.shape[0]
    per_head = lambda: pl.BlockSpec((None, PEER_HEADS, N_KEYS, tb), lambda b, i, j: (b, 0, 0, i))
    per_row = lambda: pl.BlockSpec((None, PEER_HEADS, ROWS_PER_TILE, tb), lambda b, i, j: (b, 0, j, i))
    return pl.pallas_call(
        _peer_dense_kernel,
        grid=(bsz, seq // tb, n_exp // EXPERT_TILE),
        in_specs=[pl.BlockSpec((None, D_MODEL, tb), lambda b, i, j: (b, 0, i)),
                  pl.BlockSpec((EXPERT_TILE, D_MODEL), lambda b, i, j: (j, 0)),
                  pl.BlockSpec((D_MODEL, EXPERT_TILE), lambda b, i, j: (0, j)),
                  per_head(), per_head(), per_row(), per_row(),
                  pl.BlockSpec((None, tb, D_MODEL), lambda b, i, j: (b, i, 0)),
                  pl.BlockSpec((None, 1, 6 * D_MODEL), lambda b, i, j: (b, 0, 0))],
        out_specs=pl.BlockSpec((None, tb, D_MODEL), lambda b, i, j: (b, i, 0)),
        out_shape=jax.ShapeDtypeStruct((bsz, seq, D_MODEL), F32),
        scratch_shapes=[pltpu.VMEM((EXPERT_TILE, tb), F32),
                        pltpu.VMEM((EXPERT_TILE, tb), BF16),
                        pltpu.VMEM((D_MODEL, tb), F32)],
        compiler_params=_params(("parallel", "parallel", "arbitrary"),
                                2 * PEER_HEADS * N_KEYS * tb * 4 + 2 * EXPERT_TILE * D_MODEL * 2
                                + 2 * tb * D_MODEL * 4 + 3 * EXPERT_TILE * tb * 4),
        name="peer_dense_experts",
    )(h2t, u_bf16, vt_bf16, s2, e2, thr, coef, x1, mod)


def _layer(x, c, w_ada, b_ada, norm1_g, norm2_g, w_in, q_norm_g, k_norm_g, bias, sink,
           up_f, b_f, up_b, b_b, beta_attn, beta_gla, w_out, wq, keys, u, vt):
    mod = _modulation(c, w_ada, b_ada)
    qkv, gla_in, gates = _in_projection(x, mod, norm1_g, w_in)
    att = _windowed_attention(qkv, bias, sink, q_norm_g, k_norm_g, beta_attn)
    o_f, o_b = _gla(gla_in, gates, up_f, b_f, up_b, b_b)
    x1, h2 = _out_projection(att, o_f, o_b, gla_in, x, mod, beta_gla, w_out, norm2_g)
    h2t, s2, e2, thr, coef = _peer_route(h2, wq, keys)
    return _peer_dense(h2t, u, vt, s2, e2, thr, coef, x1, mod)


def kernel(x_prompt, x_sample, c_prompt, c_sample, w_ada, b_ada, norm1_g, norm2_g, w_in, q_norm_g, k_norm_g,
           rel_bias, sink, gk_up_fwd, gk_bias_fwd, gk_up_bwd, gk_bias_bwd, beta_attn, beta_gla, w_out,
           peer_query, peer_subkeys, peer_u, peer_v):
    depth = w_ada.shape[0]
    bias = _band_bias(rel_bias)
    y_prompt, y_sample = x_prompt, x_sample
    for l in range(depth):
        params = (w_ada[l], b_ada[l], norm1_g[l], norm2_g[l], w_in[l].astype(BF16), q_norm_g[l], k_norm_g[l],
                  bias, sink[l], gk_up_fwd[l], gk_bias_fwd[l], gk_up_bwd[l], gk_bias_bwd[l],
                  beta_attn[l], beta_gla[l], w_out[l].astype(BF16), peer_query[l].astype(BF16),
                  peer_subkeys[l].astype(BF16), peer_u[l].astype(BF16), peer_v[l].T.astype(BF16))
        y_prompt = _layer(y_prompt, c_prompt, *params)
        y_sample = _layer(y_sample, c_sample, *params)
    return (y_prompt, y_sample)
```

```python
import functools
import math

import numpy as np
import jax
import jax.numpy as jnp
from jax import lax
from jax.experimental import pallas as pl
from jax.experimental.pallas import tpu as pltpu

F32 = jnp.float32
BF16 = jnp.bfloat16

D_MODEL = 1024
ATTN_HEADS = 8
ATTN_KV_HEADS = 2
ATTN_HEAD_DIM = 64
ATTN_GROUP = ATTN_HEADS // ATTN_KV_HEADS
WINDOW = 128
BLOCK = 128
N_BUCKETS = 32
MAX_DISTANCE = 128
GLA_HEADS = 4
GLA_KEY_DIM = 64
GLA_VAL_DIM = 128
GLA_GATE_RANK = 16
GLA_GATE_NORM = 16.0
GLA_CHUNK = 64
PEER_HEADS = 8
N_KEYS = 128
PEER_KEY_DIM = 256
PEER_HALF = PEER_KEY_DIM // 2
PEER_TOPK = 16
EPS = 1e-6

ATTN_Q = ATTN_HEADS * ATTN_HEAD_DIM
ATTN_KV = ATTN_KV_HEADS * ATTN_HEAD_DIM
GLA_QK = GLA_HEADS * GLA_KEY_DIM
GLA_V = GLA_HEADS * GLA_VAL_DIM
ATTN_COLS = ATTN_Q + 2 * ATTN_KV
GLA_COLS = 2 * GLA_QK + 2 * GLA_V
GATE_COLS = 2 * GLA_GATE_RANK
D_IN = ATTN_COLS + GLA_COLS + GATE_COLS

V7X_LANES = 128
V7X_SUBLANES = 8
V7X_MXU_DEPTH = 256
V7X_VMEM_LIMIT_CAP = 60 * 1024 * 1024

NEG_BIG = -1e30

TOK_TILE = 512
GLA_TILE = 512
EXPERT_TILE = 1024
ROWS_PER_TILE = EXPERT_TILE // N_KEYS
KEY_CHUNK = 32
CHUNKS_PER_TILE = N_KEYS // KEY_CHUNK
MXU_SLAB = EXPERT_TILE // CHUNKS_PER_TILE
N_RANKS = PEER_TOPK + 1
CAND_PAIRS = tuple((i, j) for i in range(N_RANKS) for j in range(N_RANKS) if (i + 1) * (j + 1) <= N_RANKS)


def _vmem_limit(block_bytes):
    return int(min(2 * block_bytes + (16 << 20), V7X_VMEM_LIMIT_CAP))


def _params(semantics, block_bytes, flags=None):
    return pltpu.CompilerParams(dimension_semantics=semantics, vmem_limit_bytes=_vmem_limit(block_bytes),
                                flags=flags)


def _nt_dot(a, b):
    return lax.dot_general(a, b, (((1,), (1,)), ((), ())), preferred_element_type=F32)


def _dot(a, b):
    return jnp.dot(a, b, preferred_element_type=F32)


def _rms(x, g):
    return x * lax.rsqrt(jnp.mean(x * x, axis=-1, keepdims=True) + EPS) * g


def _mod_kernel(c_ref, w_ref, b_ref, o_ref):
    c = c_ref[...]
    s = c * jax.nn.sigmoid(c)
    o_ref[...] = _dot(s.astype(BF16), w_ref[...].astype(BF16)) + b_ref[...]


def _modulation(c, w_ada, b_ada):
    bsz = c.shape[0]
    rows = -(-bsz // V7X_SUBLANES) * V7X_SUBLANES
    cp = jnp.zeros((rows, D_MODEL), F32).at[:bsz].set(c)
    n_out = w_ada.shape[1]
    tile = D_MODEL
    out = pl.pallas_call(
        _mod_kernel,
        grid=(n_out // tile,),
        in_specs=[pl.BlockSpec((rows, D_MODEL), lambda j: (0, 0)),
                  pl.BlockSpec((D_MODEL, tile), lambda j: (0, j)),
                  pl.BlockSpec((1, tile), lambda j: (0, j))],
        out_specs=pl.BlockSpec((rows, tile), lambda j: (0, j)),
        out_shape=jax.ShapeDtypeStruct((rows, n_out), F32),
        compiler_params=_params(("parallel",), D_MODEL * tile * 4),
        name="adaln_modulation",
    )(cp, w_ada, b_ada.reshape(1, n_out))
    return out[:bsz].reshape(bsz, 1, n_out)


def _inproj_kernel(x_ref, mod_ref, g_ref, w_ref, oa_ref, og_ref, or_ref):
    h = _rms(x_ref[...], g_ref[...])
    h = h * (1.0 + mod_ref[:, D_MODEL:2 * D_MODEL]) + mod_ref[:, 0:D_MODEL]
    p = _dot(h.astype(BF16), w_ref[...])
    oa_ref[...] = p[:, :ATTN_COLS]
    og_ref[...] = p[:, ATTN_COLS:ATTN_COLS + GLA_COLS]
    or_ref[...] = p[:, ATTN_COLS + GLA_COLS:]


def _in_projection(x, mod, norm_g, w_in_bf16):
    bsz, seq, _ = x.shape
    tb = TOK_TILE
    tok = lambda cols: pl.BlockSpec((None, tb, cols), lambda b, i: (b, i, 0))
    return pl.pallas_call(
        _inproj_kernel,
        grid=(bsz, seq // tb),
        in_specs=[tok(D_MODEL),
                  pl.BlockSpec((None, 1, 6 * D_MODEL), lambda b, i: (b, 0, 0)),
                  pl.BlockSpec((1, D_MODEL), lambda b, i: (0, 0)),
                  pl.BlockSpec((D_MODEL, D_IN), lambda b, i: (0, 0))],
        out_specs=[tok(ATTN_COLS), tok(GLA_COLS), tok(GATE_COLS)],
        out_shape=[jax.ShapeDtypeStruct((bsz, seq, ATTN_COLS), F32),
                   jax.ShapeDtypeStruct((bsz, seq, GLA_COLS), F32),
                   jax.ShapeDtypeStruct((bsz, seq, GATE_COLS), F32)],
        compiler_params=_params(("parallel", "parallel"),
                                tb * D_MODEL * 4 + D_MODEL * D_IN * 2 + 2 * tb * D_IN * 4),
        name="norm1_in_projection",
    )(x, mod, norm_g.reshape(1, D_MODEL), w_in_bf16)


def _t5_buckets(rel):
    nb = N_BUCKETS // 2
    ret = (rel > 0).astype(np.int32) * nb
    n = np.abs(rel)
    max_exact = nb // 2
    large = max_exact + (np.log(np.maximum(n, 1) / max_exact) / math.log(MAX_DISTANCE / max_exact)
                         * (nb - max_exact)).astype(np.int32)
    large = np.minimum(large, nb - 1)
    return (ret + np.where(n < max_exact, n, large)).astype(np.int32)


def _band_bias(rel_bias):
    span = 4 * BLOCK
    offsets = np.arange(span) - (2 * BLOCK - 1)
    per_offset = rel_bias[_t5_buckets(offsets)].astype(F32).T
    shifted = jnp.roll(per_offset, -(BLOCK - 1), axis=1)
    skew = jnp.tile(shifted, (1, BLOCK))[:, :BLOCK * (span - 1)].reshape(-1, BLOCK, span - 1)
    rel = np.arange(3 * BLOCK)[None, :] - BLOCK - np.arange(BLOCK)[:, None]
    band = jnp.asarray(np.abs(rel) <= WINDOW)
    return jnp.where(band[None], skew[:, :, :3 * BLOCK], NEG_BIG)


def _attn_kernel(sink_ref, q_ref, kvp_ref, kvc_ref, kvn_ref, bias_ref, qg_ref, kg_ref, beta_ref, o_ref):
    n = pl.program_id(1)
    last = pl.num_programs(1) - 1
    q = q_ref[...]
    kv = jnp.concatenate([kvp_ref[...], kvc_ref[...], kvn_ref[...]], axis=0)
    j = lax.broadcasted_iota(jnp.int32, (BLOCK, 3 * BLOCK), 1)
    valid = jnp.logical_and(jnp.logical_or(j >= BLOCK, n > 0), jnp.logical_or(j < 2 * BLOCK, n < last))
    scale = ATTN_HEAD_DIM ** -0.5
    outs = []
    for hk in range(ATTN_KV_HEADS):
        kh = _rms(kv[:, hk * ATTN_HEAD_DIM:(hk + 1) * ATTN_HEAD_DIM], kg_ref[...]).astype(BF16)
        vh = kv[:, ATTN_KV + hk * ATTN_HEAD_DIM:ATTN_KV + (hk + 1) * ATTN_HEAD_DIM].astype(BF16)
        for g in range(ATTN_GROUP):
            h = hk * ATTN_GROUP + g
            qh = _rms(q[:, h * ATTN_HEAD_DIM:(h + 1) * ATTN_HEAD_DIM], qg_ref[...]).astype(BF16)
            s = _nt_dot(qh, kh) * scale
            s = jnp.where(valid, s + bias_ref[h], NEG_BIG)
            sink = sink_ref[h]
            m = jnp.maximum(jnp.max(s, axis=-1, keepdims=True), sink)
            p = jnp.exp(s - m)
            den = jnp.sum(p, axis=-1, keepdims=True) + jnp.exp(sink - m)
            o = _dot(p.astype(BF16), vh) / den
            outs.append(_rms(o, beta_ref[:, h * ATTN_HEAD_DIM:(h + 1) * ATTN_HEAD_DIM]))
    o_ref[...] = jnp.concatenate(outs, axis=-1)


def _windowed_attention(qkv, bias, sink, q_norm_g, k_norm_g, beta_attn):
    bsz, seq, _ = qkv.shape
    nb = seq // BLOCK
    kv_cols = 2 * ATTN_KV
    kv_blk = ATTN_Q // kv_cols
    kv_spec = lambda f: pl.BlockSpec((None, BLOCK, kv_cols), lambda b, n: (b, f(n), kv_blk))
    const = lambda shape: pl.BlockSpec(shape, lambda b, n: tuple(0 for _ in shape))
    return pl.pallas_call(
        _attn_kernel,
        grid=(bsz, nb),
        in_specs=[pl.BlockSpec(memory_space=pltpu.SMEM),
                  pl.BlockSpec((None, BLOCK, ATTN_Q), lambda b, n: (b, n, 0)),
                  kv_spec(lambda n: jnp.maximum(n - 1, 0)),
                  kv_spec(lambda n: n),
                  kv_spec(lambda n: jnp.minimum(n + 1, nb - 1)),
                  const((ATTN_HEADS, BLOCK, 3 * BLOCK)),
                  const((1, ATTN_HEAD_DIM)), const((1, ATTN_HEAD_DIM)), const((1, ATTN_Q))],
        out_specs=pl.BlockSpec((None, BLOCK, ATTN_Q), lambda b, n: (b, n, 0)),
        out_shape=jax.ShapeDtypeStruct((bsz, seq, ATTN_Q), F32),
        compiler_params=_params(("parallel", "parallel"), ATTN_HEADS * BLOCK * 3 * BLOCK * 4 + 8 * BLOCK * ATTN_Q * 4),
        name="windowed_attention",
    )(sink, qkv, qkv, qkv, qkv, bias, q_norm_g.reshape(1, -1), k_norm_g.reshape(1, -1), beta_attn.reshape(1, -1))


def _split3(x):
    hi = x.astype(BF16)
    r = x - hi.astype(F32)
    mid = r.astype(BF16)
    lo = (r - mid.astype(F32)).astype(BF16)
    return hi, mid, lo


def _gla_chunk(q, k, v, gr, up, gbias, tri, eye, st_ref, reverse):
    c = GLA_CHUNK
    z = _dot(gr.astype(BF16), up) + gbias
    log_a = (jnp.minimum(z, 0.0) - jnp.log1p(jnp.exp(-jnp.abs(z)))) / GLA_GATE_NORM
    hi, mid, lo = _split3(log_a)
    b = _dot(tri, hi) + _dot(tri, mid) + _dot(tri, lo)
    b_last = b[0:1, :] if reverse else b[c - 1:c, :]
    q_dec = q * ((GLA_KEY_DIM ** -0.5) * jnp.exp(b))
    k_dec = (k * jnp.exp(-b)).astype(BF16)
    k_end = k * jnp.exp(b_last - b)
    decay = jnp.exp(b_last)
    row = lax.broadcasted_iota(jnp.int32, (c, c), 0)
    col = lax.broadcasted_iota(jnp.int32, (c, c), 1)
    causal = (col >= row) if reverse else (col <= row)
    lane = lax.broadcasted_iota(jnp.int32, (c, GLA_QK), 1)
    outs = []
    for h in range(GLA_HEADS):
        head = jnp.logical_and(lane >= h * GLA_KEY_DIM, lane < (h + 1) * GLA_KEY_DIM)
        qh = jnp.where(head, q_dec, 0.0).astype(BF16)
        keh = jnp.where(head, k_end, 0.0).astype(BF16)
        vh = v[:, h * GLA_VAL_DIM:(h + 1) * GLA_VAL_DIM].astype(BF16)
        a = jnp.where(causal, _nt_dot(qh, k_dec), 0.0)
        st = st_ref[h]
        outs.append(_dot(a.astype(BF16), vh) + _nt_dot(qh, st.astype(BF16)))
        vt = _nt_dot(eye, vh).astype(BF16)
        st_ref[h] = st * decay + _dot(vt, keh)
    return jnp.concatenate(outs, axis=-1)


def _gla_kernel(qkf_ref, vf_ref, grf_ref, qkb_ref, vb_ref, grb_ref, upf_ref, bf_ref, upb_ref, bb_ref,
                trif_ref, trib_ref, eye_ref, of_ref, ob_ref, stf_ref, stb_ref):
    @pl.when(pl.program_id(1) == 0)
    def _():
        stf_ref[...] = jnp.zeros_like(stf_ref)
        stb_ref[...] = jnp.zeros_like(stb_ref)

    nch = GLA_TILE // GLA_CHUNK

    for ci in range(nch):
        rows = slice(ci * GLA_CHUNK, (ci + 1) * GLA_CHUNK)
        of_ref[rows, :] = _gla_chunk(qkf_ref[rows, 0:GLA_QK], qkf_ref[rows, GLA_QK:2 * GLA_QK], vf_ref[rows, :],
                                     grf_ref[rows, 0:GLA_GATE_RANK], upf_ref[...], bf_ref[...],
                                     trif_ref[...], eye_ref[...], stf_ref, False)
        rows = slice((nch - 1 - ci) * GLA_CHUNK, (nch - ci) * GLA_CHUNK)
        ob_ref[rows, :] = _gla_chunk(qkb_ref[rows, 0:GLA_QK], qkb_ref[rows, GLA_QK:2 * GLA_QK], vb_ref[rows, :],
                                     grb_ref[rows, GLA_GATE_RANK:2 * GLA_GATE_RANK], upb_ref[...], bb_ref[...],
                                     trib_ref[...], eye_ref[...], stb_ref, True)


def _gla(gla_in, gates, up_f, b_f, up_b, b_b):
    bsz, seq, _ = gla_in.shape
    tb = GLA_TILE
    nblk = seq // tb
    fwd = lambda cols, cb: pl.BlockSpec((None, tb, cols), lambda b, i: (b, i, cb))
    bwd = lambda cols, cb: pl.BlockSpec((None, tb, cols), lambda b, i: (b, nblk - 1 - i, cb))
    const = lambda shape: pl.BlockSpec(shape, lambda b, i: tuple(0 for _ in shape))
    c = GLA_CHUNK
    tri_f = jnp.asarray(np.tril(np.ones((c, c), np.float32)), BF16)
    tri_b = jnp.asarray(np.triu(np.ones((c, c), np.float32)), BF16)
    eye = jnp.asarray(np.eye(GLA_VAL_DIM, dtype=np.float32), BF16)
    state = pltpu.VMEM((GLA_HEADS, GLA_VAL_DIM, GLA_QK), F32)
    return pl.pallas_call(
        _gla_kernel,
        grid=(bsz, nblk),
        in_specs=[fwd(2 * GLA_QK, 0), fwd(GLA_V, 1), fwd(GATE_COLS, 0),
                  bwd(2 * GLA_QK, 0), bwd(GLA_V, 1), bwd(GATE_COLS, 0),
                  const((GLA_GATE_RANK, GLA_QK)), const((1, GLA_QK)),
                  const((GLA_GATE_RANK, GLA_QK)), const((1, GLA_QK)),
                  const((c, c)), const((c, c)), const((GLA_VAL_DIM, GLA_VAL_DIM))],
        out_specs=[pl.BlockSpec((None, tb, GLA_V), lambda b, i: (b, i, 0)),
                   pl.BlockSpec((None, tb, GLA_V), lambda b, i: (b, nblk - 1 - i, 0))],
        out_shape=[jax.ShapeDtypeStruct((bsz, seq, GLA_V), F32)] * 2,
        scratch_shapes=[state, state],
        compiler_params=_params(("parallel", "arbitrary"), 2 * tb * (2 * GLA_QK + 2 * GLA_V + V7X_LANES) * 4),
        name="gla_bidirectional",
    )(gla_in, gla_in, gates, gla_in, gla_in, gates,
      up_f.astype(BF16), b_f.reshape(1, -1), up_b.astype(BF16), b_b.reshape(1, -1), tri_f, tri_b, eye)


def _outproj_kernel(att_ref, of_ref, ob_ref, og_ref, x_ref, mod_ref, beta_ref, w_ref, g2_ref, x1_ref, h2_ref):
    o = of_ref[...] + ob_ref[...]
    gate = og_ref[...]
    gate = gate * jax.nn.sigmoid(gate)
    parts = [att_ref[...].astype(BF16)]
    for h in range(GLA_HEADS):
        sl = slice(h * GLA_VAL_DIM, (h + 1) * GLA_VAL_DIM)
        parts.append((_rms(o[:, sl], beta_ref[:, sl]) * gate[:, sl]).astype(BF16))
    mix = _dot(jnp.concatenate(parts, axis=-1), w_ref[...])
    d = D_MODEL
    x1 = x_ref[...] + mod_ref[:, 2 * d:3 * d] * mix
    x1_ref[...] = x1
    h2 = _rms(x1, g2_ref[...]) * (1.0 + mod_ref[:, 4 * d:5 * d]) + mod_ref[:, 3 * d:4 * d]
    h2_ref[...] = h2.astype(BF16)


def _out_projection(att, o_f, o_b, gla_in, x, mod, beta_gla, w_out_bf16, norm2_g):
    bsz, seq, _ = x.shape
    tb = TOK_TILE
    tok = lambda cols, cb=0: pl.BlockSpec((None, tb, cols), lambda b, i: (b, i, cb))
    const = lambda shape: pl.BlockSpec(shape, lambda b, i: tuple(0 for _ in shape))
    return pl.pallas_call(
        _outproj_kernel,
        grid=(bsz, seq // tb),
        in_specs=[tok(ATTN_Q), tok(GLA_V), tok(GLA_V), tok(GLA_V, (2 * GLA_QK + GLA_V) // GLA_V), tok(D_MODEL),
                  pl.BlockSpec((None, 1, 6 * D_MODEL), lambda b, i: (b, 0, 0)),
                  const((1, GLA_V)), const((D_MODEL, D_MODEL)), const((1, D_MODEL))],
        out_specs=[tok(D_MODEL), tok(D_MODEL)],
        out_shape=[jax.ShapeDtypeStruct((bsz, seq, D_MODEL), F32),
                   jax.ShapeDtypeStruct((bsz, seq, D_MODEL), BF16)],
        compiler_params=_params(("parallel", "parallel"), tb * (4 * GLA_V + 3 * D_MODEL) * 4 + D_MODEL * D_MODEL * 2),
        name="out_projection_norm2",
    )(att, o_f, o_b, gla_in, x, mod, beta_gla.reshape(1, -1), w_out_bf16, norm2_g.reshape(1, -1))


def _top_values(x, n, out_ref, head):
    for r in range(n):
        m = jnp.max(x, axis=0, keepdims=True)
        out_ref[r, pl.ds(head, 1), :] = m
        x = jnp.where(x == m, -jnp.inf, x)


def _peer_route_kernel(h2_ref, wq_ref, keys_ref, h2t_ref, s2_ref, e2_ref, thr_ref, c_ref, sc_ref, a_ref, b_ref):
    h2 = h2_ref[...]
    h2t_ref[...] = h2.astype(F32).T.astype(BF16)
    q = _dot(h2, wq_ref[...]).astype(BF16)
    for h in range(PEER_HEADS):
        for p in range(2):
            blk = h * 2 + p
            s = _nt_dot(keys_ref[h, p], q[:, blk * PEER_HALF:(blk + 1) * PEER_HALF])
            sc_ref[blk] = s
            _top_values(s, N_RANKS, a_ref if p == 0 else b_ref, h)
    cand = [a_ref[i] + b_ref[j] for (i, j) in CAND_PAIRS]
    vals = []
    for r in range(N_RANKS):
        m = functools.reduce(jnp.maximum, cand)
        vals.append(m)
        cand = [jnp.where(x == m, -jnp.inf, x) for x in cand]
    top = vals[0]
    thr = 0.5 * (vals[PEER_TOPK - 1] + vals[PEER_TOPK])
    z = functools.reduce(jnp.add, [jnp.exp(v - top) for v in vals[:PEER_TOPK]])
    inv_z = 1.0 / z
    a0 = a_ref[0]
    b0 = b_ref[0]
    for h in range(PEER_HEADS):
        s1 = sc_ref[2 * h]
        s2 = sc_ref[2 * h + 1]
        thr_ref[h] = thr[h:h + 1, :] - s1
        c_ref[h] = jnp.exp(s1 - a0[h:h + 1, :]) * inv_z[h:h + 1, :]
        s2_ref[h] = s2
        e2_ref[h] = jnp.exp(s2 - b0[h:h + 1, :])


def _peer_route(h2, wq_bf16, keys_bf16):
    bsz, seq, _ = h2.shape
    tb = TOK_TILE
    per_head = lambda: pl.BlockSpec((None, PEER_HEADS, N_KEYS, tb), lambda b, i: (b, 0, 0, i))
    head_shape = jax.ShapeDtypeStruct((bsz, PEER_HEADS, N_KEYS, seq), F32)
    qcols = PEER_HEADS * PEER_KEY_DIM
    return pl.pallas_call(
        _peer_route_kernel,
        grid=(bsz, seq // tb),
        in_specs=[pl.BlockSpec((None, tb, D_MODEL), lambda b, i: (b, i, 0)),
                  pl.BlockSpec((D_MODEL, qcols), lambda b, i: (0, 0)),
                  pl.BlockSpec((PEER_HEADS, 2, N_KEYS, PEER_HALF), lambda b, i: (0, 0, 0, 0))],
        out_specs=[pl.BlockSpec((None, D_MODEL, tb), lambda b, i: (b, 0, i)),
                   per_head(), per_head(), per_head(), per_head()],
        out_shape=[jax.ShapeDtypeStruct((bsz, D_MODEL, seq), BF16), head_shape, head_shape, head_shape, head_shape],
        scratch_shapes=[pltpu.VMEM((2 * PEER_HEADS, N_KEYS, tb), F32),
                        pltpu.VMEM((N_RANKS, PEER_HEADS, tb), F32),
                        pltpu.VMEM((N_RANKS, PEER_HEADS, tb), F32)],
        compiler_params=_params(("parallel", "parallel"),
                                D_MODEL * qcols * 2 + 5 * PEER_HEADS * N_KEYS * tb * 4 + tb * qcols * 4),
        name="peer_routing",
    )(h2, wq_bf16, keys_bf16)


def _gelu(x):
    return 0.5 * x * (1.0 + lax.erf(x * (2.0 ** -0.5)))


def _routing_block(s2_ref, e2_ref, thr_ref, c_ref, act_ref, wg_ref, t, k0):
    cols = slice(t * V7X_LANES, (t + 1) * V7X_LANES)
    keys = pl.ds(k0, KEY_CHUNK)
    w = [jnp.zeros((KEY_CHUNK, V7X_LANES), F32) for _ in range(ROWS_PER_TILE)]
    for h in range(PEER_HEADS):
        s2 = s2_ref[h, keys, cols]
        e2 = e2_ref[h, keys, cols]
        for r in range(ROWS_PER_TILE):
            thr = thr_ref[h, r:r + 1, cols]
            coef = c_ref[h, r:r + 1, cols]
            w[r] = w[r] + jnp.where(s2 >= thr, e2 * coef, 0.0)
    for r in range(ROWS_PER_TILE):
        rows = pl.ds(pl.multiple_of(r * N_KEYS + k0, KEY_CHUNK), KEY_CHUNK)
        wg_ref[rows, cols] = (w[r] * _gelu(act_ref[rows, cols])).astype(BF16)


def _peer_dense_kernel(h2t_ref, u_ref, vt_ref, s2_ref, e2_ref, thr0_ref, c0_ref, thr1_ref, c1_ref, x1_ref, mod_ref,
                       y_ref, act0_ref, act1_ref, wg0_ref, wg1_ref, acc_ref):
    g = pl.program_id(2)
    tb = acc_ref.shape[1]
    act_refs = (act0_ref, act1_ref)
    wg_refs = (wg0_ref, wg1_ref)

    @pl.when(g == 0)
    def _():
        for ref in act_refs + wg_refs + (acc_ref,):
            ref[...] = jnp.zeros_like(ref)

    for k in range(2):
        thr_ref, c_ref = (thr0_ref, c0_ref) if k == 0 else (thr1_ref, c1_ref)

        def slab(c, carry, k=k, thr_ref=thr_ref, c_ref=c_ref):
            m0 = pl.multiple_of(c * MXU_SLAB, MXU_SLAB)
            rows = pl.ds(m0, MXU_SLAB)
            act_refs[k][rows, :] = _dot(u_ref[pl.ds(k * EXPERT_TILE + m0, MXU_SLAB), :], h2t_ref[...])
            acc_ref[rows, :] += _dot(vt_ref[rows, k * EXPERT_TILE:(k + 1) * EXPERT_TILE], wg_refs[k][...])
            k0 = pl.multiple_of(c * KEY_CHUNK, KEY_CHUNK)
            for t in range(tb // V7X_LANES):
                _routing_block(s2_ref, e2_ref, thr_ref, c_ref, act_refs[1 - k], wg_refs[1 - k], t, k0)
            return carry

        lax.fori_loop(0, CHUNKS_PER_TILE, slab, 0)

    @pl.when(g == pl.num_programs(2) - 1)
    def _():
        y_ref[...] = x1_ref[...] + mod_ref[:, 5 * D_MODEL:6 * D_MODEL] * acc_ref[...].T


def _peer_dense(h2t, u, vt, s2, e2, thr, coef, x1, mod):
    bsz, seq, _ = x1.shape
    tb = TOK_TILE
    n_tiles = u.shape[0] // EXPERT_TILE
    n_pairs = n_tiles // 2
    once = pl.Buffered(1)
    per_head = lambda: pl.BlockSpec((None, PEER_HEADS, N_KEYS, tb), lambda b, i, g: (b, 0, 0, i), pipeline_mode=once)
    tile_rows = lambda f: pl.BlockSpec((None, PEER_HEADS, ROWS_PER_TILE, tb),
                                       lambda b, i, g: (b, 0, jnp.clip(f(g), 0, n_tiles - 1), i))
    return pl.pallas_call(
        _peer_dense_kernel,
        grid=(bsz, seq // tb, n_pairs + 1),
        in_specs=[pl.BlockSpec((None, D_MODEL, tb), lambda b, i, g: (b, 0, i), pipeline_mode=once),
                  pl.BlockSpec((2 * EXPERT_TILE, D_MODEL), lambda b, i, g: (jnp.minimum(g, n_pairs - 1), 0)),
                  pl.BlockSpec((D_MODEL, 2 * EXPERT_TILE), lambda b, i, g: (0, jnp.maximum(g - 1, 0))),
                  per_head(), per_head(),
                  tile_rows(lambda g: 2 * g - 1), tile_rows(lambda g: 2 * g - 1),
                  tile_rows(lambda g: 2 * g), tile_rows(lambda g: 2 * g),
                  pl.BlockSpec((None, tb, D_MODEL), lambda b, i, g: (b, i, 0), pipeline_mode=once),
                  pl.BlockSpec((None, 1, 6 * D_MODEL), lambda b, i, g: (b, 0, 0))],
        out_specs=pl.BlockSpec((None, tb, D_MODEL), lambda b, i, g: (b, i, 0)),
        out_shape=jax.ShapeDtypeStruct((bsz, seq, D_MODEL), F32),
        scratch_shapes=[pltpu.VMEM((EXPERT_TILE, tb), F32), pltpu.VMEM((EXPERT_TILE, tb), F32),
                        pltpu.VMEM((EXPERT_TILE, tb), BF16), pltpu.VMEM((EXPERT_TILE, tb), BF16),
                        pltpu.VMEM((D_MODEL, tb), F32)],
        compiler_params=_params(("parallel", "parallel", "arbitrary"),
                                2 * PEER_HEADS * N_KEYS * tb * 4 + 2 * 2 * EXPERT_TILE * D_MODEL * 4
                                + 2 * tb * D_MODEL * 4 + 2 * EXPERT_TILE * tb * 4),
        name="peer_dense_experts",
    )(h2t, u, vt, s2, e2, thr, coef, thr, coef, x1, mod)


def _layer(x, c, w_ada, b_ada, norm1_g, norm2_g, w_in, q_norm_g, k_norm_g, bias, sink,
           up_f, b_f, up_b, b_b, beta_attn, beta_gla, w_out, wq, keys, u, vt):
    mod = _modulation(c, w_ada, b_ada)
    qkv, gla_in, gates = _in_projection(x, mod, norm1_g, w_in)
    att = _windowed_attention(qkv, bias, sink, q_norm_g, k_norm_g, beta_attn)
    o_f, o_b = _gla(gla_in, gates, up_f, b_f, up_b, b_b)
    x1, h2 = _out_projection(att, o_f, o_b, gla_in, x, mod, beta_gla, w_out, norm2_g)
    h2t, s2, e2, thr, coef = _peer_route(h2, wq, keys)
    return _peer_dense(h2t, u, vt, s2, e2, thr, coef, x1, mod)


def kernel(x_prompt, x_sample, c_prompt, c_sample, w_ada, b_ada, norm1_g, norm2_g, w_in, q_norm_g, k_norm_g,
           rel_bias, sink, gk_up_fwd, gk_bias_fwd, gk_up_bwd, gk_bias_bwd, beta_attn, beta_gla, w_out,
           peer_query, peer_subkeys, peer_u, peer_v):
    depth = w_ada.shape[0]
    bias = _band_bias(rel_bias)
    y_prompt, y_sample = x_prompt, x_sample
    for l in range(depth):
        params = (w_ada[l], b_ada[l], norm1_g[l], norm2_g[l], w_in[l].astype(BF16), q_norm_g[l], k_norm_g[l],
                  bias, sink[l], gk_up_fwd[l], gk_bias_fwd[l], gk_up_bwd[l], gk_bias_bwd[l],
                  beta_attn[l], beta_gla[l], w_out[l].astype(BF16), peer_query[l].astype(BF16),
                  peer_subkeys[l].astype(BF16), peer_u[l], peer_v[l].T)
        y_prompt = _layer(y_prompt, c_prompt, *params)
        y_sample = _layer(y_sample, c_sample, *params)
    return (y_prompt, y_sample)
```

```python
import functools
import math

import numpy as np
import jax
import jax.numpy as jnp
from jax import lax
from jax.experimental import pallas as pl
from jax.experimental.pallas import tpu as pltpu

F32 = jnp.float32
BF16 = jnp.bfloat16

D_MODEL = 1024
ATTN_HEADS = 8
ATTN_KV_HEADS = 2
ATTN_HEAD_DIM = 64
ATTN_GROUP = ATTN_HEADS // ATTN_KV_HEADS
WINDOW = 128
BLOCK = 128
N_BUCKETS = 32
MAX_DISTANCE = 128
GLA_HEADS = 4
GLA_KEY_DIM = 64
GLA_VAL_DIM = 128
GLA_GATE_RANK = 16
GLA_GATE_NORM = 16.0
GLA_CHUNK = 64
PEER_HEADS = 8
N_KEYS = 128
PEER_KEY_DIM = 256
PEER_HALF = PEER_KEY_DIM // 2
PEER_TOPK = 16
EPS = 1e-6

ATTN_Q = ATTN_HEADS * ATTN_HEAD_DIM
ATTN_KV = ATTN_KV_HEADS * ATTN_HEAD_DIM
GLA_QK = GLA_HEADS * GLA_KEY_DIM
GLA_V = GLA_HEADS * GLA_VAL_DIM
ATTN_COLS = ATTN_Q + 2 * ATTN_KV
GLA_COLS = 2 * GLA_QK + 2 * GLA_V
GATE_COLS = 2 * GLA_GATE_RANK
D_IN = ATTN_COLS + GLA_COLS + GATE_COLS

V7X_LANES = 128
V7X_SUBLANES = 8
V7X_VMEM_LIMIT_CAP = 60 * 1024 * 1024

NEG_BIG = -1e30

TOK_TILE = 512
GLA_TILE = 512
EXPERT_TILE = 1024
ROWS_PER_TILE = EXPERT_TILE // N_KEYS
KEY_CHUNK = 32
CHUNKS_PER_TILE = N_KEYS // KEY_CHUNK
MXU_SLAB = EXPERT_TILE // CHUNKS_PER_TILE
N_RANKS = PEER_TOPK + 1
CAND_PAIRS = tuple((i, j) for i in range(N_RANKS) for j in range(N_RANKS) if (i + 1) * (j + 1) <= N_RANKS)


def _vmem_limit(block_bytes):
    return int(min(2 * block_bytes + (16 << 20), V7X_VMEM_LIMIT_CAP))


def _params(semantics, block_bytes):
    return pltpu.CompilerParams(dimension_semantics=semantics, vmem_limit_bytes=_vmem_limit(block_bytes))


def _nt_dot(a, b):
    return lax.dot_general(a, b, (((1,), (1,)), ((), ())), preferred_element_type=F32)


def _dot(a, b):
    return jnp.dot(a, b, preferred_element_type=F32)


def _rms(x, g):
    return x * lax.rsqrt(jnp.mean(x * x, axis=-1, keepdims=True) + EPS) * g


def _mod_kernel(c_ref, w_ref, b_ref, o_ref):
    c = c_ref[...]
    s = c * jax.nn.sigmoid(c)
    o_ref[...] = _dot(s.astype(BF16), w_ref[...].astype(BF16)) + b_ref[...]


def _modulation(c, w_ada, b_ada):
    bsz = c.shape[0]
    rows = -(-bsz // V7X_SUBLANES) * V7X_SUBLANES
    cp = jnp.zeros((rows, D_MODEL), F32).at[:bsz].set(c)
    n_out = w_ada.shape[1]
    tile = D_MODEL
    out = pl.pallas_call(
        _mod_kernel,
        grid=(n_out // tile,),
        in_specs=[pl.BlockSpec((rows, D_MODEL), lambda j: (0, 0)),
                  pl.BlockSpec((D_MODEL, tile), lambda j: (0, j)),
                  pl.BlockSpec((1, tile), lambda j: (0, j))],
        out_specs=pl.BlockSpec((rows, tile), lambda j: (0, j)),
        out_shape=jax.ShapeDtypeStruct((rows, n_out), F32),
        compiler_params=_params(("parallel",), D_MODEL * tile * 4),
        name="adaln_modulation",
    )(cp, w_ada, b_ada.reshape(1, n_out))
    return out[:bsz].reshape(bsz, 1, n_out)


def _inproj_kernel(x_ref, mod_ref, g_ref, w_ref, oa_ref, og_ref, or_ref):
    h = _rms(x_ref[...], g_ref[...])
    h = h * (1.0 + mod_ref[:, D_MODEL:2 * D_MODEL]) + mod_ref[:, 0:D_MODEL]
    p = _dot(h.astype(BF16), w_ref[...])
    oa_ref[...] = p[:, :ATTN_COLS]
    og_ref[...] = p[:, ATTN_COLS:ATTN_COLS + GLA_COLS]
    or_ref[...] = p[:, ATTN_COLS + GLA_COLS:]


def _in_projection(x, mod, norm_g, w_in_bf16):
    bsz, seq, _ = x.shape
    tb = TOK_TILE
    tok = lambda cols: pl.BlockSpec((None, tb, cols), lambda b, i: (b, i, 0))
    return pl.pallas_call(
        _inproj_kernel,
        grid=(bsz, seq // tb),
        in_specs=[tok(D_MODEL),
                  pl.BlockSpec((None, 1, 6 * D_MODEL), lambda b, i: (b, 0, 0)),
                  pl.BlockSpec((1, D_MODEL), lambda b, i: (0, 0)),
                  pl.BlockSpec((D_MODEL, D_IN), lambda b, i: (0, 0))],
        out_specs=[tok(ATTN_COLS), tok(GLA_COLS), tok(GATE_COLS)],
        out_shape=[jax.ShapeDtypeStruct((bsz, seq, ATTN_COLS), F32),
                   jax.ShapeDtypeStruct((bsz, seq, GLA_COLS), F32),
                   jax.ShapeDtypeStruct((bsz, seq, GATE_COLS), F32)],
        compiler_params=_params(("parallel", "parallel"),
                                tb * D_MODEL * 4 + D_MODEL * D_IN * 2 + 2 * tb * D_IN * 4),
        name="norm1_in_projection",
    )(x, mod, norm_g.reshape(1, D_MODEL), w_in_bf16)


def _t5_buckets(rel):
    nb = N_BUCKETS // 2
    ret = (rel > 0).astype(np.int32) * nb
    n = np.abs(rel)
    max_exact = nb // 2
    large = max_exact + (np.log(np.maximum(n, 1) / max_exact) / math.log(MAX_DISTANCE / max_exact)
                         * (nb - max_exact)).astype(np.int32)
    large = np.minimum(large, nb - 1)
    return (ret + np.where(n < max_exact, n, large)).astype(np.int32)


def _band_bias(rel_bias):
    span = 4 * BLOCK
    offsets = np.arange(span) - (2 * BLOCK - 1)
    per_offset = rel_bias[_t5_buckets(offsets)].astype(F32).T
    shifted = jnp.roll(per_offset, -(BLOCK - 1), axis=1)
    skew = jnp.tile(shifted, (1, BLOCK))[:, :BLOCK * (span - 1)].reshape(-1, BLOCK, span - 1)
    rel = np.arange(3 * BLOCK)[None, :] - BLOCK - np.arange(BLOCK)[:, None]
    band = jnp.asarray(np.abs(rel) <= WINDOW)
    return jnp.where(band[None], skew[:, :, :3 * BLOCK], NEG_BIG)


def _attn_kernel(sink_ref, q_ref, kvp_ref, kvc_ref, kvn_ref, bias_ref, qg_ref, kg_ref, beta_ref, o_ref):
    n = pl.program_id(1)
    last = pl.num_programs(1) - 1
    q = q_ref[...]
    kv = jnp.concatenate([kvp_ref[...], kvc_ref[...], kvn_ref[...]], axis=0)
    j = lax.broadcasted_iota(jnp.int32, (BLOCK, 3 * BLOCK), 1)
    valid = jnp.logical_and(jnp.logical_or(j >= BLOCK, n > 0), jnp.logical_or(j < 2 * BLOCK, n < last))
    scale = ATTN_HEAD_DIM ** -0.5
    outs = []
    for hk in range(ATTN_KV_HEADS):
        kh = _rms(kv[:, hk * ATTN_HEAD_DIM:(hk + 1) * ATTN_HEAD_DIM], kg_ref[...]).astype(BF16)
        vh = kv[:, ATTN_KV + hk * ATTN_HEAD_DIM:ATTN_KV + (hk + 1) * ATTN_HEAD_DIM].astype(BF16)
        for g in range(ATTN_GROUP):
            h = hk * ATTN_GROUP + g
            qh = _rms(q[:, h * ATTN_HEAD_DIM:(h + 1) * ATTN_HEAD_DIM], qg_ref[...]).astype(BF16)
            s = _nt_dot(qh, kh) * scale
            s = jnp.where(valid, s + bias_ref[h], NEG_BIG)
            sink = sink_ref[h]
            m = jnp.maximum(jnp.max(s, axis=-1, keepdims=True), sink)
            p = jnp.exp(s - m)
            den = jnp.sum(p, axis=-1, keepdims=True) + jnp.exp(sink - m)
            o = _dot(p.astype(BF16), vh) / den
            outs.append(_rms(o, beta_ref[:, h * ATTN_HEAD_DIM:(h + 1) * ATTN_HEAD_DIM]))
    o_ref[...] = jnp.concatenate(outs, axis=-1)


def _windowed_attention(qkv, bias, sink, q_norm_g, k_norm_g, beta_attn):
    bsz, seq, _ = qkv.shape
    nb = seq // BLOCK
    kv_cols = 2 * ATTN_KV
    kv_blk = ATTN_Q // kv_cols
    kv_spec = lambda f: pl.BlockSpec((None, BLOCK, kv_cols), lambda b, n: (b, f(n), kv_blk))
    const = lambda shape: pl.BlockSpec(shape, lambda b, n: tuple(0 for _ in shape))
    return pl.pallas_call(
        _attn_kernel,
        grid=(bsz, nb),
        in_specs=[pl.BlockSpec(memory_space=pltpu.SMEM),
                  pl.BlockSpec((None, BLOCK, ATTN_Q), lambda b, n: (b, n, 0)),
                  kv_spec(lambda n: jnp.maximum(n - 1, 0)),
                  kv_spec(lambda n: n),
                  kv_spec(lambda n: jnp.minimum(n + 1, nb - 1)),
                  const((ATTN_HEADS, BLOCK, 3 * BLOCK)),
                  const((1, ATTN_HEAD_DIM)), const((1, ATTN_HEAD_DIM)), const((1, ATTN_Q))],
        out_specs=pl.BlockSpec((None, BLOCK, ATTN_Q), lambda b, n: (b, n, 0)),
        out_shape=jax.ShapeDtypeStruct((bsz, seq, ATTN_Q), F32),
        compiler_params=_params(("parallel", "parallel"), ATTN_HEADS * BLOCK * 3 * BLOCK * 4 + 8 * BLOCK * ATTN_Q * 4),
        name="windowed_attention",
    )(sink, qkv, qkv, qkv, qkv, bias, q_norm_g.reshape(1, -1), k_norm_g.reshape(1, -1), beta_attn.reshape(1, -1))


def _split3(x):
    hi = x.astype(BF16)
    r = x - hi.astype(F32)
    mid = r.astype(BF16)
    lo = (r - mid.astype(F32)).astype(BF16)
    return hi, mid, lo


def _gla_direction(qk_ref, v_ref, gr, up_ref, gb_ref, tri_ref, o_ref, st_ref, b_ref, reverse):
    c = GLA_CHUNK
    nch = GLA_TILE // c
    z = _dot(gr.astype(BF16), up_ref[...]) + gb_ref[...]
    log_a = (jnp.minimum(z, 0.0) - jnp.log1p(jnp.exp(-jnp.abs(z)))) / GLA_GATE_NORM
    hi, mid, lo = _split3(log_a)
    tri = tri_ref[...]
    b_ref[...] = _dot(tri, hi) + _dot(tri, mid) + _dot(tri, lo)

    row = lax.broadcasted_iota(jnp.int32, (c, GLA_QK), 0)
    lane = lax.broadcasted_iota(jnp.int32, (c, GLA_QK), 1)
    key_in_head = lane % GLA_KEY_DIM
    causal = (key_in_head >= row) if reverse else (key_in_head <= row)
    qk_head = lane // GLA_KEY_DIM
    v_head = lax.broadcasted_iota(jnp.int32, (c, GLA_V), 1) // GLA_VAL_DIM
    st_rows = lax.broadcasted_iota(jnp.int32, (GLA_V, GLA_QK), 0) // GLA_VAL_DIM
    st_cols = lax.broadcasted_iota(jnp.int32, (GLA_V, GLA_QK), 1) // GLA_KEY_DIM
    same_head = st_rows == st_cols

    for ci in range(nch):
        blk = nch - 1 - ci if reverse else ci
        rows = slice(blk * c, (blk + 1) * c)
        b = b_ref[rows, :]
        b_last = b[0:1, :] if reverse else b[c - 1:c, :]
        q = qk_ref[rows, 0:GLA_QK]
        k = qk_ref[rows, GLA_QK:2 * GLA_QK]
        v = v_ref[rows, :]
        q_dec = (q * ((GLA_KEY_DIM ** -0.5) * jnp.exp(b))).astype(BF16)
        k_dec = k * jnp.exp(-b)
        k_end = (k * jnp.exp(b_last - b)).astype(BF16)
        decay = jnp.exp(b_last)
        k_heads = jnp.concatenate([jnp.where(qk_head == h, k_dec, 0.0) for h in range(GLA_HEADS)], axis=0)
        v_heads = jnp.concatenate([jnp.where(v_head == h, v, 0.0) for h in range(GLA_HEADS)], axis=0)
        a = jnp.where(causal, _nt_dot(q_dec, k_heads.astype(BF16)), 0.0)
        st = st_ref[...]
        o_ref[rows, :] = _dot(a.astype(BF16), v_heads.astype(BF16)) + _nt_dot(q_dec, st.astype(BF16))
        kv_t = lax.dot_general(v.astype(BF16), k_end, (((0,), (0,)), ((), ())), preferred_element_type=F32)
        st_ref[...] = st * decay + jnp.where(same_head, kv_t, 0.0)


def _gla_kernel(qkf_ref, vf_ref, grf_ref, qkb_ref, vb_ref, grb_ref, upf_ref, bf_ref, upb_ref, bb_ref,
                trif_ref, trib_ref, of_ref, ob_ref, stf_ref, stb_ref, cumf_ref, cumb_ref):
    @pl.when(pl.program_id(1) == 0)
    def _():
        stf_ref[...] = jnp.zeros_like(stf_ref)
        stb_ref[...] = jnp.zeros_like(stb_ref)

    _gla_direction(qkf_ref, vf_ref, grf_ref[:, 0:GLA_GATE_RANK], upf_ref, bf_ref, trif_ref,
                   of_ref, stf_ref, cumf_ref, False)
    _gla_direction(qkb_ref, vb_ref, grb_ref[:, GLA_GATE_RANK:2 * GLA_GATE_RANK], upb_ref, bb_ref, trib_ref,
                   ob_ref, stb_ref, cumb_ref, True)


def _gla(gla_in, gates, up_f, b_f, up_b, b_b):
    bsz, seq, _ = gla_in.shape
    tb = GLA_TILE
    nblk = seq // tb
    fwd = lambda cols, cb: pl.BlockSpec((None, tb, cols), lambda b, i: (b, i, cb))
    bwd = lambda cols, cb: pl.BlockSpec((None, tb, cols), lambda b, i: (b, nblk - 1 - i, cb))
    const = lambda shape: pl.BlockSpec(shape, lambda b, i: tuple(0 for _ in shape))
    chunk_of = np.arange(tb) // GLA_CHUNK
    same_chunk = chunk_of[:, None] == chunk_of[None, :]
    pos = np.arange(tb)
    tri_f = jnp.asarray(same_chunk & (pos[None, :] <= pos[:, None]), BF16)
    tri_b = jnp.asarray(same_chunk & (pos[None, :] >= pos[:, None]), BF16)
    state = pltpu.VMEM((GLA_V, GLA_QK), F32)
    cum = pltpu.VMEM((tb, GLA_QK), F32)
    return pl.pallas_call(
        _gla_kernel,
        grid=(bsz, nblk),
        in_specs=[fwd(2 * GLA_QK, 0), fwd(GLA_V, 1), fwd(GATE_COLS, 0),
                  bwd(2 * GLA_QK, 0), bwd(GLA_V, 1), bwd(GATE_COLS, 0),
                  const((GLA_GATE_RANK, GLA_QK)), const((1, GLA_QK)),
                  const((GLA_GATE_RANK, GLA_QK)), const((1, GLA_QK)),
                  const((tb, tb)), const((tb, tb))],
        out_specs=[pl.BlockSpec((None, tb, GLA_V), lambda b, i: (b, i, 0)),
                   pl.BlockSpec((None, tb, GLA_V), lambda b, i: (b, nblk - 1 - i, 0))],
        out_shape=[jax.ShapeDtypeStruct((bsz, seq, GLA_V), F32)] * 2,
        scratch_shapes=[state, state, cum, cum],
        compiler_params=_params(("parallel", "arbitrary"), 2 * tb * (2 * GLA_QK + 2 * GLA_V + V7X_LANES) * 4),
        name="gla_bidirectional",
    )(gla_in, gla_in, gates, gla_in, gla_in, gates,
      up_f.astype(BF16), b_f.reshape(1, -1), up_b.astype(BF16), b_b.reshape(1, -1), tri_f, tri_b)


def _outproj_kernel(att_ref, of_ref, ob_ref, og_ref, x_ref, mod_ref, beta_ref, w_ref, g2_ref, x1_ref, h2_ref):
    o = of_ref[...] + ob_ref[...]
    gate = og_ref[...]
    gate = gate * jax.nn.sigmoid(gate)
    parts = [att_ref[...].astype(BF16)]
    for h in range(GLA_HEADS):
        sl = slice(h * GLA_VAL_DIM, (h + 1) * GLA_VAL_DIM)
        parts.append((_rms(o[:, sl], beta_ref[:, sl]) * gate[:, sl]).astype(BF16))
    mix = _dot(jnp.concatenate(parts, axis=-1), w_ref[...])
    d = D_MODEL
    x1 = x_ref[...] + mod_ref[:, 2 * d:3 * d] * mix
    x1_ref[...] = x1
    h2 = _rms(x1, g2_ref[...]) * (1.0 + mod_ref[:, 4 * d:5 * d]) + mod_ref[:, 3 * d:4 * d]
    h2_ref[...] = h2.astype(BF16)


def _out_projection(att, o_f, o_b, gla_in, x, mod, beta_gla, w_out_bf16, norm2_g):
    bsz, seq, _ = x.shape
    tb = TOK_TILE
    tok = lambda cols, cb=0: pl.BlockSpec((None, tb, cols), lambda b, i: (b, i, cb))
    const = lambda shape: pl.BlockSpec(shape, lambda b, i: tuple(0 for _ in shape))
    return pl.pallas_call(
        _outproj_kernel,
        grid=(bsz, seq // tb),
        in_specs=[tok(ATTN_Q), tok(GLA_V), tok(GLA_V), tok(GLA_V, (2 * GLA_QK + GLA_V) // GLA_V), tok(D_MODEL),
                  pl.BlockSpec((None, 1, 6 * D_MODEL), lambda b, i: (b, 0, 0)),
                  const((1, GLA_V)), const((D_MODEL, D_MODEL)), const((1, D_MODEL))],
        out_specs=[tok(D_MODEL), tok(D_MODEL)],
        out_shape=[jax.ShapeDtypeStruct((bsz, seq, D_MODEL), F32),
                   jax.ShapeDtypeStruct((bsz, seq, D_MODEL), BF16)],
        compiler_params=_params(("parallel", "parallel"), tb * (4 * GLA_V + 3 * D_MODEL) * 4 + D_MODEL * D_MODEL * 2),
        name="out_projection_norm2",
    )(att, o_f, o_b, gla_in, x, mod, beta_gla.reshape(1, -1), w_out_bf16, norm2_g.reshape(1, -1))


def _top_values(x, n, out_ref, head):
    for r in range(n):
        m = jnp.max(x, axis=0, keepdims=True)
        out_ref[r, pl.ds(head, 1), :] = m
        x = jnp.where(x == m, -jnp.inf, x)


def _peer_route_kernel(h2_ref, wq_ref, keys_ref, h2t_ref, s2_ref, e2_ref, thr_ref, c_ref, sc_ref, a_ref, b_ref):
    h2 = h2_ref[...]
    h2t_ref[...] = h2.astype(F32).T.astype(BF16)
    q = _dot(h2, wq_ref[...]).astype(BF16)
    for h in range(PEER_HEADS):
        for p in range(2):
            blk = h * 2 + p
            s = _nt_dot(keys_ref[h, p], q[:, blk * PEER_HALF:(blk + 1) * PEER_HALF])
            sc_ref[blk] = s
            _top_values(s, N_RANKS, a_ref if p == 0 else b_ref, h)
    cand = [a_ref[i] + b_ref[j] for (i, j) in CAND_PAIRS]
    vals = []
    for r in range(N_RANKS):
        m = functools.reduce(jnp.maximum, cand)
        vals.append(m)
        cand = [jnp.where(x == m, -jnp.inf, x) for x in cand]
    top = vals[0]
    thr = 0.5 * (vals[PEER_TOPK - 1] + vals[PEER_TOPK])
    z = functools.reduce(jnp.add, [jnp.exp(v - top) for v in vals[:PEER_TOPK]])
    inv_z = 1.0 / z
    a0 = a_ref[0]
    b0 = b_ref[0]
    for h in range(PEER_HEADS):
        s1 = sc_ref[2 * h]
        s2 = sc_ref[2 * h + 1]
        thr_ref[h] = thr[h:h + 1, :] - s1
        c_ref[h] = jnp.exp(s1 - a0[h:h + 1, :]) * inv_z[h:h + 1, :]
        s2_ref[h] = s2
        e2_ref[h] = jnp.exp(s2 - b0[h:h + 1, :])


def _peer_route(h2, wq_bf16, keys_bf16):
    bsz, seq, _ = h2.shape
    tb = TOK_TILE
    per_head = lambda: pl.BlockSpec((None, PEER_HEADS, N_KEYS, tb), lambda b, i: (b, 0, 0, i))
    head_shape = jax.ShapeDtypeStruct((bsz, PEER_HEADS, N_KEYS, seq), F32)
    qcols = PEER_HEADS * PEER_KEY_DIM
    return pl.pallas_call(
        _peer_route_kernel,
        grid=(bsz, seq // tb),
        in_specs=[pl.BlockSpec((None, tb, D_MODEL), lambda b, i: (b, i, 0)),
                  pl.BlockSpec((D_MODEL, qcols), lambda b, i: (0, 0)),
                  pl.BlockSpec((PEER_HEADS, 2, N_KEYS, PEER_HALF), lambda b, i: (0, 0, 0, 0))],
        out_specs=[pl.BlockSpec((None, D_MODEL, tb), lambda b, i: (b, 0, i)),
                   per_head(), per_head(), per_head(), per_head()],
        out_shape=[jax.ShapeDtypeStruct((bsz, D_MODEL, seq), BF16), head_shape, head_shape, head_shape, head_shape],
        scratch_shapes=[pltpu.VMEM((2 * PEER_HEADS, N_KEYS, tb), F32),
                        pltpu.VMEM((N_RANKS, PEER_HEADS, tb), F32),
                        pltpu.VMEM((N_RANKS, PEER_HEADS, tb), F32)],
        compiler_params=_params(("parallel", "parallel"),
                                D_MODEL * qcols * 2 + 5 * PEER_HEADS * N_KEYS * tb * 4 + tb * qcols * 4),
        name="peer_routing",
    )(h2, wq_bf16, keys_bf16)


def _gelu(x):
    return 0.5 * x * (1.0 + lax.erf(x * (2.0 ** -0.5)))


def _routing_block(s2_ref, e2_ref, thr_ref, c_ref, act_ref, wg_ref, t, k0):
    cols = slice(t * V7X_LANES, (t + 1) * V7X_LANES)
    keys = pl.ds(k0, KEY_CHUNK)
    w = [jnp.zeros((KEY_CHUNK, V7X_LANES), F32) for _ in range(ROWS_PER_TILE)]
    for h in range(PEER_HEADS):
        s2 = s2_ref[h, keys, cols]
        e2 = e2_ref[h, keys, cols]
        for r in range(ROWS_PER_TILE):
            thr = thr_ref[h, r:r + 1, cols]
            coef = c_ref[h, r:r + 1, cols]
            w[r] = w[r] + jnp.where(s2 >= thr, e2 * coef, 0.0)
    for r in range(ROWS_PER_TILE):
        rows = pl.ds(pl.multiple_of(r * N_KEYS + k0, KEY_CHUNK), KEY_CHUNK)
        wg_ref[rows, cols] = (w[r] * _gelu(act_ref[rows, cols])).astype(BF16)


def _peer_dense_kernel(h2t_ref, u_ref, vt_ref, s2_ref, e2_ref, thr0_ref, c0_ref, thr1_ref, c1_ref, x1_ref, mod_ref,
                       y_ref, act0_ref, act1_ref, wg0_ref, wg1_ref, acc_ref):
    g = pl.program_id(2)
    tb = acc_ref.shape[1]
    act_refs = (act0_ref, act1_ref)
    wg_refs = (wg0_ref, wg1_ref)

    @pl.when(g == 0)
    def _():
        for ref in act_refs + wg_refs + (acc_ref,):
            ref[...] = jnp.zeros_like(ref)

    for k in range(2):
        thr_ref, c_ref = (thr0_ref, c0_ref) if k == 0 else (thr1_ref, c1_ref)

        def slab(c, carry, k=k, thr_ref=thr_ref, c_ref=c_ref):
            m0 = pl.multiple_of(c * MXU_SLAB, MXU_SLAB)
            rows = pl.ds(m0, MXU_SLAB)
            act_refs[k][rows, :] = _dot(u_ref[pl.ds(k * EXPERT_TILE + m0, MXU_SLAB), :], h2t_ref[...])
            acc_ref[rows, :] += _dot(vt_ref[rows, k * EXPERT_TILE:(k + 1) * EXPERT_TILE], wg_refs[k][...])
            k0 = pl.multiple_of(c * KEY_CHUNK, KEY_CHUNK)
            for t in range(tb // V7X_LANES):
                _routing_block(s2_ref, e2_ref, thr_ref, c_ref, act_refs[1 - k], wg_refs[1 - k], t, k0)
            return carry

        lax.fori_loop(0, CHUNKS_PER_TILE, slab, 0)

    @pl.when(g == pl.num_programs(2) - 1)
    def _():
        y_ref[...] = x1_ref[...] + mod_ref[:, 5 * D_MODEL:6 * D_MODEL] * acc_ref[...].T


def _peer_dense(h2t, u, vt, s2, e2, thr, coef, x1, mod):
    bsz, seq, _ = x1.shape
    tb = TOK_TILE
    n_tiles = u.shape[0] // EXPERT_TILE
    n_pairs = n_tiles // 2
    once = pl.Buffered(1)
    per_head = lambda: pl.BlockSpec((None, PEER_HEADS, N_KEYS, tb), lambda b, i, g: (b, 0, 0, i), pipeline_mode=once)
    tile_rows = lambda f: pl.BlockSpec((None, PEER_HEADS, ROWS_PER_TILE, tb),
                                       lambda b, i, g: (b, 0, jnp.clip(f(g), 0, n_tiles - 1), i))
    return pl.pallas_call(
        _peer_dense_kernel,
        grid=(bsz, seq // tb, n_pairs + 1),
        in_specs=[pl.BlockSpec((None, D_MODEL, tb), lambda b, i, g: (b, 0, i), pipeline_mode=once),
                  pl.BlockSpec((2 * EXPERT_TILE, D_MODEL), lambda b, i, g: (jnp.minimum(g, n_pairs - 1), 0)),
                  pl.BlockSpec((D_MODEL, 2 * EXPERT_TILE), lambda b, i, g: (0, jnp.maximum(g - 1, 0))),
                  per_head(), per_head(),
                  tile_rows(lambda g: 2 * g - 1), tile_rows(lambda g: 2 * g - 1),
                  tile_rows(lambda g: 2 * g), tile_rows(lambda g: 2 * g),
                  pl.BlockSpec((None, tb, D_MODEL), lambda b, i, g: (b, i, 0), pipeline_mode=once),
                  pl.BlockSpec((None, 1, 6 * D_MODEL), lambda b, i, g: (b, 0, 0))],
        out_specs=pl.BlockSpec((None, tb, D_MODEL), lambda b, i, g: (b, i, 0)),
        out_shape=jax.ShapeDtypeStruct((bsz, seq, D_MODEL), F32),
        scratch_shapes=[pltpu.VMEM((EXPERT_TILE, tb), F32), pltpu.VMEM((EXPERT_TILE, tb), F32),
                        pltpu.VMEM((EXPERT_TILE, tb), BF16), pltpu.VMEM((EXPERT_TILE, tb), BF16),
                        pltpu.VMEM((D_MODEL, tb), F32)],
        compiler_params=_params(("parallel", "parallel", "arbitrary"),
                                2 * PEER_HEADS * N_KEYS * tb * 4 + 2 * 2 * EXPERT_TILE * D_MODEL * 4
                                + 2 * tb * D_MODEL * 4 + 2 * EXPERT_TILE * tb * 4),
        name="peer_dense_experts",
    )(h2t, u, vt, s2, e2, thr, coef, thr, coef, x1, mod)


def _layer(x, c, w_ada, b_ada, norm1_g, norm2_g, w_in, q_norm_g, k_norm_g, bias, sink,
           up_f, b_f, up_b, b_b, beta_attn, beta_gla, w_out, wq, keys, u, vt):
    mod = _modulation(c, w_ada, b_ada)
    qkv, gla_in, gates = _in_projection(x, mod, norm1_g, w_in)
    att = _windowed_attention(qkv, bias, sink, q_norm_g, k_norm_g, beta_attn)
    o_f, o_b = _gla(gla_in, gates, up_f, b_f, up_b, b_b)
    x1, h2 = _out_projection(att, o_f, o_b, gla_in, x, mod, beta_gla, w_out, norm2_g)
    h2t, s2, e2, thr, coef = _peer_route(h2, wq, keys)
    return _peer_dense(h2t, u, vt, s2, e2, thr, coef, x1, mod)


def kernel(x_prompt, x_sample, c_prompt, c_sample, w_ada, b_ada, norm1_g, norm2_g, w_in, q_norm_g, k_norm_g,
           rel_bias, sink, gk_up_fwd, gk_bias_fwd, gk_up_bwd, gk_bias_bwd, beta_attn, beta_gla, w_out,
           peer_query, peer_subkeys, peer_u, peer_v):
    depth = w_ada.shape[0]
    bias = _band_bias(rel_bias)
    y_prompt, y_sample = x_prompt, x_sample
    for l in range(depth):
        params = (w_ada[l], b_ada[l], norm1_g[l], norm2_g[l], w_in[l].astype(BF16), q_norm_g[l], k_norm_g[l],
                  bias, sink[l], gk_up_fwd[l], gk_bias_fwd[l], gk_up_bwd[l], gk_bias_bwd[l],
                  beta_attn[l], beta_gla[l], w_out[l].astype(BF16), peer_query[l].astype(BF16),
                  peer_subkeys[l].astype(BF16), peer_u[l], peer_v[l].T)
        y_prompt = _layer(y_prompt, c_prompt, *params)
        y_sample = _layer(y_sample, c_sample, *params)
    return (y_prompt, y_sample)
```

```python
import functools
import math

import numpy as np
import jax
import jax.numpy as jnp
from jax import lax
from jax.experimental import pallas as pl
from jax.experimental.pallas import tpu as pltpu

F32 = jnp.float32
BF16 = jnp.bfloat16

D_MODEL = 1024
ATTN_HEADS = 8
ATTN_KV_HEADS = 2
ATTN_HEAD_DIM = 64
ATTN_GROUP = ATTN_HEADS // ATTN_KV_HEADS
WINDOW = 128
BLOCK = 128
N_BUCKETS = 32
MAX_DISTANCE = 128
GLA_HEADS = 4
GLA_KEY_DIM = 64
GLA_VAL_DIM = 128
GLA_GATE_RANK = 16
GLA_GATE_NORM = 16.0
GLA_CHUNK = 64
PEER_HEADS = 8
N_KEYS = 128
PEER_KEY_DIM = 256
PEER_HALF = PEER_KEY_DIM // 2
PEER_TOPK = 16
EPS = 1e-6

ATTN_Q = ATTN_HEADS * ATTN_HEAD_DIM
ATTN_KV = ATTN_KV_HEADS * ATTN_HEAD_DIM
GLA_QK = GLA_HEADS * GLA_KEY_DIM
GLA_V = GLA_HEADS * GLA_VAL_DIM
ATTN_COLS = ATTN_Q + 2 * ATTN_KV
GLA_COLS = 2 * GLA_QK + 2 * GLA_V
GATE_COLS = 2 * GLA_GATE_RANK
D_IN = ATTN_COLS + GLA_COLS + GATE_COLS

V7X_LANES = 128
V7X_SUBLANES = 8
V7X_VMEM_LIMIT_CAP = 60 * 1024 * 1024

NEG_BIG = -1e30

TOK_TILE = 512
GLA_TILE = 512
ATTN_BLOCKS_PER_STEP = 2
EXPERT_TILE = 1024
ROWS_PER_TILE = EXPERT_TILE // N_KEYS
KEY_CHUNK = 32
CHUNKS_PER_TILE = N_KEYS // KEY_CHUNK
MXU_SLAB = EXPERT_TILE // CHUNKS_PER_TILE
N_RANKS = PEER_TOPK + 1
CAND_PAIRS = tuple((i, j) for i in range(N_RANKS) for j in range(N_RANKS) if (i + 1) * (j + 1) <= N_RANKS)


def _vmem_limit(block_bytes):
    return int(min(2 * block_bytes + (16 << 20), V7X_VMEM_LIMIT_CAP))


def _params(semantics, block_bytes):
    return pltpu.CompilerParams(dimension_semantics=semantics, vmem_limit_bytes=_vmem_limit(block_bytes))


def _nt_dot(a, b):
    return lax.dot_general(a, b, (((1,), (1,)), ((), ())), preferred_element_type=F32)


def _dot(a, b):
    return jnp.dot(a, b, preferred_element_type=F32)


def _rms(x, g):
    return x * lax.rsqrt(jnp.mean(x * x, axis=-1, keepdims=True) + EPS) * g


def _mod_kernel(c_ref, w_ref, b_ref, o_ref):
    c = c_ref[...]
    s = c * jax.nn.sigmoid(c)
    o_ref[...] = _dot(s.astype(BF16), w_ref[...].astype(BF16)) + b_ref[...]


def _modulation(c, w_ada, b_ada):
    bsz = c.shape[0]
    rows = -(-bsz // V7X_SUBLANES) * V7X_SUBLANES
    cp = jnp.zeros((rows, D_MODEL), F32).at[:bsz].set(c)
    n_out = w_ada.shape[1]
    tile = D_MODEL
    out = pl.pallas_call(
        _mod_kernel,
        grid=(n_out // tile,),
        in_specs=[pl.BlockSpec((rows, D_MODEL), lambda j: (0, 0)),
                  pl.BlockSpec((D_MODEL, tile), lambda j: (0, j)),
                  pl.BlockSpec((1, tile), lambda j: (0, j))],
        out_specs=pl.BlockSpec((rows, tile), lambda j: (0, j)),
        out_shape=jax.ShapeDtypeStruct((rows, n_out), F32),
        compiler_params=_params(("parallel",), D_MODEL * tile * 4),
        name="adaln_modulation",
    )(cp, w_ada, b_ada.reshape(1, n_out))
    return out[:bsz].reshape(bsz, 1, n_out)


def _inproj_kernel(x_ref, mod_ref, g_ref, w_ref, oa_ref, og_ref, or_ref):
    h = _rms(x_ref[...], g_ref[...])
    h = h * (1.0 + mod_ref[:, D_MODEL:2 * D_MODEL]) + mod_ref[:, 0:D_MODEL]
    p = _dot(h.astype(BF16), w_ref[...])
    oa_ref[...] = p[:, :ATTN_COLS]
    og_ref[...] = p[:, ATTN_COLS:ATTN_COLS + GLA_COLS]
    or_ref[...] = p[:, ATTN_COLS + GLA_COLS:]


def _in_projection(x, mod, norm_g, w_in_bf16):
    bsz, seq, _ = x.shape
    tb = TOK_TILE
    tok = lambda cols: pl.BlockSpec((None, tb, cols), lambda b, i: (b, i, 0))
    return pl.pallas_call(
        _inproj_kernel,
        grid=(bsz, seq // tb),
        in_specs=[tok(D_MODEL),
                  pl.BlockSpec((None, 1, 6 * D_MODEL), lambda b, i: (b, 0, 0)),
                  pl.BlockSpec((1, D_MODEL), lambda b, i: (0, 0)),
                  pl.BlockSpec((D_MODEL, D_IN), lambda b, i: (0, 0))],
        out_specs=[tok(ATTN_COLS), tok(GLA_COLS), tok(GATE_COLS)],
        out_shape=[jax.ShapeDtypeStruct((bsz, seq, ATTN_COLS), F32),
                   jax.ShapeDtypeStruct((bsz, seq, GLA_COLS), F32),
                   jax.ShapeDtypeStruct((bsz, seq, GATE_COLS), F32)],
        compiler_params=_params(("parallel", "parallel"),
                                tb * D_MODEL * 4 + D_MODEL * D_IN * 2 + 2 * tb * D_IN * 4),
        name="norm1_in_projection",
    )(x, mod, norm_g.reshape(1, D_MODEL), w_in_bf16)


def _t5_buckets(rel):
    nb = N_BUCKETS // 2
    ret = (rel > 0).astype(np.int32) * nb
    n = np.abs(rel)
    max_exact = nb // 2
    large = max_exact + (np.log(np.maximum(n, 1) / max_exact) / math.log(MAX_DISTANCE / max_exact)
                         * (nb - max_exact)).astype(np.int32)
    large = np.minimum(large, nb - 1)
    return (ret + np.where(n < max_exact, n, large)).astype(np.int32)


def _band_bias(rel_bias):
    span = 4 * BLOCK
    offsets = np.arange(span) - (2 * BLOCK - 1)
    per_offset = rel_bias[_t5_buckets(offsets)].astype(F32).T
    shifted = jnp.roll(per_offset, -(BLOCK - 1), axis=1)
    skew = jnp.tile(shifted, (1, BLOCK))[:, :BLOCK * (span - 1)].reshape(-1, BLOCK, span - 1)
    rel = np.arange(3 * BLOCK)[None, :] - BLOCK - np.arange(BLOCK)[:, None]
    band = jnp.asarray(np.abs(rel) <= WINDOW)
    return jnp.where(band[None], skew[:, :, :3 * BLOCK], NEG_BIG)


def _attn_block(q, kv, valid, sink_ref, bias_ref, qg_ref, kg_ref, beta_ref):
    scale = ATTN_HEAD_DIM ** -0.5
    outs = []
    for hk in range(ATTN_KV_HEADS):
        kh = _rms(kv[:, hk * ATTN_HEAD_DIM:(hk + 1) * ATTN_HEAD_DIM], kg_ref[...]).astype(BF16)
        vh = kv[:, ATTN_KV + hk * ATTN_HEAD_DIM:ATTN_KV + (hk + 1) * ATTN_HEAD_DIM].astype(BF16)
        for g in range(ATTN_GROUP):
            h = hk * ATTN_GROUP + g
            qh = _rms(q[:, h * ATTN_HEAD_DIM:(h + 1) * ATTN_HEAD_DIM], qg_ref[...]).astype(BF16)
            s = _nt_dot(qh, kh) * scale
            s = jnp.where(valid, s + bias_ref[h], NEG_BIG)
            sink = sink_ref[h]
            m = jnp.maximum(jnp.max(s, axis=-1, keepdims=True), sink)
            p = jnp.exp(s - m)
            den = jnp.sum(p, axis=-1, keepdims=True) + jnp.exp(sink - m)
            o = _dot(p.astype(BF16), vh) / den
            outs.append(_rms(o, beta_ref[:, h * ATTN_HEAD_DIM:(h + 1) * ATTN_HEAD_DIM]))
    return jnp.concatenate(outs, axis=-1)


def _attn_kernel(sink_ref, q_ref, kvp_ref, kvc_ref, kvn_ref, bias_ref, qg_ref, kg_ref, beta_ref, o_ref):
    m = pl.program_id(1)
    last = pl.num_programs(1) - 1
    kv = jnp.concatenate([kvp_ref[...], kvc_ref[...], kvn_ref[...]], axis=0)
    j = lax.broadcasted_iota(jnp.int32, (BLOCK, 3 * BLOCK), 1)
    for sub in range(ATTN_BLOCKS_PER_STEP):
        valid = j >= 0
        if sub == 0:
            valid = jnp.logical_and(valid, jnp.logical_or(j >= BLOCK, m > 0))
        if sub == ATTN_BLOCKS_PER_STEP - 1:
            valid = jnp.logical_and(valid, jnp.logical_or(j < 2 * BLOCK, m < last))
        rows = slice(sub * BLOCK, (sub + 1) * BLOCK)
        o_ref[rows, :] = _attn_block(q_ref[rows, :], kv[sub * BLOCK:(sub + 3) * BLOCK, :], valid,
                                     sink_ref, bias_ref, qg_ref, kg_ref, beta_ref)


def _windowed_attention(qkv, bias, sink, q_norm_g, k_norm_g, beta_attn):
    bsz, seq, _ = qkv.shape
    nb = seq // BLOCK
    step = ATTN_BLOCKS_PER_STEP
    kv_cols = 2 * ATTN_KV
    kv_blk = ATTN_Q // kv_cols
    edge = lambda f: pl.BlockSpec((None, BLOCK, kv_cols), lambda b, m: (b, f(m), kv_blk))
    const = lambda shape: pl.BlockSpec(shape, lambda b, m: tuple(0 for _ in shape))
    return pl.pallas_call(
        _attn_kernel,
        grid=(bsz, nb // step),
        in_specs=[pl.BlockSpec(memory_space=pltpu.SMEM),
                  pl.BlockSpec((None, step * BLOCK, ATTN_Q), lambda b, m: (b, m, 0)),
                  edge(lambda m: jnp.maximum(step * m - 1, 0)),
                  pl.BlockSpec((None, step * BLOCK, kv_cols), lambda b, m: (b, m, kv_blk)),
                  edge(lambda m: jnp.minimum(step * m + step, nb - 1)),
                  const((ATTN_HEADS, BLOCK, 3 * BLOCK)),
                  const((1, ATTN_HEAD_DIM)), const((1, ATTN_HEAD_DIM)), const((1, ATTN_Q))],
        out_specs=pl.BlockSpec((None, step * BLOCK, ATTN_Q), lambda b, m: (b, m, 0)),
        out_shape=jax.ShapeDtypeStruct((bsz, seq, ATTN_Q), F32),
        compiler_params=_params(("parallel", "parallel"),
                                ATTN_HEADS * BLOCK * 3 * BLOCK * 4 + 8 * step * BLOCK * ATTN_Q * 4),
        name="windowed_attention",
    )(sink, qkv, qkv, qkv, qkv, bias, q_norm_g.reshape(1, -1), k_norm_g.reshape(1, -1), beta_attn.reshape(1, -1))


def _split3(x):
    hi = x.astype(BF16)
    r = x - hi.astype(F32)
    mid = r.astype(BF16)
    lo = (r - mid.astype(F32)).astype(BF16)
    return hi, mid, lo


def _gla_direction(qk_ref, v_ref, gr, up_ref, gb_ref, tri_ref, o_ref, st_ref, b_ref, reverse):
    c = GLA_CHUNK
    nch = GLA_TILE // c
    z = _dot(gr.astype(BF16), up_ref[...]) + gb_ref[...]
    log_a = (jnp.minimum(z, 0.0) - jnp.log1p(jnp.exp(-jnp.abs(z)))) / GLA_GATE_NORM
    hi, mid, lo = _split3(log_a)
    tri = tri_ref[...]
    b_ref[...] = _dot(tri, hi) + _dot(tri, mid) + _dot(tri, lo)

    row = lax.broadcasted_iota(jnp.int32, (c, GLA_QK), 0)
    lane = lax.broadcasted_iota(jnp.int32, (c, GLA_QK), 1)
    key_in_head = lane % GLA_KEY_DIM
    causal = (key_in_head >= row) if reverse else (key_in_head <= row)
    qk_head = lane // GLA_KEY_DIM
    v_head = lax.broadcasted_iota(jnp.int32, (c, GLA_V), 1) // GLA_VAL_DIM
    st_rows = lax.broadcasted_iota(jnp.int32, (GLA_V, GLA_QK), 0) // GLA_VAL_DIM
    st_cols = lax.broadcasted_iota(jnp.int32, (GLA_V, GLA_QK), 1) // GLA_KEY_DIM
    same_head = st_rows == st_cols

    for ci in range(nch):
        blk = nch - 1 - ci if reverse else ci
        rows = slice(blk * c, (blk + 1) * c)
        b = b_ref[rows, :]
        b_last = b[0:1, :] if reverse else b[c - 1:c, :]
        q = qk_ref[rows, 0:GLA_QK]
        k = qk_ref[rows, GLA_QK:2 * GLA_QK]
        v = v_ref[rows, :]
        q_dec = (q * ((GLA_KEY_DIM ** -0.5) * jnp.exp(b))).astype(BF16)
        k_dec = k * jnp.exp(-b)
        k_end = (k * jnp.exp(b_last - b)).astype(BF16)
        decay = jnp.exp(b_last)
        k_heads = jnp.concatenate([jnp.where(qk_head == h, k_dec, 0.0) for h in range(GLA_HEADS)], axis=0)
        v_heads = jnp.concatenate([jnp.where(v_head == h, v, 0.0) for h in range(GLA_HEADS)], axis=0)
        a = jnp.where(causal, _nt_dot(q_dec, k_heads.astype(BF16)), 0.0)
        st = st_ref[...]
        o_ref[rows, :] = _dot(a.astype(BF16), v_heads.astype(BF16)) + _nt_dot(q_dec, st.astype(BF16))
        kv_t = lax.dot_general(v.astype(BF16), k_end, (((0,), (0,)), ((), ())), preferred_element_type=F32)
        st_ref[...] = st * decay + jnp.where(same_head, kv_t, 0.0)


def _gla_kernel(qkf_ref, vf_ref, grf_ref, qkb_ref, vb_ref, grb_ref, upf_ref, bf_ref, upb_ref, bb_ref,
                trif_ref, trib_ref, of_ref, ob_ref, stf_ref, stb_ref, cumf_ref, cumb_ref):
    @pl.when(pl.program_id(1) == 0)
    def _():
        stf_ref[...] = jnp.zeros_like(stf_ref)
        stb_ref[...] = jnp.zeros_like(stb_ref)

    _gla_direction(qkf_ref, vf_ref, grf_ref[:, 0:GLA_GATE_RANK], upf_ref, bf_ref, trif_ref,
                   of_ref, stf_ref, cumf_ref, False)
    _gla_direction(qkb_ref, vb_ref, grb_ref[:, GLA_GATE_RANK:2 * GLA_GATE_RANK], upb_ref, bb_ref, trib_ref,
                   ob_ref, stb_ref, cumb_ref, True)


def _gla(gla_in, gates, up_f, b_f, up_b, b_b):
    bsz, seq, _ = gla_in.shape
    tb = GLA_TILE
    nblk = seq // tb
    fwd = lambda cols, cb: pl.BlockSpec((None, tb, cols), lambda b, i: (b, i, cb))
    bwd = lambda cols, cb: pl.BlockSpec((None, tb, cols), lambda b, i: (b, nblk - 1 - i, cb))
    const = lambda shape: pl.BlockSpec(shape, lambda b, i: tuple(0 for _ in shape))
    chunk_of = np.arange(tb) // GLA_CHUNK
    same_chunk = chunk_of[:, None] == chunk_of[None, :]
    pos = np.arange(tb)
    tri_f = jnp.asarray(same_chunk & (pos[None, :] <= pos[:, None]), BF16)
    tri_b = jnp.asarray(same_chunk & (pos[None, :] >= pos[:, None]), BF16)
    state = pltpu.VMEM((GLA_V, GLA_QK), F32)
    cum = pltpu.VMEM((tb, GLA_QK), F32)
    return pl.pallas_call(
        _gla_kernel,
        grid=(bsz, nblk),
        in_specs=[fwd(2 * GLA_QK, 0), fwd(GLA_V, 1), fwd(GATE_COLS, 0),
                  bwd(2 * GLA_QK, 0), bwd(GLA_V, 1), bwd(GATE_COLS, 0),
                  const((GLA_GATE_RANK, GLA_QK)), const((1, GLA_QK)),
                  const((GLA_GATE_RANK, GLA_QK)), const((1, GLA_QK)),
                  const((tb, tb)), const((tb, tb))],
        out_specs=[pl.BlockSpec((None, tb, GLA_V), lambda b, i: (b, i, 0)),
                   pl.BlockSpec((None, tb, GLA_V), lambda b, i: (b, nblk - 1 - i, 0))],
        out_shape=[jax.ShapeDtypeStruct((bsz, seq, GLA_V), F32)] * 2,
        scratch_shapes=[state, state, cum, cum],
        compiler_params=_params(("parallel", "arbitrary"), 2 * tb * (2 * GLA_QK + 2 * GLA_V + V7X_LANES) * 4),
        name="gla_bidirectional",
    )(gla_in, gla_in, gates, gla_in, gla_in, gates,
      up_f.astype(BF16), b_f.reshape(1, -1), up_b.astype(BF16), b_b.reshape(1, -1), tri_f, tri_b)


def _outproj_kernel(att_ref, of_ref, ob_ref, og_ref, x_ref, mod_ref, beta_ref, w_ref, g2_ref, x1_ref, h2_ref):
    o = of_ref[...] + ob_ref[...]
    gate = og_ref[...]
    gate = gate * jax.nn.sigmoid(gate)
    parts = [att_ref[...].astype(BF16)]
    for h in range(GLA_HEADS):
        sl = slice(h * GLA_VAL_DIM, (h + 1) * GLA_VAL_DIM)
        parts.append((_rms(o[:, sl], beta_ref[:, sl]) * gate[:, sl]).astype(BF16))
    mix = _dot(jnp.concatenate(parts, axis=-1), w_ref[...])
    d = D_MODEL
    x1 = x_ref[...] + mod_ref[:, 2 * d:3 * d] * mix
    x1_ref[...] = x1
    h2 = _rms(x1, g2_ref[...]) * (1.0 + mod_ref[:, 4 * d:5 * d]) + mod_ref[:, 3 * d:4 * d]
    h2_ref[...] = h2.astype(BF16)


def _out_projection(att, o_f, o_b, gla_in, x, mod, beta_gla, w_out_bf16, norm2_g):
    bsz, seq, _ = x.shape
    tb = TOK_TILE
    tok = lambda cols, cb=0: pl.BlockSpec((None, tb, cols), lambda b, i: (b, i, cb))
    const = lambda shape: pl.BlockSpec(shape, lambda b, i: tuple(0 for _ in shape))
    return pl.pallas_call(
        _outproj_kernel,
        grid=(bsz, seq // tb),
        in_specs=[tok(ATTN_Q), tok(GLA_V), tok(GLA_V), tok(GLA_V, (2 * GLA_QK + GLA_V) // GLA_V), tok(D_MODEL),
                  pl.BlockSpec((None, 1, 6 * D_MODEL), lambda b, i: (b, 0, 0)),
                  const((1, GLA_V)), const((D_MODEL, D_MODEL)), const((1, D_MODEL))],
        out_specs=[tok(D_MODEL), tok(D_MODEL)],
        out_shape=[jax.ShapeDtypeStruct((bsz, seq, D_MODEL), F32),
                   jax.ShapeDtypeStruct((bsz, seq, D_MODEL), BF16)],
        compiler_params=_params(("parallel", "parallel"), tb * (4 * GLA_V + 3 * D_MODEL) * 4 + D_MODEL * D_MODEL * 2),
        name="out_projection_norm2",
    )(att, o_f, o_b, gla_in, x, mod, beta_gla.reshape(1, -1), w_out_bf16, norm2_g.reshape(1, -1))


def _sort_network(n):
    def merge(lo, hi, r):
        step = 2 * r
        if step < hi - lo:
            yield from merge(lo, hi, step)
            yield from merge(lo + r, hi, step)
            yield from ((i, i + r) for i in range(lo + r, hi - r, step))
        else:
            yield (lo, lo + r)

    def sort(lo, hi):
        if hi > lo:
            mid = lo + (hi - lo) // 2
            yield from sort(lo, mid)
            yield from sort(mid + 1, hi)
            yield from merge(lo, hi, 1)

    return tuple(sort(0, n - 1))


def _top_values(x, out_ref, head):
    n = N_KEYS // V7X_SUBLANES
    assert n == PEER_TOPK
    tiles =[x[i * V7X_SUBLANES:(i + 1) * V7X_SUBLANES, :] for i in range(n)]
    srt = list(tiles)
    for i, j in _sort_network(n):
        srt[i], srt[j] = jnp.maximum(srt[i], srt[j]), jnp.minimum(srt[i], srt[j])
    shift = V7X_SUBLANES // 2
    while shift >= 1:
        other = [pltpu.roll(t, shift, 0) for t in srt]
        srt = [jnp.maximum(srt[i], other[n - 1 - i]) for i in range(n)]
        stride = n // 2
        while stride >= 1:
            for i in range(n):
                if i & stride == 0:
                    srt[i], srt[i + stride] = (jnp.maximum(srt[i], srt[i + stride]),
                                               jnp.minimum(srt[i], srt[i + stride]))
            stride //= 2
        shift //= 2
    for r in range(PEER_TOPK):
        out_ref[r, pl.ds(head, 1), :] = srt[r][0:1, :]
    below = [jnp.where(t < srt[PEER_TOPK - 1], t, -jnp.inf) for t in tiles]
    out_ref[PEER_TOPK, pl.ds(head, 1), :] = jnp.max(functools.reduce(jnp.maximum, below), axis=0, keepdims=True)


def _peer_route_kernel(h2_ref, wq_ref, keys_ref, h2t_ref, s2_ref, e2_ref, thr_ref, c_ref, sc_ref, a_ref, b_ref):
    h2 = h2_ref[...]
    h2t_ref[...] = h2.astype(F32).T.astype(BF16)
    q = _dot(h2, wq_ref[...]).astype(BF16)
    for h in range(PEER_HEADS):
        for p in range(2):
            blk = h * 2 + p
            s = _nt_dot(keys_ref[h, p], q[:, blk * PEER_HALF:(blk + 1) * PEER_HALF])
            sc_ref[blk] = s
            _top_values(s, a_ref if p == 0 else b_ref, h)
    cand = [a_ref[i] + b_ref[j] for (i, j) in CAND_PAIRS]
    vals = []
    for r in range(N_RANKS):
        m = functools.reduce(jnp.maximum, cand)
        vals.append(m)
        cand = [jnp.where(x == m, -jnp.inf, x) for x in cand]
    top = vals[0]
    thr = 0.5 * (vals[PEER_TOPK - 1] + vals[PEER_TOPK])
    z = functools.reduce(jnp.add, [jnp.exp(v - top) for v in vals[:PEER_TOPK]])
    inv_z = 1.0 / z
    a0 = a_ref[0]
    b0 = b_ref[0]
    for h in range(PEER_HEADS):
        s1 = sc_ref[2 * h]
        s2 = sc_ref[2 * h + 1]
        thr_ref[h] = thr[h:h + 1, :] - s1
        c_ref[h] = jnp.exp(s1 - a0[h:h + 1, :]) * inv_z[h:h + 1, :]
        s2_ref[h] = s2
        e2_ref[h] = jnp.exp(s2 - b0[h:h + 1, :])


def _peer_route(h2, wq_bf16, keys_bf16):
    bsz, seq, _ = h2.shape
    tb = TOK_TILE
    per_head = lambda: pl.BlockSpec((None, PEER_HEADS, N_KEYS, tb), lambda b, i: (b, 0, 0, i))
    head_shape = jax.ShapeDtypeStruct((bsz, PEER_HEADS, N_KEYS, seq), F32)
    qcols = PEER_HEADS * PEER_KEY_DIM
    return pl.pallas_call(
        _peer_route_kernel,
        grid=(bsz, seq // tb),
        in_specs=[pl.BlockSpec((None, tb, D_MODEL), lambda b, i: (b, i, 0)),
                  pl.BlockSpec((D_MODEL, qcols), lambda b, i: (0, 0)),
                  pl.BlockSpec((PEER_HEADS, 2, N_KEYS, PEER_HALF), lambda b, i: (0, 0, 0, 0))],
        out_specs=[pl.BlockSpec((None, D_MODEL, tb), lambda b, i: (b, 0, i)),
                   per_head(), per_head(), per_head(), per_head()],
        out_shape=[jax.ShapeDtypeStruct((bsz, D_MODEL, seq), BF16), head_shape, head_shape, head_shape, head_shape],
        scratch_shapes=[pltpu.VMEM((2 * PEER_HEADS, N_KEYS, tb), F32),
                        pltpu.VMEM((N_RANKS, PEER_HEADS, tb), F32),
                        pltpu.VMEM((N_RANKS, PEER_HEADS, tb), F32)],
        compiler_params=_params(("parallel", "parallel"),
                                D_MODEL * qcols * 2 + 5 * PEER_HEADS * N_KEYS * tb * 4 + tb * qcols * 4),
        name="peer_routing",
    )(h2, wq_bf16, keys_bf16)


def _gelu(x):
    return 0.5 * x * (1.0 + lax.erf(x * (2.0 ** -0.5)))


def _routing_block(s2_ref, e2_ref, thr_ref, c_ref, act_ref, wg_ref, t, k0):
    cols = slice(t * V7X_LANES, (t + 1) * V7X_LANES)
    keys = pl.ds(k0, KEY_CHUNK)
    w = [jnp.zeros((KEY_CHUNK, V7X_LANES), F32) for _ in range(ROWS_PER_TILE)]
    for h in range(PEER_HEADS):
        s2 = s2_ref[h, keys, cols]
        e2 = e2_ref[h, keys, cols]
        for r in range(ROWS_PER_TILE):
            thr = thr_ref[h, r:r + 1, cols]
            coef = c_ref[h, r:r + 1, cols]
            w[r] = w[r] + jnp.where(s2 >= thr, e2 * coef, 0.0)
    for r in range(ROWS_PER_TILE):
        rows = pl.ds(pl.multiple_of(r * N_KEYS + k0, KEY_CHUNK), KEY_CHUNK)
        wg_ref[rows, cols] = (w[r] * _gelu(act_ref[rows, cols])).astype(BF16)


def _peer_dense_kernel(h2t_ref, u_ref, vt_ref, s2_ref, e2_ref, thr0_ref, c0_ref, thr1_ref, c1_ref, x1_ref, mod_ref,
                       y_ref, act0_ref, act1_ref, wg0_ref, wg1_ref, acc_ref):
    g = pl.program_id(2)
    tb = acc_ref.shape[1]
    act_refs = (act0_ref, act1_ref)
    wg_refs = (wg0_ref, wg1_ref)

    @pl.when(g == 0)
    def _():
        for ref in act_refs + wg_refs + (acc_ref,):
            ref[...] = jnp.zeros_like(ref)

    for k in range(2):
        thr_ref, c_ref = (thr0_ref, c0_ref) if k == 0 else (thr1_ref, c1_ref)

        def slab(c, carry, k=k, thr_ref=thr_ref, c_ref=c_ref):
            m0 = pl.multiple_of(c * MXU_SLAB, MXU_SLAB)
            rows = pl.ds(m0, MXU_SLAB)
            act_refs[k][rows, :] = _dot(u_ref[pl.ds(k * EXPERT_TILE + m0, MXU_SLAB), :], h2t_ref[...])
            acc_ref[rows, :] += _dot(vt_ref[rows, k * EXPERT_TILE:(k + 1) * EXPERT_TILE], wg_refs[k][...])
            k0 = pl.multiple_of(c * KEY_CHUNK, KEY_CHUNK)
            for t in range(tb // V7X_LANES):
                _routing_block(s2_ref, e2_ref, thr_ref, c_ref, act_refs[1 - k], wg_refs[1 - k], t, k0)
            return carry

        lax.fori_loop(0, CHUNKS_PER_TILE, slab, 0)

    @pl.when(g == pl.num_programs(2) - 1)
    def _():
        y_ref[...] = x1_ref[...] + mod_ref[:, 5 * D_MODEL:6 * D_MODEL] * acc_ref[...].T


def _peer_dense(h2t, u, vt, s2, e2, thr, coef, x1, mod):
    bsz, seq, _ = x1.shape
    tb = TOK_TILE
    n_tiles = u.shape[0] // EXPERT_TILE
    n_pairs = n_tiles // 2
    once = pl.Buffered(1)
    per_head = lambda: pl.BlockSpec((None, PEER_HEADS, N_KEYS, tb), lambda b, i, g: (b, 0, 0, i), pipeline_mode=once)
    tile_rows = lambda f: pl.BlockSpec((None, PEER_HEADS, ROWS_PER_TILE, tb),
                                       lambda b, i, g: (b, 0, jnp.clip(f(g), 0, n_tiles - 1), i))
    return pl.pallas_call(
        _peer_dense_kernel,
        grid=(bsz, seq // tb, n_pairs + 1),
        in_specs=[pl.BlockSpec((None, D_MODEL, tb), lambda b, i, g: (b, 0, i), pipeline_mode=once),
                  pl.BlockSpec((2 * EXPERT_TILE, D_MODEL), lambda b, i, g: (jnp.minimum(g, n_pairs - 1), 0)),
                  pl.BlockSpec((D_MODEL, 2 * EXPERT_TILE), lambda b, i, g: (0, jnp.maximum(g - 1, 0))),
                  per_head(), per_head(),
                  tile_rows(lambda g: 2 * g - 1), tile_rows(lambda g: 2 * g - 1),
                  tile_rows(lambda g: 2 * g), tile_rows(lambda g: 2 * g),
                  pl.BlockSpec((None, tb, D_MODEL), lambda b, i, g: (b, i, 0), pipeline_mode=once),
                  pl.BlockSpec((None, 1, 6 * D_MODEL), lambda b, i, g: (b, 0, 0))],
        out_specs=pl.BlockSpec((None, tb, D_MODEL), lambda b, i, g: (b, i, 0)),
        out_shape=jax.ShapeDtypeStruct((bsz, seq, D_MODEL), F32),
        scratch_shapes=[pltpu.VMEM((EXPERT_TILE, tb), F32), pltpu.VMEM((EXPERT_TILE, tb), F32),
                        pltpu.VMEM((EXPERT_TILE, tb), BF16), pltpu.VMEM((EXPERT_TILE, tb), BF16),
                        pltpu.VMEM((D_MODEL, tb), F32)],
        compiler_params=_params(("parallel", "parallel", "arbitrary"),
                                2 * PEER_HEADS * N_KEYS * tb * 4 + 2 * 2 * EXPERT_TILE * D_MODEL * 4
                                + 2 * tb * D_MODEL * 4 + 2 * EXPERT_TILE * tb * 4),
        name="peer_dense_experts",
    )(h2t, u, vt, s2, e2, thr, coef, thr, coef, x1, mod)


def _layer(x, c, w_ada, b_ada, norm1_g, norm2_g, w_in, q_norm_g, k_norm_g, bias, sink,
           up_f, b_f, up_b, b_b, beta_attn, beta_gla, w_out, wq, keys, u, vt):
    mod = _modulation(c, w_ada, b_ada)
    qkv, gla_in, gates = _in_projection(x, mod, norm1_g, w_in)
    att = _windowed_attention(qkv, bias, sink, q_norm_g, k_norm_g, beta_attn)
    o_f, o_b = _gla(gla_in, gates, up_f, b_f, up_b, b_b)
    x1, h2 = _out_projection(att, o_f, o_b, gla_in, x, mod, beta_gla, w_out, norm2_g)
    h2t, s2, e2, thr, coef = _peer_route(h2, wq, keys)
    return _peer_dense(h2t, u, vt, s2, e2, thr, coef, x1, mod)


def kernel(x_prompt, x_sample, c_prompt, c_sample, w_ada, b_ada, norm1_g, norm2_g, w_in, q_norm_g, k_norm_g,
           rel_bias, sink, gk_up_fwd, gk_bias_fwd, gk_up_bwd, gk_bias_bwd, beta_attn, beta_gla, w_out,
           peer_query, peer_subkeys, peer_u, peer_v):
    depth = w_ada.shape[0]
    bias = _band_bias(rel_bias)
    y_prompt, y_sample = x_prompt, x_sample
    for l in range(depth):
        params = (w_ada[l], b_ada[l], norm1_g[l], norm2_g[l], w_in[l].astype(BF16), q_norm_g[l], k_norm_g[l],
                  bias, sink[l], gk_up_fwd[l], gk_bias_fwd[l], gk_up_bwd[l], gk_bias_bwd[l],
                  beta_attn[l], beta_gla[l], w_out[l].astype(BF16), peer_query[l].astype(BF16),
                  peer_subkeys[l].astype(BF16), peer_u[l], peer_v[l].T)
        y_prompt = _layer(y_prompt, c_prompt, *params)
        y_sample = _layer(y_sample, c_sample, *params)
    return (y_prompt, y_sample)
```

```python
import functools
import math

import numpy as np
import jax
import jax.numpy as jnp
from jax import lax
from jax.experimental import pallas as pl
from jax.experimental.pallas import tpu as pltpu

F32 = jnp.float32
BF16 = jnp.bfloat16

D_MODEL = 1024
ATTN_HEADS = 8
ATTN_KV_HEADS = 2
ATTN_HEAD_DIM = 64
ATTN_GROUP = ATTN_HEADS // ATTN_KV_HEADS
WINDOW = 128
BLOCK = 128
N_BUCKETS = 32
MAX_DISTANCE = 128
GLA_HEADS = 4
GLA_KEY_DIM = 64
GLA_VAL_DIM = 128
GLA_GATE_RANK = 16
GLA_GATE_NORM = 16.0
GLA_CHUNK = 64
PEER_HEADS = 8
N_KEYS = 128
PEER_KEY_DIM = 256
PEER_HALF = PEER_KEY_DIM // 2
PEER_TOPK = 16
EPS = 1e-6

ATTN_Q = ATTN_HEADS * ATTN_HEAD_DIM
ATTN_KV = ATTN_KV_HEADS * ATTN_HEAD_DIM
GLA_QK = GLA_HEADS * GLA_KEY_DIM
GLA_V = GLA_HEADS * GLA_VAL_DIM
ATTN_COLS = ATTN_Q + 2 * ATTN_KV
GLA_COLS = 2 * GLA_QK + 2 * GLA_V
GATE_COLS = 2 * GLA_GATE_RANK
D_IN = ATTN_COLS + GLA_COLS + GATE_COLS

V7X_LANES = 128
V7X_SUBLANES = 8
V7X_VMEM_LIMIT_CAP = 60 * 1024 * 1024

NEG_BIG = -1e30

TOK_TILE = 512
GLA_TILE = 512
EXPERT_TILE = 1024
ROWS_PER_TILE = EXPERT_TILE // N_KEYS
KEY_CHUNK = 32
CHUNKS_PER_TILE = N_KEYS // KEY_CHUNK
MXU_SLAB = EXPERT_TILE // CHUNKS_PER_TILE
N_RANKS = PEER_TOPK + 1
CAND_PAIRS = tuple((i, j) for i in range(N_RANKS) for j in range(N_RANKS) if (i + 1) * (j + 1) <= N_RANKS)


def _vmem_limit(block_bytes):
    return int(min(2 * block_bytes + (16 << 20), V7X_VMEM_LIMIT_CAP))


def _params(semantics, block_bytes):
    return pltpu.CompilerParams(dimension_semantics=semantics, vmem_limit_bytes=_vmem_limit(block_bytes))


def _nt_dot(a, b):
    return lax.dot_general(a, b, (((1,), (1,)), ((), ())), preferred_element_type=F32)


def _dot(a, b):
    return jnp.dot(a, b, preferred_element_type=F32)


def _rms(x, g):
    return x * lax.rsqrt(jnp.mean(x * x, axis=-1, keepdims=True) + EPS) * g


def _mod_kernel(c_ref, w_ref, b_ref, o_ref):
    c = c_ref[...]
    s = c * jax.nn.sigmoid(c)
    o_ref[...] = _dot(s.astype(BF16), w_ref[...].astype(BF16)) + b_ref[...]


def _modulation(c, w_ada, b_ada):
    bsz = c.shape[0]
    rows = -(-bsz // V7X_SUBLANES) * V7X_SUBLANES
    cp = jnp.zeros((rows, D_MODEL), F32).at[:bsz].set(c)
    n_out = w_ada.shape[1]
    tile = D_MODEL
    out = pl.pallas_call(
        _mod_kernel,
        grid=(n_out // tile,),
        in_specs=[pl.BlockSpec((rows, D_MODEL), lambda j: (0, 0)),
                  pl.BlockSpec((D_MODEL, tile), lambda j: (0, j)),
                  pl.BlockSpec((1, tile), lambda j: (0, j))],
        out_specs=pl.BlockSpec((rows, tile), lambda j: (0, j)),
        out_shape=jax.ShapeDtypeStruct((rows, n_out), F32),
        compiler_params=_params(("parallel",), D_MODEL * tile * 4),
        name="adaln_modulation",
    )(cp, w_ada, b_ada.reshape(1, n_out))
    return out[:bsz].reshape(bsz, 1, n_out)


def _inproj_kernel(x_ref, mod_ref, g_ref, w_ref, oa_ref, og_ref, or_ref):
    h = _rms(x_ref[...], g_ref[...])
    h = h * (1.0 + mod_ref[:, D_MODEL:2 * D_MODEL]) + mod_ref[:, 0:D_MODEL]
    p = _dot(h.astype(BF16), w_ref[...])
    oa_ref[...] = p[:, :ATTN_COLS]
    og_ref[...] = p[:, ATTN_COLS:ATTN_COLS + GLA_COLS]
    or_ref[...] = p[:, ATTN_COLS + GLA_COLS:]


def _in_projection(x, mod, norm_g, w_in_bf16):
    bsz, seq, _ = x.shape
    tb = TOK_TILE
    tok = lambda cols: pl.BlockSpec((None, tb, cols), lambda b, i: (b, i, 0))
    return pl.pallas_call(
        _inproj_kernel,
        grid=(bsz, seq // tb),
        in_specs=[tok(D_MODEL),
                  pl.BlockSpec((None, 1, 6 * D_MODEL), lambda b, i: (b, 0, 0)),
                  pl.BlockSpec((1, D_MODEL), lambda b, i: (0, 0)),
                  pl.BlockSpec((D_MODEL, D_IN), lambda b, i: (0, 0))],
        out_specs=[tok(ATTN_COLS), tok(GLA_COLS), tok(GATE_COLS)],
        out_shape=[jax.ShapeDtypeStruct((bsz, seq, ATTN_COLS), F32),
                   jax.ShapeDtypeStruct((bsz, seq, GLA_COLS), F32),
                   jax.ShapeDtypeStruct((bsz, seq, GATE_COLS), F32)],
        compiler_params=_params(("parallel", "parallel"),
                                tb * D_MODEL * 4 + D_MODEL * D_IN * 2 + 2 * tb * D_IN * 4),
        name="norm1_in_projection",
    )(x, mod, norm_g.reshape(1, D_MODEL), w_in_bf16)


def _t5_buckets(rel):
    nb = N_BUCKETS // 2
    ret = (rel > 0).astype(np.int32) * nb
    n = np.abs(rel)
    max_exact = nb // 2
    large = max_exact + (np.log(np.maximum(n, 1) / max_exact) / math.log(MAX_DISTANCE / max_exact)
                         * (nb - max_exact)).astype(np.int32)
    large = np.minimum(large, nb - 1)
    return (ret + np.where(n < max_exact, n, large)).astype(np.int32)


def _band_bias(rel_bias):
    span = 4 * BLOCK
    offsets = np.arange(span) - (2 * BLOCK - 1)
    per_offset = rel_bias[_t5_buckets(offsets)].astype(F32).T
    shifted = jnp.roll(per_offset, -(BLOCK - 1), axis=1)
    skew = jnp.tile(shifted, (1, BLOCK))[:, :BLOCK * (span - 1)].reshape(-1, BLOCK, span - 1)
    rel = np.arange(3 * BLOCK)[None, :] - BLOCK - np.arange(BLOCK)[:, None]
    band = jnp.asarray(np.abs(rel) <= WINDOW)
    return jnp.where(band[None], skew[:, :, :3 * BLOCK], NEG_BIG)


def _attn_kernel(sink_ref, q_ref, kvp_ref, kvc_ref, kvn_ref, bias_ref, qg_ref, kg_ref, beta_ref, o_ref):
    n = pl.program_id(1)
    last = pl.num_programs(1) - 1
    q = q_ref[...]
    kv = jnp.concatenate([kvp_ref[...], kvc_ref[...], kvn_ref[...]], axis=0)
    j = lax.broadcasted_iota(jnp.int32, (BLOCK, 3 * BLOCK), 1)
    valid = jnp.logical_and(jnp.logical_or(j >= BLOCK, n > 0), jnp.logical_or(j < 2 * BLOCK, n < last))
    scale = ATTN_HEAD_DIM ** -0.5
    outs = []
    for hk in range(ATTN_KV_HEADS):
        kh = _rms(kv[:, hk * ATTN_HEAD_DIM:(hk + 1) * ATTN_HEAD_DIM], kg_ref[...]).astype(BF16)
        vh = kv[:, ATTN_KV + hk * ATTN_HEAD_DIM:ATTN_KV + (hk + 1) * ATTN_HEAD_DIM].astype(BF16)
        for g in range(ATTN_GROUP):
            h = hk * ATTN_GROUP + g
            qh = _rms(q[:, h * ATTN_HEAD_DIM:(h + 1) * ATTN_HEAD_DIM], qg_ref[...]).astype(BF16)
            s = _nt_dot(qh, kh) * scale
            s = jnp.where(valid, s + bias_ref[h], NEG_BIG)
            sink = sink_ref[h]
            m = jnp.maximum(jnp.max(s, axis=-1, keepdims=True), sink)
            p = jnp.exp(s - m)
            den = jnp.sum(p, axis=-1, keepdims=True) + jnp.exp(sink - m)
            o = _dot(p.astype(BF16), vh) / den
            outs.append(_rms(o, beta_ref[:, h * ATTN_HEAD_DIM:(h + 1) * ATTN_HEAD_DIM]))
    o_ref[...] = jnp.concatenate(outs, axis=-1)


def _windowed_attention(qkv, bias, sink, q_norm_g, k_norm_g, beta_attn):
    bsz, seq, _ = qkv.shape
    nb = seq // BLOCK
    kv_cols = 2 * ATTN_KV
    kv_blk = ATTN_Q // kv_cols
    kv_spec = lambda f: pl.BlockSpec((None, BLOCK, kv_cols), lambda b, n: (b, f(n), kv_blk))
    const = lambda shape: pl.BlockSpec(shape, lambda b, n: tuple(0 for _ in shape))
    return pl.pallas_call(
        _attn_kernel,
        grid=(bsz, nb),
        in_specs=[pl.BlockSpec(memory_space=pltpu.SMEM),
                  pl.BlockSpec((None, BLOCK, ATTN_Q), lambda b, n: (b, n, 0)),
                  kv_spec(lambda n: jnp.maximum(n - 1, 0)),
                  kv_spec(lambda n: n),
                  kv_spec(lambda n: jnp.minimum(n + 1, nb - 1)),
                  const((ATTN_HEADS, BLOCK, 3 * BLOCK)),
                  const((1, ATTN_HEAD_DIM)), const((1, ATTN_HEAD_DIM)), const((1, ATTN_Q))],
        out_specs=pl.BlockSpec((None, BLOCK, ATTN_Q), lambda b, n: (b, n, 0)),
        out_shape=jax.ShapeDtypeStruct((bsz, seq, ATTN_Q), F32),
        compiler_params=_params(("parallel", "parallel"), ATTN_HEADS * BLOCK * 3 * BLOCK * 4 + 8 * BLOCK * ATTN_Q * 4),
        name="windowed_attention",
    )(sink, qkv, qkv, qkv, qkv, bias, q_norm_g.reshape(1, -1), k_norm_g.reshape(1, -1), beta_attn.reshape(1, -1))


def _split3(x):
    hi = x.astype(BF16)
    r = x - hi.astype(F32)
    mid = r.astype(BF16)
    lo = (r - mid.astype(F32)).astype(BF16)
    return hi, mid, lo


def _gla_direction(qk_ref, v_ref, gr, up_ref, gb_ref, tri_ref, o_ref, st_ref, b_ref, reverse):
    c = GLA_CHUNK
    nch = GLA_TILE // c
    z = _dot(gr.astype(BF16), up_ref[...]) + gb_ref[...]
    log_a = (jnp.minimum(z, 0.0) - jnp.log1p(jnp.exp(-jnp.abs(z)))) / GLA_GATE_NORM
    hi, mid, lo = _split3(log_a)
    tri = tri_ref[...]
    b_ref[...] = _dot(tri, hi) + _dot(tri, mid) + _dot(tri, lo)

    row = lax.broadcasted_iota(jnp.int32, (c, GLA_QK), 0)
    lane = lax.broadcasted_iota(jnp.int32, (c, GLA_QK), 1)
    key_in_head = lane % GLA_KEY_DIM
    causal = (key_in_head >= row) if reverse else (key_in_head <= row)
    qk_head = lane // GLA_KEY_DIM
    v_head = lax.broadcasted_iota(jnp.int32, (c, GLA_V), 1) // GLA_VAL_DIM
    st_rows = lax.broadcasted_iota(jnp.int32, (GLA_V, GLA_QK), 0) // GLA_VAL_DIM
    st_cols = lax.broadcasted_iota(jnp.int32, (GLA_V, GLA_QK), 1) // GLA_KEY_DIM
    same_head = st_rows == st_cols

    for ci in range(nch):
        blk = nch - 1 - ci if reverse else ci
        rows = slice(blk * c, (blk + 1) * c)
        b = b_ref[rows, :]
        b_last = b[0:1, :] if reverse else b[c - 1:c, :]
        q = qk_ref[rows, 0:GLA_QK]
        k = qk_ref[rows, GLA_QK:2 * GLA_QK]
        v = v_ref[rows, :]
        q_dec = (q * ((GLA_KEY_DIM ** -0.5) * jnp.exp(b))).astype(BF16)
        k_dec = k * jnp.exp(-b)
        k_end = (k * jnp.exp(b_last - b)).astype(BF16)
        decay = jnp.exp(b_last)
        k_heads = jnp.concatenate([jnp.where(qk_head == h, k_dec, 0.0) for h in range(GLA_HEADS)], axis=0)
        v_heads = jnp.concatenate([jnp.where(v_head == h, v, 0.0) for h in range(GLA_HEADS)], axis=0)
        a = jnp.where(causal, _nt_dot(q_dec, k_heads.astype(BF16)), 0.0)
        st = st_ref[...]
        o_ref[rows, :] = _dot(a.astype(BF16), v_heads.astype(BF16)) + _nt_dot(q_dec, st.astype(BF16))
        kv_t = lax.dot_general(v.astype(BF16), k_end, (((0,), (0,)), ((), ())), preferred_element_type=F32)
        st_ref[...] = st * decay + jnp.where(same_head, kv_t, 0.0)


def _gla_kernel(qkf_ref, vf_ref, grf_ref, qkb_ref, vb_ref, grb_ref, upf_ref, bf_ref, upb_ref, bb_ref,
                trif_ref, trib_ref, of_ref, ob_ref, stf_ref, stb_ref, cumf_ref, cumb_ref):
    @pl.when(pl.program_id(1) == 0)
    def _():
        stf_ref[...] = jnp.zeros_like(stf_ref)
        stb_ref[...] = jnp.zeros_like(stb_ref)

    _gla_direction(qkf_ref, vf_ref, grf_ref[:, 0:GLA_GATE_RANK], upf_ref, bf_ref, trif_ref,
                   of_ref, stf_ref, cumf_ref, False)
    _gla_direction(qkb_ref, vb_ref, grb_ref[:, GLA_GATE_RANK:2 * GLA_GATE_RANK], upb_ref, bb_ref, trib_ref,
                   ob_ref, stb_ref, cumb_ref, True)


def _gla(gla_in, gates, up_f, b_f, up_b, b_b):
    bsz, seq, _ = gla_in.shape
    tb = GLA_TILE
    nblk = seq // tb
    fwd = lambda cols, cb: pl.BlockSpec((None, tb, cols), lambda b, i: (b, i, cb))
    bwd = lambda cols, cb: pl.BlockSpec((None, tb, cols), lambda b, i: (b, nblk - 1 - i, cb))
    const = lambda shape: pl.BlockSpec(shape, lambda b, i: tuple(0 for _ in shape))
    chunk_of = np.arange(tb) // GLA_CHUNK
    same_chunk = chunk_of[:, None] == chunk_of[None, :]
    pos = np.arange(tb)
    tri_f = jnp.asarray(same_chunk & (pos[None, :] <= pos[:, None]), BF16)
    tri_b = jnp.asarray(same_chunk & (pos[None, :] >= pos[:, None]), BF16)
    state = pltpu.VMEM((GLA_V, GLA_QK), F32)
    cum = pltpu.VMEM((tb, GLA_QK), F32)
    return pl.pallas_call(
        _gla_kernel,
        grid=(bsz, nblk),
        in_specs=[fwd(2 * GLA_QK, 0), fwd(GLA_V, 1), fwd(GATE_COLS, 0),
                  bwd(2 * GLA_QK, 0), bwd(GLA_V, 1), bwd(GATE_COLS, 0),
                  const((GLA_GATE_RANK, GLA_QK)), const((1, GLA_QK)),
                  const((GLA_GATE_RANK, GLA_QK)), const((1, GLA_QK)),
                  const((tb, tb)), const((tb, tb))],
        out_specs=[pl.BlockSpec((None, tb, GLA_V), lambda b, i: (b, i, 0)),
                   pl.BlockSpec((None, tb, GLA_V), lambda b, i: (b, nblk - 1 - i, 0))],
        out_shape=[jax.ShapeDtypeStruct((bsz, seq, GLA_V), F32)] * 2,
        scratch_shapes=[state, state, cum, cum],
        compiler_params=_params(("parallel", "arbitrary"), 2 * tb * (2 * GLA_QK + 2 * GLA_V + V7X_LANES) * 4),
        name="gla_bidirectional",
    )(gla_in, gla_in, gates, gla_in, gla_in, gates,
      up_f.astype(BF16), b_f.reshape(1, -1), up_b.astype(BF16), b_b.reshape(1, -1), tri_f, tri_b)


def _outproj_kernel(att_ref, of_ref, ob_ref, og_ref, x_ref, mod_ref, beta_ref, w_ref, g2_ref, x1_ref, h2_ref):
    o = of_ref[...] + ob_ref[...]
    gate = og_ref[...]
    gate = gate * jax.nn.sigmoid(gate)
    parts = [att_ref[...].astype(BF16)]
    for h in range(GLA_HEADS):
        sl = slice(h * GLA_VAL_DIM, (h + 1) * GLA_VAL_DIM)
        parts.append((_rms(o[:, sl], beta_ref[:, sl]) * gate[:, sl]).astype(BF16))
    mix = _dot(jnp.concatenate(parts, axis=-1), w_ref[...])
    d = D_MODEL
    x1 = x_ref[...] + mod_ref[:, 2 * d:3 * d] * mix
    x1_ref[...] = x1
    h2 = _rms(x1, g2_ref[...]) * (1.0 + mod_ref[:, 4 * d:5 * d]) + mod_ref[:, 3 * d:4 * d]
    h2_ref[...] = h2.astype(BF16)


def _out_projection(att, o_f, o_b, gla_in, x, mod, beta_gla, w_out_bf16, norm2_g):
    bsz, seq, _ = x.shape
    tb = TOK_TILE
    tok = lambda cols, cb=0: pl.BlockSpec((None, tb, cols), lambda b, i: (b, i, cb))
    const = lambda shape: pl.BlockSpec(shape, lambda b, i: tuple(0 for _ in shape))
    return pl.pallas_call(
        _outproj_kernel,
        grid=(bsz, seq // tb),
        in_specs=[tok(ATTN_Q), tok(GLA_V), tok(GLA_V), tok(GLA_V, (2 * GLA_QK + GLA_V) // GLA_V), tok(D_MODEL),
                  pl.BlockSpec((None, 1, 6 * D_MODEL), lambda b, i: (b, 0, 0)),
                  const((1, GLA_V)), const((D_MODEL, D_MODEL)), const((1, D_MODEL))],
        out_specs=[tok(D_MODEL), tok(D_MODEL)],
        out_shape=[jax.ShapeDtypeStruct((bsz, seq, D_MODEL), F32),
                   jax.ShapeDtypeStruct((bsz, seq, D_MODEL), BF16)],
        compiler_params=_params(("parallel", "parallel"), tb * (4 * GLA_V + 3 * D_MODEL) * 4 + D_MODEL * D_MODEL * 2),
        name="out_projection_norm2",
    )(att, o_f, o_b, gla_in, x, mod, beta_gla.reshape(1, -1), w_out_bf16, norm2_g.reshape(1, -1))


def _sort_network(n):
    def merge(lo, hi, r):
        step = 2 * r
        if step < hi - lo:
            yield from merge(lo, hi, step)
            yield from merge(lo + r, hi, step)
            yield from ((i, i + r) for i in range(lo + r, hi - r, step))
        else:
            yield (lo, lo + r)

    def sort(lo, hi):
        if hi > lo:
            mid = lo + (hi - lo) // 2
            yield from sort(lo, mid)
            yield from sort(mid + 1, hi)
            yield from merge(lo, hi, 1)

    return tuple(sort(0, n - 1))


def _top_values(x, out_ref, head):
    n = N_KEYS // V7X_SUBLANES
    assert n == PEER_TOPK
    tiles =[x[i * V7X_SUBLANES:(i + 1) * V7X_SUBLANES, :] for i in range(n)]
    srt = list(tiles)
    for i, j in _sort_network(n):
        srt[i], srt[j] = jnp.maximum(srt[i], srt[j]), jnp.minimum(srt[i], srt[j])
    shift = V7X_SUBLANES // 2
    while shift >= 1:
        other = [pltpu.roll(t, shift, 0) for t in srt]
        srt = [jnp.maximum(srt[i], other[n - 1 - i]) for i in range(n)]
        stride = n // 2
        while stride >= 1:
            for i in range(n):
                if i & stride == 0:
                    srt[i], srt[i + stride] = (jnp.maximum(srt[i], srt[i + stride]),
                                               jnp.minimum(srt[i], srt[i + stride]))
            stride //= 2
        shift //= 2
    for r in range(PEER_TOPK):
        out_ref[r, pl.ds(head, 1), :] = srt[r][0:1, :]
    below = [jnp.where(t < srt[PEER_TOPK - 1], t, -jnp.inf) for t in tiles]
    out_ref[PEER_TOPK, pl.ds(head, 1), :] = jnp.max(functools.reduce(jnp.maximum, below), axis=0, keepdims=True)


def _peer_route_kernel(h2_ref, wq_ref, keys_ref, h2t_ref, s2_ref, e2_ref, thr_ref, c_ref, sc_ref, a_ref, b_ref):
    h2 = h2_ref[...]
    h2t_ref[...] = h2.astype(F32).T.astype(BF16)
    q = _dot(h2, wq_ref[...]).astype(BF16)
    for h in range(PEER_HEADS):
        for p in range(2):
            blk = h * 2 + p
            s = _nt_dot(keys_ref[h, p], q[:, blk * PEER_HALF:(blk + 1) * PEER_HALF])
            sc_ref[blk] = s
            _top_values(s, a_ref if p == 0 else b_ref, h)
    cand = [a_ref[i] + b_ref[j] for (i, j) in CAND_PAIRS]
    vals = []
    for r in range(N_RANKS):
        m = functools.reduce(jnp.maximum, cand)
        vals.append(m)
        cand = [jnp.where(x == m, -jnp.inf, x) for x in cand]
    top = vals[0]
    thr = 0.5 * (vals[PEER_TOPK - 1] + vals[PEER_TOPK])
    z = functools.reduce(jnp.add, [jnp.exp(v - top) for v in vals[:PEER_TOPK]])
    inv_z = 1.0 / z
    a0 = a_ref[0]
    b0 = b_ref[0]
    for h in range(PEER_HEADS):
        s1 = sc_ref[2 * h]
        s2 = sc_ref[2 * h + 1]
        thr_ref[h] = thr[h:h + 1, :] - s1
        c_ref[h] = jnp.exp(s1 - a0[h:h + 1, :]) * inv_z[h:h + 1, :]
        s2_ref[h] = s2
        e2_ref[h] = jnp.exp(s2 - b0[h:h + 1, :])


def _peer_route(h2, wq_bf16, keys_bf16):
    bsz, seq, _ = h2.shape
    tb = TOK_TILE
    per_head = lambda: pl.BlockSpec((None, PEER_HEADS, N_KEYS, tb), lambda b, i: (b, 0, 0, i))
    head_shape = jax.ShapeDtypeStruct((bsz, PEER_HEADS, N_KEYS, seq), F32)
    qcols = PEER_HEADS * PEER_KEY_DIM
    return pl.pallas_call(
        _peer_route_kernel,
        grid=(bsz, seq // tb),
        in_specs=[pl.BlockSpec((None, tb, D_MODEL), lambda b, i: (b, i, 0)),
                  pl.BlockSpec((D_MODEL, qcols), lambda b, i: (0, 0)),
                  pl.BlockSpec((PEER_HEADS, 2, N_KEYS, PEER_HALF), lambda b, i: (0, 0, 0, 0))],
        out_specs=[pl.BlockSpec((None, D_MODEL, tb), lambda b, i: (b, 0, i)),
                   per_head(), per_head(), per_head(), per_head()],
        out_shape=[jax.ShapeDtypeStruct((bsz, D_MODEL, seq), BF16), head_shape, head_shape, head_shape, head_shape],
        scratch_shapes=[pltpu.VMEM((2 * PEER_HEADS, N_KEYS, tb), F32),
                        pltpu.VMEM((N_RANKS, PEER_HEADS, tb), F32),
                        pltpu.VMEM((N_RANKS, PEER_HEADS, tb), F32)],
        compiler_params=_params(("parallel", "parallel"),
                                D_MODEL * qcols * 2 + 5 * PEER_HEADS * N_KEYS * tb * 4 + tb * qcols * 4),
        name="peer_routing",
    )(h2, wq_bf16, keys_bf16)


def _gelu(x):
    return 0.5 * x * (1.0 + lax.erf(x * (2.0 ** -0.5)))


def _routing_block(s2_ref, e2_ref, thr_ref, c_ref, act_ref, wg_ref, t, k0):
    cols = slice(t * V7X_LANES, (t + 1) * V7X_LANES)
    keys = pl.ds(k0, KEY_CHUNK)
    w = [jnp.zeros((KEY_CHUNK, V7X_LANES), F32) for _ in range(ROWS_PER_TILE)]
    for h in range(PEER_HEADS):
        s2 = s2_ref[h, keys, cols]
        e2 = e2_ref[h, keys, cols]
        for r in range(ROWS_PER_TILE):
            thr = thr_ref[h, r:r + 1, cols]
            coef = c_ref[h, r:r + 1, cols]
            w[r] = w[r] + jnp.where(s2 >= thr, e2 * coef, 0.0)
    for r in range(ROWS_PER_TILE):
        rows = pl.ds(pl.multiple_of(r * N_KEYS + k0, KEY_CHUNK), KEY_CHUNK)
        wg_ref[rows, cols] = (w[r] * _gelu(act_ref[rows, cols])).astype(BF16)


def _peer_dense_kernel(h2t_ref, u_ref, vt_ref, s2_ref, e2_ref, thr0_ref, c0_ref, thr1_ref, c1_ref, x1_ref, mod_ref,
                       y_ref, act0_ref, act1_ref, wg0_ref, wg1_ref, acc_ref):
    g = pl.program_id(2)
    tb = acc_ref.shape[1]
    act_refs = (act0_ref, act1_ref)
    wg_refs = (wg0_ref, wg1_ref)

    @pl.when(g == 0)
    def _():
        for ref in act_refs + wg_refs + (acc_ref,):
            ref[...] = jnp.zeros_like(ref)

    for k in range(2):
        thr_ref, c_ref = (thr0_ref, c0_ref) if k == 0 else (thr1_ref, c1_ref)

        def slab(c, carry, k=k, thr_ref=thr_ref, c_ref=c_ref, with_stage_b=True):
            m0 = pl.multiple_of(c * MXU_SLAB, MXU_SLAB)
            rows = pl.ds(m0, MXU_SLAB)
            act_refs[k][rows, :] = _dot(u_ref[pl.ds(k * EXPERT_TILE + m0, MXU_SLAB), :], h2t_ref[...])
            acc_ref[rows, :] += _dot(vt_ref[rows, k * EXPERT_TILE:(k + 1) * EXPERT_TILE], wg_refs[k][...])
            if with_stage_b:
                k0 = pl.multiple_of(c * KEY_CHUNK, KEY_CHUNK)
                for t in range(tb // V7X_LANES):
                    _routing_block(s2_ref, e2_ref, thr_ref, c_ref, act_refs[1 - k], wg_refs[1 - k], t, k0)
            return carry

        has_tile = (g > 0) if k == 0 else (g < pl.num_programs(2) - 1)

        @pl.when(has_tile)
        def _(slab=slab):
            lax.fori_loop(0, CHUNKS_PER_TILE, slab, 0)

        @pl.when(jnp.logical_not(has_tile))
        def _(slab=slab):
            lax.fori_loop(0, CHUNKS_PER_TILE, functools.partial(slab, with_stage_b=False), 0)

    @pl.when(g == pl.num_programs(2) - 1)
    def _():
        y_ref[...] = x1_ref[...] + mod_ref[:, 5 * D_MODEL:6 * D_MODEL] * acc_ref[...].T


def _transposed_blocks(v):
    blk = 2 * EXPERT_TILE
    return jnp.transpose(v.reshape(v.shape[0] // blk, blk, v.shape[1]), (0, 2, 1))


def _peer_dense(h2t, u, vt, s2, e2, thr, coef, x1, mod):
    bsz, seq, _ = x1.shape
    tb = TOK_TILE
    n_tiles = u.shape[0] // EXPERT_TILE
    n_pairs = n_tiles // 2
    once = pl.Buffered(1)
    per_head = lambda: pl.BlockSpec((None, PEER_HEADS, N_KEYS, tb), lambda b, i, g: (b, 0, 0, i), pipeline_mode=once)
    tile_rows = lambda f: pl.BlockSpec((None, PEER_HEADS, ROWS_PER_TILE, tb),
                                       lambda b, i, g: (b, 0, jnp.clip(f(g), 0, n_tiles - 1), i))
    return pl.pallas_call(
        _peer_dense_kernel,
        grid=(bsz, seq // tb, n_pairs + 1),
        in_specs=[pl.BlockSpec((None, D_MODEL, tb), lambda b, i, g: (b, 0, i), pipeline_mode=once),
                  pl.BlockSpec((2 * EXPERT_TILE, D_MODEL), lambda b, i, g: (jnp.minimum(g, n_pairs - 1), 0)),
                  pl.BlockSpec((None, D_MODEL, 2 * EXPERT_TILE), lambda b, i, g: (jnp.maximum(g - 1, 0), 0, 0)),
                  per_head(), per_head(),
                  tile_rows(lambda g: 2 * g - 1), tile_rows(lambda g: 2 * g - 1),
                  tile_rows(lambda g: 2 * g), tile_rows(lambda g: 2 * g),
                  pl.BlockSpec((None, tb, D_MODEL), lambda b, i, g: (b, i, 0), pipeline_mode=once),
                  pl.BlockSpec((None, 1, 6 * D_MODEL), lambda b, i, g: (b, 0, 0))],
        out_specs=pl.BlockSpec((None, tb, D_MODEL), lambda b, i, g: (b, i, 0)),
        out_shape=jax.ShapeDtypeStruct((bsz, seq, D_MODEL), F32),
        scratch_shapes=[pltpu.VMEM((EXPERT_TILE, tb), F32), pltpu.VMEM((EXPERT_TILE, tb), F32),
                        pltpu.VMEM((EXPERT_TILE, tb), BF16), pltpu.VMEM((EXPERT_TILE, tb), BF16),
                        pltpu.VMEM((D_MODEL, tb), F32)],
        compiler_params=_params(("parallel", "parallel", "arbitrary"),
                                2 * PEER_HEADS * N_KEYS * tb * 4 + 2 * 2 * EXPERT_TILE * D_MODEL * 4
                                + 2 * tb * D_MODEL * 4 + 2 * EXPERT_TILE * tb * 4),
        name="peer_dense_experts",
    )(h2t, u, vt, s2, e2, thr, coef, thr, coef, x1, mod)


def _layer(x, c, w_ada, b_ada, norm1_g, norm2_g, w_in, q_norm_g, k_norm_g, bias, sink,
           up_f, b_f, up_b, b_b, beta_attn, beta_gla, w_out, wq, keys, u, vt):
    mod = _modulation(c, w_ada, b_ada)
    qkv, gla_in, gates = _in_projection(x, mod, norm1_g, w_in)
    att = _windowed_attention(qkv, bias, sink, q_norm_g, k_norm_g, beta_attn)
    o_f, o_b = _gla(gla_in, gates, up_f, b_f, up_b, b_b)
    x1, h2 = _out_projection(att, o_f, o_b, gla_in, x, mod, beta_gla, w_out, norm2_g)
    h2t, s2, e2, thr, coef = _peer_route(h2, wq, keys)
    return _peer_dense(h2t, u, vt, s2, e2, thr, coef, x1, mod)


def kernel(x_prompt, x_sample, c_prompt, c_sample, w_ada, b_ada, norm1_g, norm2_g, w_in, q_norm_g, k_norm_g,
           rel_bias, sink, gk_up_fwd, gk_bias_fwd, gk_up_bwd, gk_bias_bwd, beta_attn, beta_gla, w_out,
           peer_query, peer_subkeys, peer_u, peer_v):
    depth = w_ada.shape[0]
    bias = _band_bias(rel_bias)
    y_prompt, y_sample = x_prompt, x_sample
    for l in range(depth):
        params = (w_ada[l], b_ada[l], norm1_g[l], norm2_g[l], w_in[l].astype(BF16), q_norm_g[l], k_norm_g[l],
                  bias, sink[l], gk_up_fwd[l], gk_bias_fwd[l], gk_up_bwd[l], gk_bias_bwd[l],
                  beta_attn[l], beta_gla[l], w_out[l].astype(BF16), peer_query[l].astype(BF16),
                  peer_subkeys[l].astype(BF16), peer_u[l], _transposed_blocks(peer_v[l]))
        y_prompt = _layer(y_prompt, c_prompt, *params)
        y_sample = _layer(y_sample, c_sample, *params)
    return (y_prompt, y_sample)
```

```python
import functools
import math

import numpy as np
import jax
import jax.numpy as jnp
from jax import lax
from jax.experimental import pallas as pl
from jax.experimental.pallas import tpu as pltpu

F32 = jnp.float32
BF16 = jnp.bfloat16

D_MODEL = 1024
ATTN_HEADS = 8
ATTN_KV_HEADS = 2
ATTN_HEAD_DIM = 64
ATTN_GROUP = ATTN_HEADS // ATTN_KV_HEADS
WINDOW = 128
BLOCK = 128
N_BUCKETS = 32
MAX_DISTANCE = 128
GLA_HEADS = 4
GLA_KEY_DIM = 64
GLA_VAL_DIM = 128
GLA_GATE_RANK = 16
GLA_GATE_NORM = 16.0
GLA_CHUNK = 64
PEER_HEADS = 8
N_KEYS = 128
PEER_KEY_DIM = 256
PEER_HALF = PEER_KEY_DIM // 2
PEER_TOPK = 16
EPS = 1e-6

ATTN_Q = ATTN_HEADS * ATTN_HEAD_DIM
ATTN_KV = ATTN_KV_HEADS * ATTN_HEAD_DIM
GLA_QK = GLA_HEADS * GLA_KEY_DIM
GLA_V = GLA_HEADS * GLA_VAL_DIM
ATTN_COLS = ATTN_Q + 2 * ATTN_KV
GLA_COLS = 2 * GLA_QK + 2 * GLA_V
GATE_COLS = 2 * GLA_GATE_RANK
D_IN = ATTN_COLS + GLA_COLS + GATE_COLS

V7X_LANES = 128
V7X_SUBLANES = 8
V7X_VMEM_LIMIT_CAP = 60 * 1024 * 1024

NEG_BIG = -1e30

TOK_TILE = 512
GLA_TILE = 512
PEER_TOK_TILE = 1024
PEER_PASS_TOK = TOK_TILE
EXPERT_TILE = 512
ROWS_PER_TILE = EXPERT_TILE // N_KEYS
ROW_BLOCK = 2 * ROWS_PER_TILE
KEY_CHUNK = 64
CHUNKS_PER_TILE = N_KEYS // KEY_CHUNK
ACT_SLAB = EXPERT_TILE // CHUNKS_PER_TILE
OUT_SLAB = D_MODEL // CHUNKS_PER_TILE
N_RANKS = PEER_TOPK + 1
CAND_PAIRS = tuple((i, j) for i in range(N_RANKS) for j in range(N_RANKS) if (i + 1) * (j + 1) <= N_RANKS)


def _vmem_limit(block_bytes):
    return int(min(2 * block_bytes + (16 << 20), V7X_VMEM_LIMIT_CAP))


def _params(semantics, block_bytes):
    return pltpu.CompilerParams(dimension_semantics=semantics, vmem_limit_bytes=_vmem_limit(block_bytes))


def _nt_dot(a, b):
    return lax.dot_general(a, b, (((1,), (1,)), ((), ())), preferred_element_type=F32)


def _dot(a, b):
    return jnp.dot(a, b, preferred_element_type=F32)


def _rms(x, g):
    return x * lax.rsqrt(jnp.mean(x * x, axis=-1, keepdims=True) + EPS) * g


def _mod_kernel(c_ref, w_ref, b_ref, o_ref):
    c = c_ref[...]
    s = c * jax.nn.sigmoid(c)
    o_ref[...] = _dot(s.astype(BF16), w_ref[...].astype(BF16)) + b_ref[...]


def _modulation(c, w_ada, b_ada):
    bsz = c.shape[0]
    rows = -(-bsz // V7X_SUBLANES) * V7X_SUBLANES
    cp = jnp.zeros((rows, D_MODEL), F32).at[:bsz].set(c)
    n_out = w_ada.shape[1]
    tile = D_MODEL
    out = pl.pallas_call(
        _mod_kernel,
        grid=(n_out // tile,),
        in_specs=[pl.BlockSpec((rows, D_MODEL), lambda j: (0, 0)),
                  pl.BlockSpec((D_MODEL, tile), lambda j: (0, j)),
                  pl.BlockSpec((1, tile), lambda j: (0, j))],
        out_specs=pl.BlockSpec((rows, tile), lambda j: (0, j)),
        out_shape=jax.ShapeDtypeStruct((rows, n_out), F32),
        compiler_params=_params(("parallel",), D_MODEL * tile * 4),
        name="adaln_modulation",
    )(cp, w_ada, b_ada.reshape(1, n_out))
    return out[:bsz].reshape(bsz, 1, n_out)


def _inproj_kernel(x_ref, mod_ref, g_ref, w_ref, oa_ref, og_ref, or_ref):
    h = _rms(x_ref[...], g_ref[...])
    h = h * (1.0 + mod_ref[:, D_MODEL:2 * D_MODEL]) + mod_ref[:, 0:D_MODEL]
    p = _dot(h.astype(BF16), w_ref[...])
    oa_ref[...] = p[:, :ATTN_COLS]
    og_ref[...] = p[:, ATTN_COLS:ATTN_COLS + GLA_COLS]
    or_ref[...] = p[:, ATTN_COLS + GLA_COLS:]


def _in_projection(x, mod, norm_g, w_in_bf16):
    bsz, seq, _ = x.shape
    tb = TOK_TILE
    tok = lambda cols: pl.BlockSpec((None, tb, cols), lambda b, i: (b, i, 0))
    return pl.pallas_call(
        _inproj_kernel,
        grid=(bsz, seq // tb),
        in_specs=[tok(D_MODEL),
                  pl.BlockSpec((None, 1, 6 * D_MODEL), lambda b, i: (b, 0, 0)),
                  pl.BlockSpec((1, D_MODEL), lambda b, i: (0, 0)),
                  pl.BlockSpec((D_MODEL, D_IN), lambda b, i: (0, 0))],
        out_specs=[tok(ATTN_COLS), tok(GLA_COLS), tok(GATE_COLS)],
        out_shape=[jax.ShapeDtypeStruct((bsz, seq, ATTN_COLS), F32),
                   jax.ShapeDtypeStruct((bsz, seq, GLA_COLS), F32),
                   jax.ShapeDtypeStruct((bsz, seq, GATE_COLS), F32)],
        compiler_params=_params(("parallel", "parallel"),
                                tb * D_MODEL * 4 + D_MODEL * D_IN * 2 + 2 * tb * D_IN * 4),
        name="norm1_in_projection",
    )(x, mod, norm_g.reshape(1, D_MODEL), w_in_bf16)


def _t5_buckets(rel):
    nb = N_BUCKETS // 2
    ret = (rel > 0).astype(np.int32) * nb
    n = np.abs(rel)
    max_exact = nb // 2
    large = max_exact + (np.log(np.maximum(n, 1) / max_exact) / math.log(MAX_DISTANCE / max_exact)
                         * (nb - max_exact)).astype(np.int32)
    large = np.minimum(large, nb - 1)
    return (ret + np.where(n < max_exact, n, large)).astype(np.int32)


def _band_bias(rel_bias):
    span = 4 * BLOCK
    offsets = np.arange(span) - (2 * BLOCK - 1)
    per_offset = rel_bias[_t5_buckets(offsets)].astype(F32).T
    shifted = jnp.roll(per_offset, -(BLOCK - 1), axis=1)
    skew = jnp.tile(shifted, (1, BLOCK))[:, :BLOCK * (span - 1)].reshape(-1, BLOCK, span - 1)
    rel = np.arange(3 * BLOCK)[None, :] - BLOCK - np.arange(BLOCK)[:, None]
    band = jnp.asarray(np.abs(rel) <= WINDOW)
    return jnp.where(band[None], skew[:, :, :3 * BLOCK], NEG_BIG)


def _attn_kernel(sink_ref, q_ref, kvp_ref, kvc_ref, kvn_ref, bias_ref, qg_ref, kg_ref, beta_ref, o_ref):
    n = pl.program_id(1)
    last = pl.num_programs(1) - 1
    q = q_ref[...]
    kv = jnp.concatenate([kvp_ref[...], kvc_ref[...], kvn_ref[...]], axis=0)
    j = lax.broadcasted_iota(jnp.int32, (BLOCK, 3 * BLOCK), 1)
    valid = jnp.logical_and(jnp.logical_or(j >= BLOCK, n > 0), jnp.logical_or(j < 2 * BLOCK, n < last))
    scale = ATTN_HEAD_DIM ** -0.5
    outs = []
    for hk in range(ATTN_KV_HEADS):
        kh = _rms(kv[:, hk * ATTN_HEAD_DIM:(hk + 1) * ATTN_HEAD_DIM], kg_ref[...]).astype(BF16)
        vh = kv[:, ATTN_KV + hk * ATTN_HEAD_DIM:ATTN_KV + (hk + 1) * ATTN_HEAD_DIM].astype(BF16)
        for g in range(ATTN_GROUP):
            h = hk * ATTN_GROUP + g
            qh = _rms(q[:, h * ATTN_HEAD_DIM:(h + 1) * ATTN_HEAD_DIM], qg_ref[...]).astype(BF16)
            s = _nt_dot(qh, kh) * scale
            s = jnp.where(valid, s + bias_ref[h], NEG_BIG)
            sink = sink_ref[h]
            m = jnp.maximum(jnp.max(s, axis=-1, keepdims=True), sink)
            p = jnp.exp(s - m)
            den = jnp.sum(p, axis=-1, keepdims=True) + jnp.exp(sink - m)
            o = _dot(p.astype(BF16), vh) / den
            outs.append(_rms(o, beta_ref[:, h * ATTN_HEAD_DIM:(h + 1) * ATTN_HEAD_DIM]))
    o_ref[...] = jnp.concatenate(outs, axis=-1)


def _windowed_attention(qkv, bias, sink, q_norm_g, k_norm_g, beta_attn):
    bsz, seq, _ = qkv.shape
    nb = seq // BLOCK
    kv_cols = 2 * ATTN_KV
    kv_blk = ATTN_Q // kv_cols
    kv_spec = lambda f: pl.BlockSpec((None, BLOCK, kv_cols), lambda b, n: (b, f(n), kv_blk))
    const = lambda shape: pl.BlockSpec(shape, lambda b, n: tuple(0 for _ in shape))
    return pl.pallas_call(
        _attn_kernel,
        grid=(bsz, nb),
        in_specs=[pl.BlockSpec(memory_space=pltpu.SMEM),
                  pl.BlockSpec((None, BLOCK, ATTN_Q), lambda b, n: (b, n, 0)),
                  kv_spec(lambda n: jnp.maximum(n - 1, 0)),
                  kv_spec(lambda n: n),
                  kv_spec(lambda n: jnp.minimum(n + 1, nb - 1)),
                  const((ATTN_HEADS, BLOCK, 3 * BLOCK)),
                  const((1, ATTN_HEAD_DIM)), const((1, ATTN_HEAD_DIM)), const((1, ATTN_Q))],
        out_specs=pl.BlockSpec((None, BLOCK, ATTN_Q), lambda b, n: (b, n, 0)),
        out_shape=jax.ShapeDtypeStruct((bsz, seq, ATTN_Q), F32),
        compiler_params=_params(("parallel", "parallel"), ATTN_HEADS * BLOCK * 3 * BLOCK * 4 + 8 * BLOCK * ATTN_Q * 4),
        name="windowed_attention",
    )(sink, qkv, qkv, qkv, qkv, bias, q_norm_g.reshape(1, -1), k_norm_g.reshape(1, -1), beta_attn.reshape(1, -1))


def _split3(x):
    hi = x.astype(BF16)
    r = x - hi.astype(F32)
    mid = r.astype(BF16)
    lo = (r - mid.astype(F32)).astype(BF16)
    return hi, mid, lo


def _gla_direction(qk_ref, v_ref, gr, up_ref, gb_ref, tri_ref, o_ref, st_ref, b_ref, reverse):
    c = GLA_CHUNK
    nch = GLA_TILE // c
    z = _dot(gr.astype(BF16), up_ref[...]) + gb_ref[...]
    log_a = (jnp.minimum(z, 0.0) - jnp.log1p(jnp.exp(-jnp.abs(z)))) / GLA_GATE_NORM
    hi, mid, lo = _split3(log_a)
    tri = tri_ref[...]
    b_ref[...] = _dot(tri, hi) + _dot(tri, mid) + _dot(tri, lo)

    row = lax.broadcasted_iota(jnp.int32, (c, GLA_QK), 0)
    lane = lax.broadcasted_iota(jnp.int32, (c, GLA_QK), 1)
    key_in_head = lane % GLA_KEY_DIM
    causal = (key_in_head >= row) if reverse else (key_in_head <= row)
    qk_head = lane // GLA_KEY_DIM
    v_head = lax.broadcasted_iota(jnp.int32, (c, GLA_V), 1) // GLA_VAL_DIM
    st_rows = lax.broadcasted_iota(jnp.int32, (GLA_V, GLA_QK), 0) // GLA_VAL_DIM
    st_cols = lax.broadcasted_iota(jnp.int32, (GLA_V, GLA_QK), 1) // GLA_KEY_DIM
    same_head = st_rows == st_cols

    for ci in range(nch):
        blk = nch - 1 - ci if reverse else ci
        rows = slice(blk * c, (blk + 1) * c)
        b = b_ref[rows, :]
        b_last = b[0:1, :] if reverse else b[c - 1:c, :]
        q = qk_ref[rows, 0:GLA_QK]
        k = qk_ref[rows, GLA_QK:2 * GLA_QK]
        v = v_ref[rows, :]
        q_dec = (q * ((GLA_KEY_DIM ** -0.5) * jnp.exp(b))).astype(BF16)
        k_dec = k * jnp.exp(-b)
        k_end = (k * jnp.exp(b_last - b)).astype(BF16)
        decay = jnp.exp(b_last)
        k_heads = jnp.concatenate([jnp.where(qk_head == h, k_dec, 0.0) for h in range(GLA_HEADS)], axis=0)
        v_heads = jnp.concatenate([jnp.where(v_head == h, v, 0.0) for h in range(GLA_HEADS)], axis=0)
        a = jnp.where(causal, _nt_dot(q_dec, k_heads.astype(BF16)), 0.0)
        st = st_ref[...]
        o_ref[rows, :] = _dot(a.astype(BF16), v_heads.astype(BF16)) + _nt_dot(q_dec, st.astype(BF16))
        kv_t = lax.dot_general(v.astype(BF16), k_end, (((0,), (0,)), ((), ())), preferred_element_type=F32)
        st_ref[...] = st * decay + jnp.where(same_head, kv_t, 0.0)


def _gla_kernel(qkf_ref, vf_ref, grf_ref, qkb_ref, vb_ref, grb_ref, upf_ref, bf_ref, upb_ref, bb_ref,
                trif_ref, trib_ref, of_ref, ob_ref, stf_ref, stb_ref, cumf_ref, cumb_ref):
    @pl.when(pl.program_id(1) == 0)
    def _():
        stf_ref[...] = jnp.zeros_like(stf_ref)
        stb_ref[...] = jnp.zeros_like(stb_ref)

    _gla_direction(qkf_ref, vf_ref, grf_ref[:, 0:GLA_GATE_RANK], upf_ref, bf_ref, trif_ref,
                   of_ref, stf_ref, cumf_ref, False)
    _gla_direction(qkb_ref, vb_ref, grb_ref[:, GLA_GATE_RANK:2 * GLA_GATE_RANK], upb_ref, bb_ref, trib_ref,
                   ob_ref, stb_ref, cumb_ref, True)


def _gla(gla_in, gates, up_f, b_f, up_b, b_b):
    bsz, seq, _ = gla_in.shape
    tb = GLA_TILE
    nblk = seq // tb
    fwd = lambda cols, cb: pl.BlockSpec((None, tb, cols), lambda b, i: (b, i, cb))
    bwd = lambda cols, cb: pl.BlockSpec((None, tb, cols), lambda b, i: (b, nblk - 1 - i, cb))
    const = lambda shape: pl.BlockSpec(shape, lambda b, i: tuple(0 for _ in shape))
    chunk_of = np.arange(tb) // GLA_CHUNK
    same_chunk = chunk_of[:, None] == chunk_of[None, :]
    pos = np.arange(tb)
    tri_f = jnp.asarray(same_chunk & (pos[None, :] <= pos[:, None]), BF16)
    tri_b = jnp.asarray(same_chunk & (pos[None, :] >= pos[:, None]), BF16)
    state = pltpu.VMEM((GLA_V, GLA_QK), F32)
    cum = pltpu.VMEM((tb, GLA_QK), F32)
    return pl.pallas_call(
        _gla_kernel,
        grid=(bsz, nblk),
        in_specs=[fwd(2 * GLA_QK, 0), fwd(GLA_V, 1), fwd(GATE_COLS, 0),
                  bwd(2 * GLA_QK, 0), bwd(GLA_V, 1), bwd(GATE_COLS, 0),
                  const((GLA_GATE_RANK, GLA_QK)), const((1, GLA_QK)),
                  const((GLA_GATE_RANK, GLA_QK)), const((1, GLA_QK)),
                  const((tb, tb)), const((tb, tb))],
        out_specs=[pl.BlockSpec((None, tb, GLA_V), lambda b, i: (b, i, 0)),
                   pl.BlockSpec((None, tb, GLA_V), lambda b, i: (b, nblk - 1 - i, 0))],
        out_shape=[jax.ShapeDtypeStruct((bsz, seq, GLA_V), F32)] * 2,
        scratch_shapes=[state, state, cum, cum],
        compiler_params=_params(("parallel", "arbitrary"), 2 * tb * (2 * GLA_QK + 2 * GLA_V + V7X_LANES) * 4),
        name="gla_bidirectional",
    )(gla_in, gla_in, gates, gla_in, gla_in, gates,
      up_f.astype(BF16), b_f.reshape(1, -1), up_b.astype(BF16), b_b.reshape(1, -1), tri_f, tri_b)


def _outproj_kernel(att_ref, of_ref, ob_ref, og_ref, x_ref, mod_ref, beta_ref, w_ref, g2_ref, x1_ref, h2_ref):
    o = of_ref[...] + ob_ref[...]
    gate = og_ref[...]
    gate = gate * jax.nn.sigmoid(gate)
    parts = [att_ref[...].astype(BF16)]
    for h in range(GLA_HEADS):
        sl = slice(h * GLA_VAL_DIM, (h + 1) * GLA_VAL_DIM)
        parts.append((_rms(o[:, sl], beta_ref[:, sl]) * gate[:, sl]).astype(BF16))
    mix = _dot(jnp.concatenate(parts, axis=-1), w_ref[...])
    d = D_MODEL
    x1 = x_ref[...] + mod_ref[:, 2 * d:3 * d] * mix
    x1_ref[...] = x1
    h2 = _rms(x1, g2_ref[...]) * (1.0 + mod_ref[:, 4 * d:5 * d]) + mod_ref[:, 3 * d:4 * d]
    h2_ref[...] = h2.astype(BF16)


def _out_projection(att, o_f, o_b, gla_in, x, mod, beta_gla, w_out_bf16, norm2_g):
    bsz, seq, _ = x.shape
    tb = TOK_TILE
    tok = lambda cols, cb=0: pl.BlockSpec((None, tb, cols), lambda b, i: (b, i, cb))
    const = lambda shape: pl.BlockSpec(shape, lambda b, i: tuple(0 for _ in shape))
    return pl.pallas_call(
        _outproj_kernel,
        grid=(bsz, seq // tb),
        in_specs=[tok(ATTN_Q), tok(GLA_V), tok(GLA_V), tok(GLA_V, (2 * GLA_QK + GLA_V) // GLA_V), tok(D_MODEL),
                  pl.BlockSpec((None, 1, 6 * D_MODEL), lambda b, i: (b, 0, 0)),
                  const((1, GLA_V)), const((D_MODEL, D_MODEL)), const((1, D_MODEL))],
        out_specs=[tok(D_MODEL), tok(D_MODEL)],
        out_shape=[jax.ShapeDtypeStruct((bsz, seq, D_MODEL), F32),
                   jax.ShapeDtypeStruct((bsz, seq, D_MODEL), BF16)],
        compiler_params=_params(("parallel", "parallel"), tb * (4 * GLA_V + 3 * D_MODEL) * 4 + D_MODEL * D_MODEL * 2),
        name="out_projection_norm2",
    )(att, o_f, o_b, gla_in, x, mod, beta_gla.reshape(1, -1), w_out_bf16, norm2_g.reshape(1, -1))


def _sort_network(n):
    def merge(lo, hi, r):
        step = 2 * r
        if step < hi - lo:
            yield from merge(lo, hi, step)
            yield from merge(lo + r, hi, step)
            yield from ((i, i + r) for i in range(lo + r, hi - r, step))
        else:
            yield (lo, lo + r)

    def sort(lo, hi):
        if hi > lo:
            mid = lo + (hi - lo) // 2
            yield from sort(lo, mid)
            yield from sort(mid + 1, hi)
            yield from merge(lo, hi, 1)

    return tuple(sort(0, n - 1))


def _top_values(x, out_ref, head):
    n = N_KEYS // V7X_SUBLANES
    assert n == PEER_TOPK
    tiles =[x[i * V7X_SUBLANES:(i + 1) * V7X_SUBLANES, :] for i in range(n)]
    srt = list(tiles)
    for i, j in _sort_network(n):
        srt[i], srt[j] = jnp.maximum(srt[i], srt[j]), jnp.minimum(srt[i], srt[j])
    shift = V7X_SUBLANES // 2
    while shift >= 1:
        other = [pltpu.roll(t, shift, 0) for t in srt]
        srt = [jnp.maximum(srt[i], other[n - 1 - i]) for i in range(n)]
        stride = n // 2
        while stride >= 1:
            for i in range(n):
                if i & stride == 0:
                    srt[i], srt[i + stride] = (jnp.maximum(srt[i], srt[i + stride]),
                                               jnp.minimum(srt[i], srt[i + stride]))
            stride //= 2
        shift //= 2
    for r in range(PEER_TOPK):
        out_ref[r, pl.ds(head, 1), :] = srt[r][0:1, :]
    below = [jnp.where(t < srt[PEER_TOPK - 1], t, -jnp.inf) for t in tiles]
    out_ref[PEER_TOPK, pl.ds(head, 1), :] = jnp.max(functools.reduce(jnp.maximum, below), axis=0, keepdims=True)


def _peer_route_kernel(h2_ref, wq_ref, keys_ref, h2t_ref, s2_ref, e2_ref, thr_ref, c_ref, sc_ref, a_ref, b_ref):
    h2 = h2_ref[...]
    h2t_ref[...] = h2.astype(F32).T.astype(BF16)
    q = _dot(h2, wq_ref[...]).astype(BF16)
    for h in range(PEER_HEADS):
        for p in range(2):
            blk = h * 2 + p
            s = _nt_dot(keys_ref[h, p], q[:, blk * PEER_HALF:(blk + 1) * PEER_HALF])
            sc_ref[blk] = s
            _top_values(s, a_ref if p == 0 else b_ref, h)
    cand = [a_ref[i] + b_ref[j] for (i, j) in CAND_PAIRS]
    vals = []
    for r in range(N_RANKS):
        m = functools.reduce(jnp.maximum, cand)
        vals.append(m)
        cand = [jnp.where(x == m, -jnp.inf, x) for x in cand]
    top = vals[0]
    thr = 0.5 * (vals[PEER_TOPK - 1] + vals[PEER_TOPK])
    z = functools.reduce(jnp.add, [jnp.exp(v - top) for v in vals[:PEER_TOPK]])
    inv_z = 1.0 / z
    a0 = a_ref[0]
    b0 = b_ref[0]
    for h in range(PEER_HEADS):
        s1 = sc_ref[2 * h]
        s2 = sc_ref[2 * h + 1]
        thr_ref[h] = thr[h:h + 1, :] - s1
        c_ref[h] = jnp.exp(s1 - a0[h:h + 1, :]) * inv_z[h:h + 1, :]
        s2_ref[h] = s2
        e2_ref[h] = jnp.exp(s2 - b0[h:h + 1, :])


def _peer_route(h2, wq_bf16, keys_bf16):
    bsz, seq, _ = h2.shape
    tb = TOK_TILE
    per_head = lambda: pl.BlockSpec((None, PEER_HEADS, N_KEYS, tb), lambda b, i: (b, 0, 0, i))
    head_shape = jax.ShapeDtypeStruct((bsz, PEER_HEADS, N_KEYS, seq), F32)
    qcols = PEER_HEADS * PEER_KEY_DIM
    return pl.pallas_call(
        _peer_route_kernel,
        grid=(bsz, seq // tb),
        in_specs=[pl.BlockSpec((None, tb, D_MODEL), lambda b, i: (b, i, 0)),
                  pl.BlockSpec((D_MODEL, qcols), lambda b, i: (0, 0)),
                  pl.BlockSpec((PEER_HEADS, 2, N_KEYS, PEER_HALF), lambda b, i: (0, 0, 0, 0))],
        out_specs=[pl.BlockSpec((None, None, D_MODEL, tb), lambda b, i: (b, i, 0, 0)),
                   per_head(), per_head(), per_head(), per_head()],
        out_shape=[jax.ShapeDtypeStruct((bsz, seq // tb, D_MODEL, tb), BF16),
                   head_shape, head_shape, head_shape, head_shape],
        scratch_shapes=[pltpu.VMEM((2 * PEER_HEADS, N_KEYS, tb), F32),
                        pltpu.VMEM((N_RANKS, PEER_HEADS, tb), F32),
                        pltpu.VMEM((N_RANKS, PEER_HEADS, tb), F32)],
        compiler_params=_params(("parallel", "parallel"),
                                D_MODEL * qcols * 2 + 5 * PEER_HEADS * N_KEYS * tb * 4 + tb * qcols * 4),
        name="peer_routing",
    )(h2, wq_bf16, keys_bf16)


def _gelu(x):
    return 0.5 * x * (1.0 + lax.erf(x * (2.0 ** -0.5)))


def _routing_block(s2_ref, e2_ref, thr_ref, c_ref, row0, act_ref, wg_ref, col0, k0):
    cols = slice(col0, col0 + V7X_LANES)
    local = slice(col0 % PEER_PASS_TOK, col0 % PEER_PASS_TOK + V7X_LANES)
    keys = pl.ds(k0, KEY_CHUNK)
    w = [jnp.zeros((KEY_CHUNK, V7X_LANES), F32) for _ in range(ROWS_PER_TILE)]
    for h in range(PEER_HEADS):
        s2 = s2_ref[h, keys, cols]
        e2 = e2_ref[h, keys, cols]
        for r in range(ROWS_PER_TILE):
            thr = thr_ref[h, row0 + r:row0 + r + 1, cols]
            coef = c_ref[h, row0 + r:row0 + r + 1, cols]
            w[r] = w[r] + jnp.where(s2 >= thr, e2 * coef, 0.0)
    for r in range(ROWS_PER_TILE):
        rows = pl.ds(pl.multiple_of(r * N_KEYS + k0, KEY_CHUNK), KEY_CHUNK)
        wg_ref[rows, local] = (w[r] * _gelu(act_ref[rows, local])).astype(BF16)


def _peer_dense_kernel(h2t_ref, u_ref, vt_ref, s2_ref, e2_ref, thr0_ref, c0_ref, thr1_ref, c1_ref, x1_ref, mod_ref,
                       y_ref, *scratch):
    g = pl.program_id(2)
    n_pass = PEER_TOK_TILE // PEER_PASS_TOK
    act_refs = [scratch[k * n_pass:(k + 1) * n_pass] for k in range(2)]
    wg_refs = [scratch[(2 + k) * n_pass:(3 + k) * n_pass] for k in range(2)]
    acc_refs = scratch[4 * n_pass:5 * n_pass]

    @pl.when(g == 0)
    def _():
        for ref in scratch:
            ref[...] = jnp.zeros_like(ref)

    for k in range(2):
        thr_ref, c_ref, row0 = (thr0_ref, c0_ref, ROWS_PER_TILE) if k == 0 else (thr1_ref, c1_ref, 0)
        tile = slice(k * EXPERT_TILE, (k + 1) * EXPERT_TILE)
        has_tile = (g > 0) if k == 0 else (g < pl.num_programs(2) - 1)
        for p in range(n_pass):
            def slab(c, carry, k=k, p=p, thr_ref=thr_ref, c_ref=c_ref, row0=row0, tile=tile, with_stage_b=True):
                a0 = pl.multiple_of(c * ACT_SLAB, ACT_SLAB)
                act_refs[k][p][pl.ds(a0, ACT_SLAB), :] = _dot(u_ref[pl.ds(k * EXPERT_TILE + a0, ACT_SLAB), :],
                                                             h2t_ref[p])
                rows = pl.ds(pl.multiple_of(c * OUT_SLAB, OUT_SLAB), OUT_SLAB)
                acc_refs[p][rows, :] += _dot(vt_ref[rows, tile], wg_refs[k][p][...])
                if with_stage_b:
                    k0 = pl.multiple_of(c * KEY_CHUNK, KEY_CHUNK)
                    for t in range(PEER_PASS_TOK // V7X_LANES):
                        _routing_block(s2_ref, e2_ref, thr_ref, c_ref, row0, act_refs[1 - k][p], wg_refs[1 - k][p],
                                       p * PEER_PASS_TOK + t * V7X_LANES, k0)
                return carry

            @pl.when(has_tile)
            def _(slab=slab):
                lax.fori_loop(0, CHUNKS_PER_TILE, slab, 0)

            @pl.when(jnp.logical_not(has_tile))
            def _(slab=slab):
                lax.fori_loop(0, CHUNKS_PER_TILE, functools.partial(slab, with_stage_b=False), 0)

    @pl.when(g == pl.num_programs(2) - 1)
    def _():
        gate = mod_ref[:, 5 * D_MODEL:6 * D_MODEL]
        for p in range(n_pass):
            rows = slice(p * PEER_PASS_TOK, (p + 1) * PEER_PASS_TOK)
            y_ref[rows, :] = x1_ref[rows, :] + gate * acc_refs[p][...].T


def _transposed_blocks(v):
    blk = 2 * EXPERT_TILE
    return jnp.transpose(v.reshape(v.shape[0] // blk, blk, v.shape[1]), (0, 2, 1))


def _peer_dense(h2t, u, vt, s2, e2, thr, coef, x1, mod):
    bsz, seq, _ = x1.shape
    tb = PEER_TOK_TILE
    n_pass = tb // PEER_PASS_TOK
    n_tiles = u.shape[0] // EXPERT_TILE
    n_pairs = n_tiles // 2
    once = pl.Buffered(1)
    per_head = lambda: pl.BlockSpec((None, PEER_HEADS, N_KEYS, tb), lambda b, i, g: (b, 0, 0, i), pipeline_mode=once)
    row_block = lambda f: pl.BlockSpec((None, PEER_HEADS, ROW_BLOCK, tb),
                                       lambda b, i, g: (b, 0, jnp.clip(f(g), 0, n_pairs - 1), i))
    pass_f32 = pltpu.VMEM((EXPERT_TILE, PEER_PASS_TOK), F32)
    pass_bf16 = pltpu.VMEM((EXPERT_TILE, PEER_PASS_TOK), BF16)
    return pl.pallas_call(
        _peer_dense_kernel,
        grid=(bsz, seq // tb, n_pairs + 1),
        in_specs=[pl.BlockSpec((None, n_pass, D_MODEL, PEER_PASS_TOK), lambda b, i, g: (b, i, 0, 0), pipeline_mode=once),
                  pl.BlockSpec((2 * EXPERT_TILE, D_MODEL), lambda b, i, g: (jnp.minimum(g, n_pairs - 1), 0)),
                  pl.BlockSpec((None, D_MODEL, 2 * EXPERT_TILE), lambda b, i, g: (jnp.maximum(g - 1, 0), 0, 0)),
                  per_head(), per_head(),
                  row_block(lambda g: g - 1), row_block(lambda g: g - 1),
                  row_block(lambda g: g), row_block(lambda g: g),
                  pl.BlockSpec((None, tb, D_MODEL), lambda b, i, g: (b, i, 0), pipeline_mode=once),
                  pl.BlockSpec((None, 1, 6 * D_MODEL), lambda b, i, g: (b, 0, 0))],
        out_specs=pl.BlockSpec((None, tb, D_MODEL), lambda b, i, g: (b, i, 0)),
        out_shape=jax.ShapeDtypeStruct((bsz, seq, D_MODEL), F32),
        scratch_shapes=([pass_f32] * (2 * n_pass) + [pass_bf16] * (2 * n_pass)
                        + [pltpu.VMEM((D_MODEL, PEER_PASS_TOK), F32)] * n_pass),
        compiler_params=_params(("parallel", "parallel", "arbitrary"),
                                2 * PEER_HEADS * N_KEYS * tb * 4 + 2 * 2 * EXPERT_TILE * D_MODEL * 4
                                + 3 * tb * D_MODEL * 4 + 4 * EXPERT_TILE * tb * 4),
        name="peer_dense_experts",
    )(h2t, u, vt, s2, e2, thr, coef, thr, coef, x1, mod)


def _layer(x, c, w_ada, b_ada, norm1_g, norm2_g, w_in, q_norm_g, k_norm_g, bias, sink,
           up_f, b_f, up_b, b_b, beta_attn, beta_gla, w_out, wq, keys, u, vt):
    mod = _modulation(c, w_ada, b_ada)
    qkv, gla_in, gates = _in_projection(x, mod, norm1_g, w_in)
    att = _windowed_attention(qkv, bias, sink, q_norm_g, k_norm_g, beta_attn)
    o_f, o_b = _gla(gla_in, gates, up_f, b_f, up_b, b_b)
    x1, h2 = _out_projection(att, o_f, o_b, gla_in, x, mod, beta_gla, w_out, norm2_g)
    h2t, s2, e2, thr, coef = _peer_route(h2, wq, keys)
    return _peer_dense(h2t, u, vt, s2, e2, thr, coef, x1, mod)


def kernel(x_prompt, x_sample, c_prompt, c_sample, w_ada, b_ada, norm1_g, norm2_g, w_in, q_norm_g, k_norm_g,
           rel_bias, sink, gk_up_fwd, gk_bias_fwd, gk_up_bwd, gk_bias_bwd, beta_attn, beta_gla, w_out,
           peer_query, peer_subkeys, peer_u, peer_v):
    depth = w_ada.shape[0]
    bias = _band_bias(rel_bias)
    y_prompt, y_sample = x_prompt, x_sample
    for l in range(depth):
        params = (w_ada[l], b_ada[l], norm1_g[l], norm2_g[l], w_in[l].astype(BF16), q_norm_g[l], k_norm_g[l],
                  bias, sink[l], gk_up_fwd[l], gk_bias_fwd[l], gk_up_bwd[l], gk_bias_bwd[l],
                  beta_attn[l], beta_gla[l], w_out[l].astype(BF16), peer_query[l].astype(BF16),
                  peer_subkeys[l].astype(BF16), peer_u[l], _transposed_blocks(peer_v[l]))
        y_prompt = _layer(y_prompt, c_prompt, *params)
        y_sample = _layer(y_sample, c_sample, *params)
    return (y_prompt, y_sample)
```

```python
import functools
import math

import numpy as np
import jax
import jax.numpy as jnp
from jax import lax
from jax.experimental import pallas as pl
from jax.experimental.pallas import tpu as pltpu

F32 = jnp.float32
BF16 = jnp.bfloat16

D_MODEL = 1024
ATTN_HEADS = 8
ATTN_KV_HEADS = 2
ATTN_HEAD_DIM = 64
ATTN_GROUP = ATTN_HEADS // ATTN_KV_HEADS
WINDOW = 128
BLOCK = 128
N_BUCKETS = 32
MAX_DISTANCE = 128
GLA_HEADS = 4
GLA_KEY_DIM = 64
GLA_VAL_DIM = 128
GLA_GATE_RANK = 16
GLA_GATE_NORM = 16.0
GLA_CHUNK = 64
PEER_HEADS = 8
N_KEYS = 128
PEER_KEY_DIM = 256
PEER_HALF = PEER_KEY_DIM // 2
PEER_TOPK = 16
EPS = 1e-6

ATTN_Q = ATTN_HEADS * ATTN_HEAD_DIM
ATTN_KV = ATTN_KV_HEADS * ATTN_HEAD_DIM
GLA_QK = GLA_HEADS * GLA_KEY_DIM
GLA_V = GLA_HEADS * GLA_VAL_DIM
ATTN_COLS = ATTN_Q + 2 * ATTN_KV
GLA_COLS = 2 * GLA_QK + 2 * GLA_V
GATE_COLS = 2 * GLA_GATE_RANK
D_IN = ATTN_COLS + GLA_COLS + GATE_COLS

V7X_LANES = 128
V7X_SUBLANES = 8
V7X_VMEM_LIMIT_CAP = 60 * 1024 * 1024

NEG_BIG = -1e30

TOK_TILE = 512
GLA_TILE = 512
PEER_TOK_TILE = 1024
PEER_PASS_TOK = TOK_TILE
EXPERT_TILE = 512
ROWS_PER_TILE = EXPERT_TILE // N_KEYS
ROW_BLOCK = 2 * ROWS_PER_TILE
KEY_CHUNK = 64
CHUNKS_PER_TILE = N_KEYS // KEY_CHUNK
ACT_SLAB = EXPERT_TILE // CHUNKS_PER_TILE
OUT_SLAB = D_MODEL // CHUNKS_PER_TILE
N_RANKS = PEER_TOPK + 1
CAND_PAIRS = tuple((i, j) for i in range(N_RANKS) for j in range(N_RANKS) if (i + 1) * (j + 1) <= N_RANKS)


def _vmem_limit(block_bytes):
    return int(min(2 * block_bytes + (16 << 20), V7X_VMEM_LIMIT_CAP))


def _params(semantics, block_bytes):
    return pltpu.CompilerParams(dimension_semantics=semantics, vmem_limit_bytes=_vmem_limit(block_bytes))


def _nt_dot(a, b):
    return lax.dot_general(a, b, (((1,), (1,)), ((), ())), preferred_element_type=F32)


def _dot(a, b):
    return jnp.dot(a, b, preferred_element_type=F32)


def _rms(x, g):
    return x * lax.rsqrt(jnp.mean(x * x, axis=-1, keepdims=True) + EPS) * g


def _mod_kernel(c_ref, w_ref, b_ref, o_ref):
    c = c_ref[...]
    s = c * jax.nn.sigmoid(c)
    o_ref[...] = _dot(s.astype(BF16), w_ref[...].astype(BF16)) + b_ref[...]


def _modulation(c, w_ada, b_ada):
    bsz = c.shape[0]
    rows = -(-bsz // V7X_SUBLANES) * V7X_SUBLANES
    cp = jnp.zeros((rows, D_MODEL), F32).at[:bsz].set(c)
    n_out = w_ada.shape[1]
    tile = D_MODEL
    out = pl.pallas_call(
        _mod_kernel,
        grid=(n_out // tile,),
        in_specs=[pl.BlockSpec((rows, D_MODEL), lambda j: (0, 0)),
                  pl.BlockSpec((D_MODEL, tile), lambda j: (0, j)),
                  pl.BlockSpec((1, tile), lambda j: (0, j))],
        out_specs=pl.BlockSpec((rows, tile), lambda j: (0, j)),
        out_shape=jax.ShapeDtypeStruct((rows, n_out), F32),
        compiler_params=_params(("parallel",), D_MODEL * tile * 4),
        name="adaln_modulation",
    )(cp, w_ada, b_ada.reshape(1, n_out))
    return out[:bsz].reshape(bsz, 1, n_out)


def _inproj_kernel(x_ref, mod_ref, g_ref, w_ref, oa_ref, og_ref, or_ref):
    h = _rms(x_ref[...], g_ref[...])
    h = h * (1.0 + mod_ref[:, D_MODEL:2 * D_MODEL]) + mod_ref[:, 0:D_MODEL]
    p = _dot(h.astype(BF16), w_ref[...])
    oa_ref[...] = p[:, :ATTN_COLS].astype(BF16)
    og_ref[...] = p[:, ATTN_COLS:ATTN_COLS + GLA_COLS].astype(BF16)
    or_ref[...] = p[:, ATTN_COLS + GLA_COLS:]


def _in_projection(x, mod, norm_g, w_in_bf16):
    bsz, seq, _ = x.shape
    tb = TOK_TILE
    tok = lambda cols: pl.BlockSpec((None, tb, cols), lambda b, i: (b, i, 0))
    return pl.pallas_call(
        _inproj_kernel,
        grid=(bsz, seq // tb),
        in_specs=[tok(D_MODEL),
                  pl.BlockSpec((None, 1, 6 * D_MODEL), lambda b, i: (b, 0, 0)),
                  pl.BlockSpec((1, D_MODEL), lambda b, i: (0, 0)),
                  pl.BlockSpec((D_MODEL, D_IN), lambda b, i: (0, 0))],
        out_specs=[tok(ATTN_COLS), tok(GLA_COLS), tok(GATE_COLS)],
        out_shape=[jax.ShapeDtypeStruct((bsz, seq, ATTN_COLS), BF16),
                   jax.ShapeDtypeStruct((bsz, seq, GLA_COLS), BF16),
                   jax.ShapeDtypeStruct((bsz, seq, GATE_COLS), F32)],
        compiler_params=_params(("parallel", "parallel"),
                                tb * D_MODEL * 4 + D_MODEL * D_IN * 2 + 2 * tb * D_IN * 4),
        name="norm1_in_projection",
    )(x, mod, norm_g.reshape(1, D_MODEL), w_in_bf16)


def _t5_buckets(rel):
    nb = N_BUCKETS // 2
    ret = (rel > 0).astype(np.int32) * nb
    n = np.abs(rel)
    max_exact = nb // 2
    large = max_exact + (np.log(np.maximum(n, 1) / max_exact) / math.log(MAX_DISTANCE / max_exact)
                         * (nb - max_exact)).astype(np.int32)
    large = np.minimum(large, nb - 1)
    return (ret + np.where(n < max_exact, n, large)).astype(np.int32)


def _band_bias(rel_bias):
    span = 4 * BLOCK
    offsets = np.arange(span) - (2 * BLOCK - 1)
    per_offset = rel_bias[_t5_buckets(offsets)].astype(F32).T
    shifted = jnp.roll(per_offset, -(BLOCK - 1), axis=1)
    skew = jnp.tile(shifted, (1, BLOCK))[:, :BLOCK * (span - 1)].reshape(-1, BLOCK, span - 1)
    rel = np.arange(3 * BLOCK)[None, :] - BLOCK - np.arange(BLOCK)[:, None]
    band = jnp.asarray(np.abs(rel) <= WINDOW)
    return jnp.where(band[None], skew[:, :, :3 * BLOCK], NEG_BIG)


def _attn_kernel(sink_ref, q_ref, kvp_ref, kvc_ref, kvn_ref, bias_ref, qg_ref, kg_ref, beta_ref, o_ref):
    n = pl.program_id(1)
    last = pl.num_programs(1) - 1
    q = q_ref[...].astype(F32)
    kv = jnp.concatenate([kvp_ref[...], kvc_ref[...], kvn_ref[...]], axis=0).astype(F32)
    j = lax.broadcasted_iota(jnp.int32, (BLOCK, 3 * BLOCK), 1)
    valid = jnp.logical_and(jnp.logical_or(j >= BLOCK, n > 0), jnp.logical_or(j < 2 * BLOCK, n < last))
    scale = ATTN_HEAD_DIM ** -0.5
    outs = []
    for hk in range(ATTN_KV_HEADS):
        kh = _rms(kv[:, hk * ATTN_HEAD_DIM:(hk + 1) * ATTN_HEAD_DIM], kg_ref[...]).astype(BF16)
        vh = kv[:, ATTN_KV + hk * ATTN_HEAD_DIM:ATTN_KV + (hk + 1) * ATTN_HEAD_DIM].astype(BF16)
        for g in range(ATTN_GROUP):
            h = hk * ATTN_GROUP + g
            qh = _rms(q[:, h * ATTN_HEAD_DIM:(h + 1) * ATTN_HEAD_DIM], qg_ref[...]).astype(BF16)
            s = _nt_dot(qh, kh) * scale
            s = jnp.where(valid, s + bias_ref[h], NEG_BIG)
            sink = sink_ref[h]
            m = jnp.maximum(jnp.max(s, axis=-1, keepdims=True), sink)
            p = jnp.exp(s - m)
            den = jnp.sum(p, axis=-1, keepdims=True) + jnp.exp(sink - m)
            o = _dot(p.astype(BF16), vh) / den
            outs.append(_rms(o, beta_ref[:, h * ATTN_HEAD_DIM:(h + 1) * ATTN_HEAD_DIM]))
    o_ref[...] = jnp.concatenate(outs, axis=-1).astype(BF16)


def _windowed_attention(qkv, bias, sink, q_norm_g, k_norm_g, beta_attn):
    bsz, seq, _ = qkv.shape
    nb = seq // BLOCK
    kv_cols = 2 * ATTN_KV
    kv_blk = ATTN_Q // kv_cols
    kv_spec = lambda f: pl.BlockSpec((None, BLOCK, kv_cols), lambda b, n: (b, f(n), kv_blk))
    const = lambda shape: pl.BlockSpec(shape, lambda b, n: tuple(0 for _ in shape))
    return pl.pallas_call(
        _attn_kernel,
        grid=(bsz, nb),
        in_specs=[pl.BlockSpec(memory_space=pltpu.SMEM),
                  pl.BlockSpec((None, BLOCK, ATTN_Q), lambda b, n: (b, n, 0)),
                  kv_spec(lambda n: jnp.maximum(n - 1, 0)),
                  kv_spec(lambda n: n),
                  kv_spec(lambda n: jnp.minimum(n + 1, nb - 1)),
                  const((ATTN_HEADS, BLOCK, 3 * BLOCK)),
                  const((1, ATTN_HEAD_DIM)), const((1, ATTN_HEAD_DIM)), const((1, ATTN_Q))],
        out_specs=pl.BlockSpec((None, BLOCK, ATTN_Q), lambda b, n: (b, n, 0)),
        out_shape=jax.ShapeDtypeStruct((bsz, seq, ATTN_Q), BF16),
        compiler_params=_params(("parallel", "parallel"), ATTN_HEADS * BLOCK * 3 * BLOCK * 4 + 8 * BLOCK * ATTN_Q * 4),
        name="windowed_attention",
    )(sink, qkv, qkv, qkv, qkv, bias, q_norm_g.reshape(1, -1), k_norm_g.reshape(1, -1), beta_attn.reshape(1, -1))


def _split3(x):
    hi = x.astype(BF16)
    r = x - hi.astype(F32)
    mid = r.astype(BF16)
    lo = (r - mid.astype(F32)).astype(BF16)
    return hi, mid, lo


def _gla_direction(qk_ref, v_ref, gr, up_ref, gb_ref, tri_ref, o_ref, st_ref, b_ref, reverse):
    c = GLA_CHUNK
    nch = GLA_TILE // c
    z = _dot(gr.astype(BF16), up_ref[...]) + gb_ref[...]
    log_a = (jnp.minimum(z, 0.0) - jnp.log1p(jnp.exp(-jnp.abs(z)))) / GLA_GATE_NORM
    hi, mid, lo = _split3(log_a)
    tri = tri_ref[...]
    b_ref[...] = _dot(tri, hi) + _dot(tri, mid) + _dot(tri, lo)

    row = lax.broadcasted_iota(jnp.int32, (c, GLA_QK), 0)
    lane = lax.broadcasted_iota(jnp.int32, (c, GLA_QK), 1)
    key_in_head = lane % GLA_KEY_DIM
    causal = (key_in_head >= row) if reverse else (key_in_head <= row)
    qk_head = lane // GLA_KEY_DIM
    v_head = lax.broadcasted_iota(jnp.int32, (c, GLA_V), 1) // GLA_VAL_DIM
    st_rows = lax.broadcasted_iota(jnp.int32, (GLA_V, GLA_QK), 0) // GLA_VAL_DIM
    st_cols = lax.broadcasted_iota(jnp.int32, (GLA_V, GLA_QK), 1) // GLA_KEY_DIM
    same_head = st_rows == st_cols

    for ci in range(nch):
        blk = nch - 1 - ci if reverse else ci
        rows = slice(blk * c, (blk + 1) * c)
        b = b_ref[rows, :]
        b_last = b[0:1, :] if reverse else b[c - 1:c, :]
        q = qk_ref[rows, 0:GLA_QK].astype(F32)
        k = qk_ref[rows, GLA_QK:2 * GLA_QK].astype(F32)
        v = v_ref[rows, :].astype(F32)
        q_dec = (q * ((GLA_KEY_DIM ** -0.5) * jnp.exp(b))).astype(BF16)
        k_dec = k * jnp.exp(-b)
        k_end = (k * jnp.exp(b_last - b)).astype(BF16)
        decay = jnp.exp(b_last)
        k_heads = jnp.concatenate([jnp.where(qk_head == h, k_dec, 0.0) for h in range(GLA_HEADS)], axis=0)
        v_heads = jnp.concatenate([jnp.where(v_head == h, v, 0.0) for h in range(GLA_HEADS)], axis=0)
        a = jnp.where(causal, _nt_dot(q_dec, k_heads.astype(BF16)), 0.0)
        st = st_ref[...]
        o_ref[rows, :] = (_dot(a.astype(BF16), v_heads.astype(BF16)) + _nt_dot(q_dec, st.astype(BF16))).astype(BF16)
        kv_t = lax.dot_general(v.astype(BF16), k_end, (((0,), (0,)), ((), ())), preferred_element_type=F32)
        st_ref[...] = st * decay + jnp.where(same_head, kv_t, 0.0)


def _gla_kernel(qkf_ref, vf_ref, grf_ref, qkb_ref, vb_ref, grb_ref, upf_ref, bf_ref, upb_ref, bb_ref,
                trif_ref, trib_ref, of_ref, ob_ref, stf_ref, stb_ref, cumf_ref, cumb_ref):
    @pl.when(pl.program_id(1) == 0)
    def _():
        stf_ref[...] = jnp.zeros_like(stf_ref)
        stb_ref[...] = jnp.zeros_like(stb_ref)

    _gla_direction(qkf_ref, vf_ref, grf_ref[:, 0:GLA_GATE_RANK], upf_ref, bf_ref, trif_ref,
                   of_ref, stf_ref, cumf_ref, False)
    _gla_direction(qkb_ref, vb_ref, grb_ref[:, GLA_GATE_RANK:2 * GLA_GATE_RANK], upb_ref, bb_ref, trib_ref,
                   ob_ref, stb_ref, cumb_ref, True)


def _gla(gla_in, gates, up_f, b_f, up_b, b_b):
    bsz, seq, _ = gla_in.shape
    tb = GLA_TILE
    nblk = seq // tb
    fwd = lambda cols, cb: pl.BlockSpec((None, tb, cols), lambda b, i: (b, i, cb))
    bwd = lambda cols, cb: pl.BlockSpec((None, tb, cols), lambda b, i: (b, nblk - 1 - i, cb))
    const = lambda shape: pl.BlockSpec(shape, lambda b, i: tuple(0 for _ in shape))
    chunk_of = np.arange(tb) // GLA_CHUNK
    same_chunk = chunk_of[:, None] == chunk_of[None, :]
    pos = np.arange(tb)
    tri_f = jnp.asarray(same_chunk & (pos[None, :] <= pos[:, None]), BF16)
    tri_b = jnp.asarray(same_chunk & (pos[None, :] >= pos[:, None]), BF16)
    state = pltpu.VMEM((GLA_V, GLA_QK), F32)
    cum = pltpu.VMEM((tb, GLA_QK), F32)
    return pl.pallas_call(
        _gla_kernel,
        grid=(bsz, nblk),
        in_specs=[fwd(2 * GLA_QK, 0), fwd(GLA_V, 1), fwd(GATE_COLS, 0),
                  bwd(2 * GLA_QK, 0), bwd(GLA_V, 1), bwd(GATE_COLS, 0),
                  const((GLA_GATE_RANK, GLA_QK)), const((1, GLA_QK)),
                  const((GLA_GATE_RANK, GLA_QK)), const((1, GLA_QK)),
                  const((tb, tb)), const((tb, tb))],
        out_specs=[pl.BlockSpec((None, tb, GLA_V), lambda b, i: (b, i, 0)),
                   pl.BlockSpec((None, tb, GLA_V), lambda b, i: (b, nblk - 1 - i, 0))],
        out_shape=[jax.ShapeDtypeStruct((bsz, seq, GLA_V), BF16)] * 2,
        scratch_shapes=[state, state, cum, cum],
        compiler_params=_params(("parallel", "arbitrary"), 2 * tb * (2 * GLA_QK + 2 * GLA_V + V7X_LANES) * 4),
        name="gla_bidirectional",
    )(gla_in, gla_in, gates, gla_in, gla_in, gates,
      up_f.astype(BF16), b_f.reshape(1, -1), up_b.astype(BF16), b_b.reshape(1, -1), tri_f, tri_b)


def _outproj_kernel(att_ref, of_ref, ob_ref, og_ref, x_ref, mod_ref, beta_ref, w_ref, g2_ref, x1_ref, h2_ref):
    o = of_ref[...].astype(F32) + ob_ref[...].astype(F32)
    gate = og_ref[...].astype(F32)
    gate = gate * jax.nn.sigmoid(gate)
    parts = [att_ref[...]]
    for h in range(GLA_HEADS):
        sl = slice(h * GLA_VAL_DIM, (h + 1) * GLA_VAL_DIM)
        parts.append((_rms(o[:, sl], beta_ref[:, sl]) * gate[:, sl]).astype(BF16))
    mix = _dot(jnp.concatenate(parts, axis=-1), w_ref[...])
    d = D_MODEL
    x1 = x_ref[...] + mod_ref[:, 2 * d:3 * d] * mix
    x1_ref[...] = x1
    h2 = _rms(x1, g2_ref[...]) * (1.0 + mod_ref[:, 4 * d:5 * d]) + mod_ref[:, 3 * d:4 * d]
    h2_ref[...] = h2.astype(BF16)


def _out_projection(att, o_f, o_b, gla_in, x, mod, beta_gla, w_out_bf16, norm2_g):
    bsz, seq, _ = x.shape
    tb = TOK_TILE
    tok = lambda cols, cb=0: pl.BlockSpec((None, tb, cols), lambda b, i: (b, i, cb))
    const = lambda shape: pl.BlockSpec(shape, lambda b, i: tuple(0 for _ in shape))
    return pl.pallas_call(
        _outproj_kernel,
        grid=(bsz, seq // tb),
        in_specs=[tok(ATTN_Q), tok(GLA_V), tok(GLA_V), tok(GLA_V, (2 * GLA_QK + GLA_V) // GLA_V), tok(D_MODEL),
                  pl.BlockSpec((None, 1, 6 * D_MODEL), lambda b, i: (b, 0, 0)),
                  const((1, GLA_V)), const((D_MODEL, D_MODEL)), const((1, D_MODEL))],
        out_specs=[tok(D_MODEL), tok(D_MODEL)],
        out_shape=[jax.ShapeDtypeStruct((bsz, seq, D_MODEL), F32),
                   jax.ShapeDtypeStruct((bsz, seq, D_MODEL), BF16)],
        compiler_params=_params(("parallel", "parallel"), tb * (4 * GLA_V + 3 * D_MODEL) * 4 + D_MODEL * D_MODEL * 2),
        name="out_projection_norm2",
    )(att, o_f, o_b, gla_in, x, mod, beta_gla.reshape(1, -1), w_out_bf16, norm2_g.reshape(1, -1))


def _sort_network(n):
    def merge(lo, hi, r):
        step = 2 * r
        if step < hi - lo:
            yield from merge(lo, hi, step)
            yield from merge(lo + r, hi, step)
            yield from ((i, i + r) for i in range(lo + r, hi - r, step))
        else:
            yield (lo, lo + r)

    def sort(lo, hi):
        if hi > lo:
            mid = lo + (hi - lo) // 2
            yield from sort(lo, mid)
            yield from sort(mid + 1, hi)
            yield from merge(lo, hi, 1)

    return tuple(sort(0, n - 1))


def _top_values(x, out_ref, head):
    n = N_KEYS // V7X_SUBLANES
    assert n == PEER_TOPK
    tiles =[x[i * V7X_SUBLANES:(i + 1) * V7X_SUBLANES, :] for i in range(n)]
    srt = list(tiles)
    for i, j in _sort_network(n):
        srt[i], srt[j] = jnp.maximum(srt[i], srt[j]), jnp.minimum(srt[i], srt[j])
    shift = V7X_SUBLANES // 2
    while shift >= 1:
        other = [pltpu.roll(t, shift, 0) for t in srt]
        srt = [jnp.maximum(srt[i], other[n - 1 - i]) for i in range(n)]
        stride = n // 2
        while stride >= 1:
            for i in range(n):
                if i & stride == 0:
                    srt[i], srt[i + stride] = (jnp.maximum(srt[i], srt[i + stride]),
                                               jnp.minimum(srt[i], srt[i + stride]))
            stride //= 2
        shift //= 2
    for r in range(PEER_TOPK):
        out_ref[r, pl.ds(head, 1), :] = srt[r][0:1, :]
    below = [jnp.where(t < srt[PEER_TOPK - 1], t, -jnp.inf) for t in tiles]
    out_ref[PEER_TOPK, pl.ds(head, 1), :] = jnp.max(functools.reduce(jnp.maximum, below), axis=0, keepdims=True)


def _peer_route_kernel(h2_ref, wq_ref, keys_ref, h2t_ref, s2_ref, e2_ref, thr_ref, c_ref, sc_ref, a_ref, b_ref):
    h2 = h2_ref[...]
    h2t_ref[...] = h2.astype(F32).T.astype(BF16)
    q = _dot(h2, wq_ref[...]).astype(BF16)
    for h in range(PEER_HEADS):
        for p in range(2):
            blk = h * 2 + p
            s = _nt_dot(keys_ref[h, p], q[:, blk * PEER_HALF:(blk + 1) * PEER_HALF])
            sc_ref[blk] = s
            _top_values(s, a_ref if p == 0 else b_ref, h)
    cand = [a_ref[i] + b_ref[j] for (i, j) in CAND_PAIRS]
    vals = []
    for r in range(N_RANKS):
        m = functools.reduce(jnp.maximum, cand)
        vals.append(m)
        cand = [jnp.where(x == m, -jnp.inf, x) for x in cand]
    top = vals[0]
    thr = 0.5 * (vals[PEER_TOPK - 1] + vals[PEER_TOPK])
    z = functools.reduce(jnp.add, [jnp.exp(v - top) for v in vals[:PEER_TOPK]])
    inv_z = 1.0 / z
    a0 = a_ref[0]
    b0 = b_ref[0]
    for h in range(PEER_HEADS):
        s1 = sc_ref[2 * h]
        s2 = sc_ref[2 * h + 1]
        thr_ref[h] = thr[h:h + 1, :] - s1
        c_ref[h] = jnp.exp(s1 - a0[h:h + 1, :]) * inv_z[h:h + 1, :]
        s2_ref[h] = s2
        e2_ref[h] = jnp.exp(s2 - b0[h:h + 1, :])


def _peer_route(h2, wq_bf16, keys_bf16):
    bsz, seq, _ = h2.shape
    tb = TOK_TILE
    per_head = lambda: pl.BlockSpec((None, PEER_HEADS, N_KEYS, tb), lambda b, i: (b, 0, 0, i))
    head_shape = jax.ShapeDtypeStruct((bsz, PEER_HEADS, N_KEYS, seq), F32)
    qcols = PEER_HEADS * PEER_KEY_DIM
    return pl.pallas_call(
        _peer_route_kernel,
        grid=(bsz, seq // tb),
        in_specs=[pl.BlockSpec((None, tb, D_MODEL), lambda b, i: (b, i, 0)),
                  pl.BlockSpec((D_MODEL, qcols), lambda b, i: (0, 0)),
                  pl.BlockSpec((PEER_HEADS, 2, N_KEYS, PEER_HALF), lambda b, i: (0, 0, 0, 0))],
        out_specs=[pl.BlockSpec((None, None, D_MODEL, tb), lambda b, i: (b, i, 0, 0)),
                   per_head(), per_head(), per_head(), per_head()],
        out_shape=[jax.ShapeDtypeStruct((bsz, seq // tb, D_MODEL, tb), BF16),
                   head_shape, head_shape, head_shape, head_shape],
        scratch_shapes=[pltpu.VMEM((2 * PEER_HEADS, N_KEYS, tb), F32),
                        pltpu.VMEM((N_RANKS, PEER_HEADS, tb), F32),
                        pltpu.VMEM((N_RANKS, PEER_HEADS, tb), F32)],
        compiler_params=_params(("parallel", "parallel"),
                                D_MODEL * qcols * 2 + 5 * PEER_HEADS * N_KEYS * tb * 4 + tb * qcols * 4),
        name="peer_routing",
    )(h2, wq_bf16, keys_bf16)


def _gelu(x):
    return 0.5 * x * (1.0 + lax.erf(x * (2.0 ** -0.5)))


def _routing_block(s2_ref, e2_ref, thr_ref, c_ref, row0, act_ref, wg_ref, col0, k0):
    cols = slice(col0, col0 + V7X_LANES)
    local = slice(col0 % PEER_PASS_TOK, col0 % PEER_PASS_TOK + V7X_LANES)
    keys = pl.ds(k0, KEY_CHUNK)
    w = [jnp.zeros((KEY_CHUNK, V7X_LANES), F32) for _ in range(ROWS_PER_TILE)]
    for h in range(PEER_HEADS):
        s2 = s2_ref[h, keys, cols]
        e2 = e2_ref[h, keys, cols]
        for r in range(ROWS_PER_TILE):
            thr = thr_ref[h, row0 + r:row0 + r + 1, cols]
            coef = c_ref[h, row0 + r:row0 + r + 1, cols]
            w[r] = w[r] + jnp.where(s2 >= thr, e2 * coef, 0.0)
    for r in range(ROWS_PER_TILE):
        rows = pl.ds(pl.multiple_of(r * N_KEYS + k0, KEY_CHUNK), KEY_CHUNK)
        wg_ref[rows, local] = (w[r] * _gelu(act_ref[rows, local])).astype(BF16)


def _peer_dense_kernel(h2t_ref, u_ref, vt_ref, s2_ref, e2_ref, thr0_ref, c0_ref, thr1_ref, c1_ref, x1_ref, mod_ref,
                       y_ref, *scratch):
    g = pl.program_id(2)
    n_pass = PEER_TOK_TILE // PEER_PASS_TOK
    act_refs = [scratch[k * n_pass:(k + 1) * n_pass] for k in range(2)]
    wg_refs = [scratch[(2 + k) * n_pass:(3 + k) * n_pass] for k in range(2)]
    acc_refs = scratch[4 * n_pass:5 * n_pass]

    @pl.when(g == 0)
    def _():
        for ref in scratch:
            ref[...] = jnp.zeros_like(ref)

    for k in range(2):
        thr_ref, c_ref, row0 = (thr0_ref, c0_ref, ROWS_PER_TILE) if k == 0 else (thr1_ref, c1_ref, 0)
        tile = slice(k * EXPERT_TILE, (k + 1) * EXPERT_TILE)
        has_tile = (g > 0) if k == 0 else (g < pl.num_programs(2) - 1)
        for p in range(n_pass):
            def slab(c, carry, k=k, p=p, thr_ref=thr_ref, c_ref=c_ref, row0=row0, tile=tile, with_stage_b=True):
                a0 = pl.multiple_of(c * ACT_SLAB, ACT_SLAB)
                act_refs[k][p][pl.ds(a0, ACT_SLAB), :] = _dot(u_ref[pl.ds(k * EXPERT_TILE + a0, ACT_SLAB), :],
                                                             h2t_ref[p])
                rows = pl.ds(pl.multiple_of(c * OUT_SLAB, OUT_SLAB), OUT_SLAB)
                acc_refs[p][rows, :] += _dot(vt_ref[rows, tile], wg_refs[k][p][...])
                if with_stage_b:
                    k0 = pl.multiple_of(c * KEY_CHUNK, KEY_CHUNK)
                    for t in range(PEER_PASS_TOK // V7X_LANES):
                        _routing_block(s2_ref, e2_ref, thr_ref, c_ref, row0, act_refs[1 - k][p], wg_refs[1 - k][p],
                                       p * PEER_PASS_TOK + t * V7X_LANES, k0)
                return carry

            @pl.when(has_tile)
            def _(slab=slab):
                lax.fori_loop(0, CHUNKS_PER_TILE, slab, 0)

            @pl.when(jnp.logical_not(has_tile))
            def _(slab=slab):
                lax.fori_loop(0, CHUNKS_PER_TILE, functools.partial(slab, with_stage_b=False), 0)

    @pl.when(g == pl.num_programs(2) - 1)
    def _():
        gate = mod_ref[:, 5 * D_MODEL:6 * D_MODEL]
        for p in range(n_pass):
            rows = slice(p * PEER_PASS_TOK, (p + 1) * PEER_PASS_TOK)
            y_ref[rows, :] = x1_ref[rows, :] + gate * acc_refs[p][...].T


def _transposed_blocks(v):
    blk = 2 * EXPERT_TILE
    return jnp.transpose(v.reshape(v.shape[0] // blk, blk, v.shape[1]), (0, 2, 1))


def _peer_dense(h2t, u, vt, s2, e2, thr, coef, x1, mod):
    bsz, seq, _ = x1.shape
    tb = PEER_TOK_TILE
    n_pass = tb // PEER_PASS_TOK
    n_tiles = u.shape[0] // EXPERT_TILE
    n_pairs = n_tiles // 2
    once = pl.Buffered(1)
    per_head = lambda: pl.BlockSpec((None, PEER_HEADS, N_KEYS, tb), lambda b, i, g: (b, 0, 0, i), pipeline_mode=once)
    row_block = lambda f: pl.BlockSpec((None, PEER_HEADS, ROW_BLOCK, tb),
                                       lambda b, i, g: (b, 0, jnp.clip(f(g), 0, n_pairs - 1), i))
    pass_f32 = pltpu.VMEM((EXPERT_TILE, PEER_PASS_TOK), F32)
    pass_bf16 = pltpu.VMEM((EXPERT_TILE, PEER_PASS_TOK), BF16)
    return pl.pallas_call(
        _peer_dense_kernel,
        grid=(bsz, seq // tb, n_pairs + 1),
        in_specs=[pl.BlockSpec((None, n_pass, D_MODEL, PEER_PASS_TOK), lambda b, i, g: (b, i, 0, 0), pipeline_mode=once),
                  pl.BlockSpec((2 * EXPERT_TILE, D_MODEL), lambda b, i, g: (jnp.minimum(g, n_pairs - 1), 0)),
                  pl.BlockSpec((None, D_MODEL, 2 * EXPERT_TILE), lambda b, i, g: (jnp.maximum(g - 1, 0), 0, 0)),
                  per_head(), per_head(),
                  row_block(lambda g: g - 1), row_block(lambda g: g - 1),
                  row_block(lambda g: g), row_block(lambda g: g),
                  pl.BlockSpec((None, tb, D_MODEL), lambda b, i, g: (b, i, 0), pipeline_mode=once),
                  pl.BlockSpec((None, 1, 6 * D_MODEL), lambda b, i, g: (b, 0, 0))],
        out_specs=pl.BlockSpec((None, tb, D_MODEL), lambda b, i, g: (b, i, 0)),
        out_shape=jax.ShapeDtypeStruct((bsz, seq, D_MODEL), F32),
        scratch_shapes=([pass_f32] * (2 * n_pass) + [pass_bf16] * (2 * n_pass)
                        + [pltpu.VMEM((D_MODEL, PEER_PASS_TOK), F32)] * n_pass),
        compiler_params=_params(("parallel", "parallel", "arbitrary"),
                                2 * PEER_HEADS * N_KEYS * tb * 4 + 2 * 2 * EXPERT_TILE * D_MODEL * 4
                                + 3 * tb * D_MODEL * 4 + 4 * EXPERT_TILE * tb * 4),
        name="peer_dense_experts",
    )(h2t, u, vt, s2, e2, thr, coef, thr, coef, x1, mod)


def _layer(x, c, w_ada, b_ada, norm1_g, norm2_g, w_in, q_norm_g, k_norm_g, bias, sink,
           up_f, b_f, up_b, b_b, beta_attn, beta_gla, w_out, wq, keys, u, vt):
    mod = _modulation(c, w_ada, b_ada)
    qkv, gla_in, gates = _in_projection(x, mod, norm1_g, w_in)
    att = _windowed_attention(qkv, bias, sink, q_norm_g, k_norm_g, beta_attn)
    o_f, o_b = _gla(gla_in, gates, up_f, b_f, up_b, b_b)
    x1, h2 = _out_projection(att, o_f, o_b, gla_in, x, mod, beta_gla, w_out, norm2_g)
    h2t, s2, e2, thr, coef = _peer_route(h2, wq, keys)
    return _peer_dense(h2t, u, vt, s2, e2, thr, coef, x1, mod)


def kernel(x_prompt, x_sample, c_prompt, c_sample, w_ada, b_ada, norm1_g, norm2_g, w_in, q_norm_g, k_norm_g,
           rel_bias, sink, gk_up_fwd, gk_bias_fwd, gk_up_bwd, gk_bias_bwd, beta_attn, beta_gla, w_out,
           peer_query, peer_subkeys, peer_u, peer_v):
    depth = w_ada.shape[0]
    bias = _band_bias(rel_bias)
    y_prompt, y_sample = x_prompt, x_sample
    for l in range(depth):
        params = (w_ada[l], b_ada[l], norm1_g[l], norm2_g[l], w_in[l].astype(BF16), q_norm_g[l], k_norm_g[l],
                  bias, sink[l], gk_up_fwd[l], gk_bias_fwd[l], gk_up_bwd[l], gk_bias_bwd[l],
                  beta_attn[l], beta_gla[l], w_out[l].astype(BF16), peer_query[l].astype(BF16),
                  peer_subkeys[l].astype(BF16), peer_u[l], _transposed_blocks(peer_v[l]))
        y_prompt = _layer(y_prompt, c_prompt, *params)
        y_sample = _layer(y_sample, c_sample, *params)
    return (y_prompt, y_sample)
```

```python
import functools
import math

import numpy as np
import jax
import jax.numpy as jnp
from jax import lax
from jax.experimental import pallas as pl
from jax.experimental.pallas import tpu as pltpu

F32 = jnp.float32
BF16 = jnp.bfloat16

D_MODEL = 1024
ATTN_HEADS = 8
ATTN_KV_HEADS = 2
ATTN_HEAD_DIM = 64
ATTN_GROUP = ATTN_HEADS // ATTN_KV_HEADS
WINDOW = 128
BLOCK = 128
N_BUCKETS = 32
MAX_DISTANCE = 128
GLA_HEADS = 4
GLA_KEY_DIM = 64
GLA_VAL_DIM = 128
GLA_GATE_RANK = 16
GLA_GATE_NORM = 16.0
GLA_CHUNK = 64
PEER_HEADS = 8
N_KEYS = 128
PEER_KEY_DIM = 256
PEER_HALF = PEER_KEY_DIM // 2
PEER_TOPK = 16
EPS = 1e-6

ATTN_Q = ATTN_HEADS * ATTN_HEAD_DIM
ATTN_KV = ATTN_KV_HEADS * ATTN_HEAD_DIM
GLA_QK = GLA_HEADS * GLA_KEY_DIM
GLA_V = GLA_HEADS * GLA_VAL_DIM
ATTN_COLS = ATTN_Q + 2 * ATTN_KV
GLA_COLS = 2 * GLA_QK + 2 * GLA_V
GATE_COLS = 2 * GLA_GATE_RANK
D_IN = ATTN_COLS + GLA_COLS + GATE_COLS

V7X_LANES = 128
V7X_SUBLANES = 8
V7X_VMEM_LIMIT_CAP = 60 * 1024 * 1024

NEG_BIG = -1e30

TOK_TILE = 512
GLA_TILE = 512
PEER_TOK_TILE = 1024
PEER_PASS_TOK = TOK_TILE
EXPERT_TILE = 512
ROWS_PER_TILE = EXPERT_TILE // N_KEYS
ROW_BLOCK = 2 * ROWS_PER_TILE
KEY_CHUNK = 64
KEY_BLOCK = 32
CHUNKS_PER_TILE = N_KEYS // KEY_CHUNK
ACT_SLAB = EXPERT_TILE // CHUNKS_PER_TILE
OUT_SLAB = D_MODEL // CHUNKS_PER_TILE
N_RANKS = PEER_TOPK + 1
CAND_PAIRS = tuple((i, j) for i in range(N_RANKS) for j in range(N_RANKS) if (i + 1) * (j + 1) <= N_RANKS)


def _vmem_limit(block_bytes):
    return int(min(2 * block_bytes + (16 << 20), V7X_VMEM_LIMIT_CAP))


def _params(semantics, block_bytes):
    return pltpu.CompilerParams(dimension_semantics=semantics, vmem_limit_bytes=_vmem_limit(block_bytes))


def _nt_dot(a, b):
    return lax.dot_general(a, b, (((1,), (1,)), ((), ())), preferred_element_type=F32)


def _dot(a, b):
    return jnp.dot(a, b, preferred_element_type=F32)


def _rms(x, g):
    return x * lax.rsqrt(jnp.mean(x * x, axis=-1, keepdims=True) + EPS) * g


def _mod_kernel(c_ref, w_ref, b_ref, o_ref):
    c = c_ref[...]
    s = c * jax.nn.sigmoid(c)
    o_ref[...] = _dot(s.astype(BF16), w_ref[...].astype(BF16)) + b_ref[...]


def _modulation(c, w_ada, b_ada):
    bsz = c.shape[0]
    rows = -(-bsz // V7X_SUBLANES) * V7X_SUBLANES
    cp = jnp.zeros((rows, D_MODEL), F32).at[:bsz].set(c)
    n_out = w_ada.shape[1]
    tile = D_MODEL
    out = pl.pallas_call(
        _mod_kernel,
        grid=(n_out // tile,),
        in_specs=[pl.BlockSpec((rows, D_MODEL), lambda j: (0, 0)),
                  pl.BlockSpec((D_MODEL, tile), lambda j: (0, j)),
                  pl.BlockSpec((1, tile), lambda j: (0, j))],
        out_specs=pl.BlockSpec((rows, tile), lambda j: (0, j)),
        out_shape=jax.ShapeDtypeStruct((rows, n_out), F32),
        compiler_params=_params(("parallel",), D_MODEL * tile * 4),
        name="adaln_modulation",
    )(cp, w_ada, b_ada.reshape(1, n_out))
    return out[:bsz].reshape(bsz, 1, n_out)


def _inproj_kernel(x_ref, mod_ref, g_ref, w_ref, oa_ref, og_ref, or_ref):
    h = _rms(x_ref[...], g_ref[...])
    h = h * (1.0 + mod_ref[:, D_MODEL:2 * D_MODEL]) + mod_ref[:, 0:D_MODEL]
    p = _dot(h.astype(BF16), w_ref[...])
    oa_ref[...] = p[:, :ATTN_COLS]
    og_ref[...] = p[:, ATTN_COLS:ATTN_COLS + GLA_COLS]
    or_ref[...] = p[:, ATTN_COLS + GLA_COLS:]


def _in_projection(x, mod, norm_g, w_in_bf16):
    bsz, seq, _ = x.shape
    tb = TOK_TILE
    tok = lambda cols: pl.BlockSpec((None, tb, cols), lambda b, i: (b, i, 0))
    return pl.pallas_call(
        _inproj_kernel,
        grid=(bsz, seq // tb),
        in_specs=[tok(D_MODEL),
                  pl.BlockSpec((None, 1, 6 * D_MODEL), lambda b, i: (b, 0, 0)),
                  pl.BlockSpec((1, D_MODEL), lambda b, i: (0, 0)),
                  pl.BlockSpec((D_MODEL, D_IN), lambda b, i: (0, 0))],
        out_specs=[tok(ATTN_COLS), tok(GLA_COLS), tok(GATE_COLS)],
        out_shape=[jax.ShapeDtypeStruct((bsz, seq, ATTN_COLS), F32),
                   jax.ShapeDtypeStruct((bsz, seq, GLA_COLS), F32),
                   jax.ShapeDtypeStruct((bsz, seq, GATE_COLS), F32)],
        compiler_params=_params(("parallel", "parallel"),
                                tb * D_MODEL * 4 + D_MODEL * D_IN * 2 + 2 * tb * D_IN * 4),
        name="norm1_in_projection",
    )(x, mod, norm_g.reshape(1, D_MODEL), w_in_bf16)


def _t5_buckets(rel):
    nb = N_BUCKETS // 2
    ret = (rel > 0).astype(np.int32) * nb
    n = np.abs(rel)
    max_exact = nb // 2
    large = max_exact + (np.log(np.maximum(n, 1) / max_exact) / math.log(MAX_DISTANCE / max_exact)
                         * (nb - max_exact)).astype(np.int32)
    large = np.minimum(large, nb - 1)
    return (ret + np.where(n < max_exact, n, large)).astype(np.int32)


def _band_bias(rel_bias):
    span = 4 * BLOCK
    offsets = np.arange(span) - (2 * BLOCK - 1)
    per_offset = rel_bias[_t5_buckets(offsets)].astype(F32).T
    shifted = jnp.roll(per_offset, -(BLOCK - 1), axis=1)
    skew = jnp.tile(shifted, (1, BLOCK))[:, :BLOCK * (span - 1)].reshape(-1, BLOCK, span - 1)
    rel = np.arange(3 * BLOCK)[None, :] - BLOCK - np.arange(BLOCK)[:, None]
    band = jnp.asarray(np.abs(rel) <= WINDOW)
    return jnp.where(band[None], skew[:, :, :3 * BLOCK], NEG_BIG)


def _attn_kernel(sink_ref, q_ref, kvp_ref, kvc_ref, kvn_ref, bias_ref, qg_ref, kg_ref, beta_ref, o_ref):
    n = pl.program_id(1)
    last = pl.num_programs(1) - 1
    q = q_ref[...]
    kv = jnp.concatenate([kvp_ref[...], kvc_ref[...], kvn_ref[...]], axis=0)
    j = lax.broadcasted_iota(jnp.int32, (BLOCK, 3 * BLOCK), 1)
    valid = jnp.logical_and(jnp.logical_or(j >= BLOCK, n > 0), jnp.logical_or(j < 2 * BLOCK, n < last))
    scale = ATTN_HEAD_DIM ** -0.5
    outs = []
    for hk in range(ATTN_KV_HEADS):
        kh = _rms(kv[:, hk * ATTN_HEAD_DIM:(hk + 1) * ATTN_HEAD_DIM], kg_ref[...]).astype(BF16)
        vh = kv[:, ATTN_KV + hk * ATTN_HEAD_DIM:ATTN_KV + (hk + 1) * ATTN_HEAD_DIM].astype(BF16)
        for g in range(ATTN_GROUP):
            h = hk * ATTN_GROUP + g
            qh = _rms(q[:, h * ATTN_HEAD_DIM:(h + 1) * ATTN_HEAD_DIM], qg_ref[...]).astype(BF16)
            s = _nt_dot(qh, kh) * scale
            s = jnp.where(valid, s + bias_ref[h], NEG_BIG)
            sink = sink_ref[h]
            m = jnp.maximum(jnp.max(s, axis=-1, keepdims=True), sink)
            p = jnp.exp(s - m)
            den = jnp.sum(p, axis=-1, keepdims=True) + jnp.exp(sink - m)
            o = _dot(p.astype(BF16), vh) / den
            outs.append(_rms(o, beta_ref[:, h * ATTN_HEAD_DIM:(h + 1) * ATTN_HEAD_DIM]))
    o_ref[...] = jnp.concatenate(outs, axis=-1)


def _windowed_attention(qkv, bias, sink, q_norm_g, k_norm_g, beta_attn):
    bsz, seq, _ = qkv.shape
    nb = seq // BLOCK
    kv_cols = 2 * ATTN_KV
    kv_blk = ATTN_Q // kv_cols
    kv_spec = lambda f: pl.BlockSpec((None, BLOCK, kv_cols), lambda b, n: (b, f(n), kv_blk))
    const = lambda shape: pl.BlockSpec(shape, lambda b, n: tuple(0 for _ in shape))
    return pl.pallas_call(
        _attn_kernel,
        grid=(bsz, nb),
        in_specs=[pl.BlockSpec(memory_space=pltpu.SMEM),
                  pl.BlockSpec((None, BLOCK, ATTN_Q), lambda b, n: (b, n, 0)),
                  kv_spec(lambda n: jnp.maximum(n - 1, 0)),
                  kv_spec(lambda n: n),
                  kv_spec(lambda n: jnp.minimum(n + 1, nb - 1)),
                  const((ATTN_HEADS, BLOCK, 3 * BLOCK)),
                  const((1, ATTN_HEAD_DIM)), const((1, ATTN_HEAD_DIM)), const((1, ATTN_Q))],
        out_specs=pl.BlockSpec((None, BLOCK, ATTN_Q), lambda b, n: (b, n, 0)),
        out_shape=jax.ShapeDtypeStruct((bsz, seq, ATTN_Q), F32),
        compiler_params=_params(("parallel", "parallel"), ATTN_HEADS * BLOCK * 3 * BLOCK * 4 + 8 * BLOCK * ATTN_Q * 4),
        name="windowed_attention",
    )(sink, qkv, qkv, qkv, qkv, bias, q_norm_g.reshape(1, -1), k_norm_g.reshape(1, -1), beta_attn.reshape(1, -1))


def _split3(x):
    hi = x.astype(BF16)
    r = x - hi.astype(F32)
    mid = r.astype(BF16)
    lo = (r - mid.astype(F32)).astype(BF16)
    return hi, mid, lo


def _gla_direction(qk_ref, v_ref, gr, up_ref, gb_ref, tri_ref, o_ref, st_ref, b_ref, reverse):
    c = GLA_CHUNK
    nch = GLA_TILE // c
    z = _dot(gr.astype(BF16), up_ref[...]) + gb_ref[...]
    log_a = (jnp.minimum(z, 0.0) - jnp.log1p(jnp.exp(-jnp.abs(z)))) / GLA_GATE_NORM
    hi, mid, lo = _split3(log_a)
    tri = tri_ref[...]
    b_ref[...] = _dot(tri, hi) + _dot(tri, mid) + _dot(tri, lo)

    row = lax.broadcasted_iota(jnp.int32, (c, GLA_QK), 0)
    lane = lax.broadcasted_iota(jnp.int32, (c, GLA_QK), 1)
    key_in_head = lane % GLA_KEY_DIM
    causal = (key_in_head >= row) if reverse else (key_in_head <= row)
    qk_head = lane // GLA_KEY_DIM
    v_head = lax.broadcasted_iota(jnp.int32, (c, GLA_V), 1) // GLA_VAL_DIM
    st_rows = lax.broadcasted_iota(jnp.int32, (GLA_V, GLA_QK), 0) // GLA_VAL_DIM
    st_cols = lax.broadcasted_iota(jnp.int32, (GLA_V, GLA_QK), 1) // GLA_KEY_DIM
    same_head = st_rows == st_cols

    for ci in range(nch):
        blk = nch - 1 - ci if reverse else ci
        rows = slice(blk * c, (blk + 1) * c)
        b = b_ref[rows, :]
        b_last = b[0:1, :] if reverse else b[c - 1:c, :]
        q = qk_ref[rows, 0:GLA_QK]
        k = qk_ref[rows, GLA_QK:2 * GLA_QK]
        v = v_ref[rows, :]
        q_dec = (q * ((GLA_KEY_DIM ** -0.5) * jnp.exp(b))).astype(BF16)
        k_dec = k * jnp.exp(-b)
        k_end = (k * jnp.exp(b_last - b)).astype(BF16)
        decay = jnp.exp(b_last)
        k_heads = jnp.concatenate([jnp.where(qk_head == h, k_dec, 0.0) for h in range(GLA_HEADS)], axis=0)
        v_heads = jnp.concatenate([jnp.where(v_head == h, v, 0.0) for h in range(GLA_HEADS)], axis=0)
        a = jnp.where(causal, _nt_dot(q_dec, k_heads.astype(BF16)), 0.0)
        st = st_ref[...]
        o_ref[rows, :] = _dot(a.astype(BF16), v_heads.astype(BF16)) + _nt_dot(q_dec, st.astype(BF16))
        kv_t = lax.dot_general(v.astype(BF16), k_end, (((0,), (0,)), ((), ())), preferred_element_type=F32)
        st_ref[...] = st * decay + jnp.where(same_head, kv_t, 0.0)


def _gla_kernel(qkf_ref, vf_ref, grf_ref, qkb_ref, vb_ref, grb_ref, upf_ref, bf_ref, upb_ref, bb_ref,
                trif_ref, trib_ref, of_ref, ob_ref, stf_ref, stb_ref, cumf_ref, cumb_ref):
    @pl.when(pl.program_id(1) == 0)
    def _():
        stf_ref[...] = jnp.zeros_like(stf_ref)
        stb_ref[...] = jnp.zeros_like(stb_ref)

    _gla_direction(qkf_ref, vf_ref, grf_ref[:, 0:GLA_GATE_RANK], upf_ref, bf_ref, trif_ref,
                   of_ref, stf_ref, cumf_ref, False)
    _gla_direction(qkb_ref, vb_ref, grb_ref[:, GLA_GATE_RANK:2 * GLA_GATE_RANK], upb_ref, bb_ref, trib_ref,
                   ob_ref, stb_ref, cumb_ref, True)


def _gla(gla_in, gates, up_f, b_f, up_b, b_b):
    bsz, seq, _ = gla_in.shape
    tb = GLA_TILE
    nblk = seq // tb
    fwd = lambda cols, cb: pl.BlockSpec((None, tb, cols), lambda b, i: (b, i, cb))
    bwd = lambda cols, cb: pl.BlockSpec((None, tb, cols), lambda b, i: (b, nblk - 1 - i, cb))
    const = lambda shape: pl.BlockSpec(shape, lambda b, i: tuple(0 for _ in shape))
    chunk_of = np.arange(tb) // GLA_CHUNK
    same_chunk = chunk_of[:, None] == chunk_of[None, :]
    pos = np.arange(tb)
    tri_f = jnp.asarray(same_chunk & (pos[None, :] <= pos[:, None]), BF16)
    tri_b = jnp.asarray(same_chunk & (pos[None, :] >= pos[:, None]), BF16)
    state = pltpu.VMEM((GLA_V, GLA_QK), F32)
    cum = pltpu.VMEM((tb, GLA_QK), F32)
    return pl.pallas_call(
        _gla_kernel,
        grid=(bsz, nblk),
        in_specs=[fwd(2 * GLA_QK, 0), fwd(GLA_V, 1), fwd(GATE_COLS, 0),
                  bwd(2 * GLA_QK, 0), bwd(GLA_V, 1), bwd(GATE_COLS, 0),
                  const((GLA_GATE_RANK, GLA_QK)), const((1, GLA_QK)),
                  const((GLA_GATE_RANK, GLA_QK)), const((1, GLA_QK)),
                  const((tb, tb)), const((tb, tb))],
        out_specs=[pl.BlockSpec((None, tb, GLA_V), lambda b, i: (b, i, 0)),
                   pl.BlockSpec((None, tb, GLA_V), lambda b, i: (b, nblk - 1 - i, 0))],
        out_shape=[jax.ShapeDtypeStruct((bsz, seq, GLA_V), F32)] * 2,
        scratch_shapes=[state, state, cum, cum],
        compiler_params=_params(("parallel", "arbitrary"), 2 * tb * (2 * GLA_QK + 2 * GLA_V + V7X_LANES) * 4),
        name="gla_bidirectional",
    )(gla_in, gla_in, gates, gla_in, gla_in, gates,
      up_f.astype(BF16), b_f.reshape(1, -1), up_b.astype(BF16), b_b.reshape(1, -1), tri_f, tri_b)


def _outproj_kernel(att_ref, of_ref, ob_ref, og_ref, x_ref, mod_ref, beta_ref, w_ref, g2_ref, x1_ref, h2_ref):
    o = of_ref[...] + ob_ref[...]
    gate = og_ref[...]
    gate = gate * jax.nn.sigmoid(gate)
    parts = [att_ref[...].astype(BF16)]
    for h in range(GLA_HEADS):
        sl = slice(h * GLA_VAL_DIM, (h + 1) * GLA_VAL_DIM)
        parts.append((_rms(o[:, sl], beta_ref[:, sl]) * gate[:, sl]).astype(BF16))
    mix = _dot(jnp.concatenate(parts, axis=-1), w_ref[...])
    d = D_MODEL
    x1 = x_ref[...] + mod_ref[:, 2 * d:3 * d] * mix
    x1_ref[...] = x1
    h2 = _rms(x1, g2_ref[...]) * (1.0 + mod_ref[:, 4 * d:5 * d]) + mod_ref[:, 3 * d:4 * d]
    h2_ref[...] = h2.astype(BF16)


def _out_projection(att, o_f, o_b, gla_in, x, mod, beta_gla, w_out_bf16, norm2_g):
    bsz, seq, _ = x.shape
    tb = TOK_TILE
    tok = lambda cols, cb=0: pl.BlockSpec((None, tb, cols), lambda b, i: (b, i, cb))
    const = lambda shape: pl.BlockSpec(shape, lambda b, i: tuple(0 for _ in shape))
    return pl.pallas_call(
        _outproj_kernel,
        grid=(bsz, seq // tb),
        in_specs=[tok(ATTN_Q), tok(GLA_V), tok(GLA_V), tok(GLA_V, (2 * GLA_QK + GLA_V) // GLA_V), tok(D_MODEL),
                  pl.BlockSpec((None, 1, 6 * D_MODEL), lambda b, i: (b, 0, 0)),
                  const((1, GLA_V)), const((D_MODEL, D_MODEL)), const((1, D_MODEL))],
        out_specs=[tok(D_MODEL), tok(D_MODEL)],
        out_shape=[jax.ShapeDtypeStruct((bsz, seq, D_MODEL), F32),
                   jax.ShapeDtypeStruct((bsz, seq, D_MODEL), BF16)],
        compiler_params=_params(("parallel", "parallel"), tb * (4 * GLA_V + 3 * D_MODEL) * 4 + D_MODEL * D_MODEL * 2),
        name="out_projection_norm2",
    )(att, o_f, o_b, gla_in, x, mod, beta_gla.reshape(1, -1), w_out_bf16, norm2_g.reshape(1, -1))


def _sort_network(n):
    def merge(lo, hi, r):
        step = 2 * r
        if step < hi - lo:
            yield from merge(lo, hi, step)
            yield from merge(lo + r, hi, step)
            yield from ((i, i + r) for i in range(lo + r, hi - r, step))
        else:
            yield (lo, lo + r)

    def sort(lo, hi):
        if hi > lo:
            mid = lo + (hi - lo) // 2
            yield from sort(lo, mid)
            yield from sort(mid + 1, hi)
            yield from merge(lo, hi, 1)

    return tuple(sort(0, n - 1))


def _top_values(x, out_ref, head):
    n = N_KEYS // V7X_SUBLANES
    assert n == PEER_TOPK
    tiles =[x[i * V7X_SUBLANES:(i + 1) * V7X_SUBLANES, :] for i in range(n)]
    srt = list(tiles)
    for i, j in _sort_network(n):
        srt[i], srt[j] = jnp.maximum(srt[i], srt[j]), jnp.minimum(srt[i], srt[j])
    shift = V7X_SUBLANES // 2
    while shift >= 1:
        other = [pltpu.roll(t, shift, 0) for t in srt]
        srt = [jnp.maximum(srt[i], other[n - 1 - i]) for i in range(n)]
        stride = n // 2
        while stride >= 1:
            for i in range(n):
                if i & stride == 0:
                    srt[i], srt[i + stride] = (jnp.maximum(srt[i], srt[i + stride]),
                                               jnp.minimum(srt[i], srt[i + stride]))
            stride //= 2
        shift //= 2
    for r in range(PEER_TOPK):
        out_ref[r, pl.ds(head, 1), :] = srt[r][0:1, :]
    below = [jnp.where(t < srt[PEER_TOPK - 1], t, -jnp.inf) for t in tiles]
    out_ref[PEER_TOPK, pl.ds(head, 1), :] = jnp.max(functools.reduce(jnp.maximum, below), axis=0, keepdims=True)


def _peer_route_kernel(h2_ref, wq_ref, keys_ref, h2t_ref, s2_ref, e2_ref, thr_ref, c_ref, sc_ref, a_ref, b_ref):
    h2 = h2_ref[...]
    h2t_ref[...] = h2.astype(F32).T.astype(BF16)
    q = _dot(h2, wq_ref[...]).astype(BF16)
    for h in range(PEER_HEADS):
        for p in range(2):
            blk = h * 2 + p
            s = _nt_dot(keys_ref[h, p], q[:, blk * PEER_HALF:(blk + 1) * PEER_HALF])
            sc_ref[blk] = s
            _top_values(s, a_ref if p == 0 else b_ref, h)
    cand = [a_ref[i] + b_ref[j] for (i, j) in CAND_PAIRS]
    vals = []
    for r in range(N_RANKS):
        m = functools.reduce(jnp.maximum, cand)
        vals.append(m)
        cand = [jnp.where(x == m, -jnp.inf, x) for x in cand]
    top = vals[0]
    thr = 0.5 * (vals[PEER_TOPK - 1] + vals[PEER_TOPK])
    z = functools.reduce(jnp.add, [jnp.exp(v - top) for v in vals[:PEER_TOPK]])
    inv_z = 1.0 / z
    a0 = a_ref[0]
    b0 = b_ref[0]
    for h in range(PEER_HEADS):
        s1 = sc_ref[2 * h]
        s2 = sc_ref[2 * h + 1]
        thr_ref[h] = thr[h:h + 1, :] - s1
        c_ref[h] = jnp.exp(s1 - a0[h:h + 1, :]) * inv_z[h:h + 1, :]
        s2_ref[h] = s2
        e2_ref[h] = jnp.exp(s2 - b0[h:h + 1, :])


def _peer_route(h2, wq_bf16, keys_bf16):
    bsz, seq, _ = h2.shape
    tb = TOK_TILE
    per_head = lambda: pl.BlockSpec((None, PEER_HEADS, N_KEYS, tb), lambda b, i: (b, 0, 0, i))
    head_shape = jax.ShapeDtypeStruct((bsz, PEER_HEADS, N_KEYS, seq), F32)
    qcols = PEER_HEADS * PEER_KEY_DIM
    return pl.pallas_call(
        _peer_route_kernel,
        grid=(bsz, seq // tb),
        in_specs=[pl.BlockSpec((None, tb, D_MODEL), lambda b, i: (b, i, 0)),
                  pl.BlockSpec((D_MODEL, qcols), lambda b, i: (0, 0)),
                  pl.BlockSpec((PEER_HEADS, 2, N_KEYS, PEER_HALF), lambda b, i: (0, 0, 0, 0))],
        out_specs=[pl.BlockSpec((None, None, D_MODEL, tb), lambda b, i: (b, i, 0, 0)),
                   per_head(), per_head(), per_head(), per_head()],
        out_shape=[jax.ShapeDtypeStruct((bsz, seq // tb, D_MODEL, tb), BF16),
                   head_shape, head_shape, head_shape, head_shape],
        scratch_shapes=[pltpu.VMEM((2 * PEER_HEADS, N_KEYS, tb), F32),
                        pltpu.VMEM((N_RANKS, PEER_HEADS, tb), F32),
                        pltpu.VMEM((N_RANKS, PEER_HEADS, tb), F32)],
        compiler_params=_params(("parallel", "parallel"),
                                D_MODEL * qcols * 2 + 5 * PEER_HEADS * N_KEYS * tb * 4 + tb * qcols * 4),
        name="peer_routing",
    )(h2, wq_bf16, keys_bf16)


def _gelu(x):
    return 0.5 * x * (1.0 + lax.erf(x * (2.0 ** -0.5)))


def _routing_block(s2_ref, e2_ref, thr_ref, c_ref, row0, act_ref, wg_ref, col0, k0):
    cols = slice(col0, col0 + V7X_LANES)
    local = slice(col0 % PEER_PASS_TOK, col0 % PEER_PASS_TOK + V7X_LANES)
    for kk in range(0, KEY_CHUNK, KEY_BLOCK):
        keys = pl.ds(pl.multiple_of(k0 + kk, KEY_BLOCK), KEY_BLOCK)
        w = [jnp.zeros((KEY_BLOCK, V7X_LANES), F32) for _ in range(ROWS_PER_TILE)]
        for h in range(PEER_HEADS):
            s2 = s2_ref[h, keys, cols]
            e2 = e2_ref[h, keys, cols]
            for r in range(ROWS_PER_TILE):
                thr = thr_ref[h, row0 + r:row0 + r + 1, cols]
                coef = c_ref[h, row0 + r:row0 + r + 1, cols]
                w[r] = w[r] + jnp.where(s2 >= thr, e2 * coef, 0.0)
        for r in range(ROWS_PER_TILE):
            rows = pl.ds(pl.multiple_of(r * N_KEYS + k0 + kk, KEY_BLOCK), KEY_BLOCK)
            wg_ref[rows, local] = (w[r] * _gelu(act_ref[rows, local])).astype(BF16)


def _peer_dense_kernel(h2t_ref, u_ref, vt_ref, s2_ref, e2_ref, thr0_ref, c0_ref, thr1_ref, c1_ref, x1_ref, mod_ref,
                       y_ref, *scratch):
    g = pl.program_id(2)
    n_pass = PEER_TOK_TILE // PEER_PASS_TOK
    act_refs = [scratch[k * n_pass:(k + 1) * n_pass] for k in range(2)]
    wg_refs = [scratch[(2 + k) * n_pass:(3 + k) * n_pass] for k in range(2)]
    acc_refs = scratch[4 * n_pass:5 * n_pass]

    @pl.when(g == 0)
    def _():
        for ref in scratch:
            ref[...] = jnp.zeros_like(ref)

    for k in range(2):
        thr_ref, c_ref, row0 = (thr0_ref, c0_ref, ROWS_PER_TILE) if k == 0 else (thr1_ref, c1_ref, 0)
        tile = slice(k * EXPERT_TILE, (k + 1) * EXPERT_TILE)
        has_tile = (g > 0) if k == 0 else (g < pl.num_programs(2) - 1)
        for p in range(n_pass):
            def slab(c, carry, k=k, p=p, thr_ref=thr_ref, c_ref=c_ref, row0=row0, tile=tile, with_stage_b=True):
                a0 = pl.multiple_of(c * ACT_SLAB, ACT_SLAB)
                act_refs[k][p][pl.ds(a0, ACT_SLAB), :] = _dot(u_ref[pl.ds(k * EXPERT_TILE + a0, ACT_SLAB), :],
                                                             h2t_ref[p])
                rows = pl.ds(pl.multiple_of(c * OUT_SLAB, OUT_SLAB), OUT_SLAB)
                acc_refs[p][rows, :] += _dot(vt_ref[rows, tile], wg_refs[k][p][...])
                if with_stage_b:
                    k0 = pl.multiple_of(c * KEY_CHUNK, KEY_CHUNK)
                    for t in range(PEER_PASS_TOK // V7X_LANES):
                        _routing_block(s2_ref, e2_ref, thr_ref, c_ref, row0, act_refs[1 - k][p], wg_refs[1 - k][p],
                                       p * PEER_PASS_TOK + t * V7X_LANES, k0)
                return carry

            @pl.when(has_tile)
            def _(slab=slab):
                lax.fori_loop(0, CHUNKS_PER_TILE, slab, 0)

            @pl.when(jnp.logical_not(has_tile))
            def _(slab=slab):
                lax.fori_loop(0, CHUNKS_PER_TILE, functools.partial(slab, with_stage_b=False), 0)

    @pl.when(g == pl.num_programs(2) - 1)
    def _():
        gate = mod_ref[:, 5 * D_MODEL:6 * D_MODEL]
        for p in range(n_pass):
            rows = slice(p * PEER_PASS_TOK, (p + 1) * PEER_PASS_TOK)
            y_ref[rows, :] = x1_ref[rows, :] + gate * acc_refs[p][...].T


def _transposed_blocks(v):
    blk = 2 * EXPERT_TILE
    return jnp.transpose(v.reshape(v.shape[0] // blk, blk, v.shape[1]), (0, 2, 1))


def _peer_dense(h2t, u, vt, s2, e2, thr, coef, x1, mod):
    bsz, seq, _ = x1.shape
    tb = PEER_TOK_TILE
    n_pass = tb // PEER_PASS_TOK
    n_tiles = u.shape[0] // EXPERT_TILE
    n_pairs = n_tiles // 2
    once = pl.Buffered(1)
    per_head = lambda: pl.BlockSpec((None, PEER_HEADS, N_KEYS, tb), lambda b, i, g: (b, 0, 0, i), pipeline_mode=once)
    row_block = lambda f: pl.BlockSpec((None, PEER_HEADS, ROW_BLOCK, tb),
                                       lambda b, i, g: (b, 0, jnp.clip(f(g), 0, n_pairs - 1), i))
    pass_f32 = pltpu.VMEM((EXPERT_TILE, PEER_PASS_TOK), F32)
    pass_bf16 = pltpu.VMEM((EXPERT_TILE, PEER_PASS_TOK), BF16)
    return pl.pallas_call(
        _peer_dense_kernel,
        grid=(bsz, seq // tb, n_pairs + 1),
        in_specs=[pl.BlockSpec((None, n_pass, D_MODEL, PEER_PASS_TOK), lambda b, i, g: (b, i, 0, 0), pipeline_mode=once),
                  pl.BlockSpec((2 * EXPERT_TILE, D_MODEL), lambda b, i, g: (jnp.minimum(g, n_pairs - 1), 0)),
                  pl.BlockSpec((None, D_MODEL, 2 * EXPERT_TILE), lambda b, i, g: (jnp.maximum(g - 1, 0), 0, 0)),
                  per_head(), per_head(),
                  row_block(lambda g: g - 1), row_block(lambda g: g - 1),
                  row_block(lambda g: g), row_block(lambda g: g),
                  pl.BlockSpec((None, tb, D_MODEL), lambda b, i, g: (b, i, 0), pipeline_mode=once),
                  pl.BlockSpec((None, 1, 6 * D_MODEL), lambda b, i, g: (b, 0, 0))],
        out_specs=pl.BlockSpec((None, tb, D_MODEL), lambda b, i, g: (b, i, 0)),
        out_shape=jax.ShapeDtypeStruct((bsz, seq, D_MODEL), F32),
        scratch_shapes=([pass_f32] * (2 * n_pass) + [pass_bf16] * (2 * n_pass)
                        + [pltpu.VMEM((D_MODEL, PEER_PASS_TOK), F32)] * n_pass),
        compiler_params=_params(("parallel", "parallel", "arbitrary"),
                                2 * PEER_HEADS * N_KEYS * tb * 4 + 2 * 2 * EXPERT_TILE * D_MODEL * 4
                                + 3 * tb * D_MODEL * 4 + 4 * EXPERT_TILE * tb * 4),
        name="peer_dense_experts",
    )(h2t, u, vt, s2, e2, thr, coef, thr, coef, x1, mod)


def _layer(x, c, w_ada, b_ada, norm1_g, norm2_g, w_in, q_norm_g, k_norm_g, bias, sink,
           up_f, b_f, up_b, b_b, beta_attn, beta_gla, w_out, wq, keys, u, vt):
    mod = _modulation(c, w_ada, b_ada)
    qkv, gla_in, gates = _in_projection(x, mod, norm1_g, w_in)
    att = _windowed_attention(qkv, bias, sink, q_norm_g, k_norm_g, beta_attn)
    o_f, o_b = _gla(gla_in, gates, up_f, b_f, up_b, b_b)
    x1, h2 = _out_projection(att, o_f, o_b, gla_in, x, mod, beta_gla, w_out, norm2_g)
    h2t, s2, e2, thr, coef = _peer_route(h2, wq, keys)
    return _peer_dense(h2t, u, vt, s2, e2, thr, coef, x1, mod)


def kernel(x_prompt, x_sample, c_prompt, c_sample, w_ada, b_ada, norm1_g, norm2_g, w_in, q_norm_g, k_norm_g,
           rel_bias, sink, gk_up_fwd, gk_bias_fwd, gk_up_bwd, gk_bias_bwd, beta_attn, beta_gla, w_out,
           peer_query, peer_subkeys, peer_u, peer_v):
    depth = w_ada.shape[0]
    bias = _band_bias(rel_bias)
    y_prompt, y_sample = x_prompt, x_sample
    for l in range(depth):
        params = (w_ada[l], b_ada[l], norm1_g[l], norm2_g[l], w_in[l].astype(BF16), q_norm_g[l], k_norm_g[l],
                  bias, sink[l], gk_up_fwd[l], gk_bias_fwd[l], gk_up_bwd[l], gk_bias_bwd[l],
                  beta_attn[l], beta_gla[l], w_out[l].astype(BF16), peer_query[l].astype(BF16),
                  peer_subkeys[l].astype(BF16), peer_u[l], _transposed_blocks(peer_v[l]))
        y_prompt = _layer(y_prompt, c_prompt, *params)
        y_sample = _layer(y_sample, c_sample, *params)
    return (y_prompt, y_sample)
```

```python
import functools
import math

import numpy as np
import jax
import jax.numpy as jnp
from jax import lax
from jax.experimental import pallas as pl
from jax.experimental.pallas import tpu as pltpu

F32 = jnp.float32
BF16 = jnp.bfloat16

D_MODEL = 1024
ATTN_HEADS = 8
ATTN_KV_HEADS = 2
ATTN_HEAD_DIM = 64
ATTN_GROUP = ATTN_HEADS // ATTN_KV_HEADS
WINDOW = 128
BLOCK = 128
N_BUCKETS = 32
MAX_DISTANCE = 128
GLA_HEADS = 4
GLA_KEY_DIM = 64
GLA_VAL_DIM = 128
GLA_GATE_RANK = 16
GLA_GATE_NORM = 16.0
GLA_CHUNK = 64
PEER_HEADS = 8
N_KEYS = 128
PEER_KEY_DIM = 256
PEER_HALF = PEER_KEY_DIM // 2
PEER_TOPK = 16
EPS = 1e-6

ATTN_Q = ATTN_HEADS * ATTN_HEAD_DIM
ATTN_KV = ATTN_KV_HEADS * ATTN_HEAD_DIM
GLA_QK = GLA_HEADS * GLA_KEY_DIM
GLA_V = GLA_HEADS * GLA_VAL_DIM
ATTN_COLS = ATTN_Q + 2 * ATTN_KV
GLA_COLS = 2 * GLA_QK + 2 * GLA_V
GATE_COLS = 2 * GLA_GATE_RANK
D_IN = ATTN_COLS + GLA_COLS + GATE_COLS

V7X_LANES = 128
V7X_SUBLANES = 8
V7X_VMEM_LIMIT_CAP = 60 * 1024 * 1024

NEG_BIG = -1e30

TOK_TILE = 512
GLA_TILE = 512
PEER_TOK_TILE = 1024
PEER_PASS_TOK = TOK_TILE
EXPERT_TILE = 512
ROWS_PER_TILE = EXPERT_TILE // N_KEYS
ROW_BLOCK = 2 * ROWS_PER_TILE
KEY_CHUNK = 64
KEY_BLOCK = 32
CHUNKS_PER_TILE = N_KEYS // KEY_CHUNK
ACT_SLAB = EXPERT_TILE // CHUNKS_PER_TILE
OUT_SLAB = D_MODEL // CHUNKS_PER_TILE
N_RANKS = PEER_TOPK + 1
CAND_PAIRS = tuple((i, j) for i in range(N_RANKS) for j in range(N_RANKS) if (i + 1) * (j + 1) <= N_RANKS)


def _vmem_limit(block_bytes):
    return int(min(2 * block_bytes + (16 << 20), V7X_VMEM_LIMIT_CAP))


def _params(semantics, block_bytes):
    return pltpu.CompilerParams(dimension_semantics=semantics, vmem_limit_bytes=_vmem_limit(block_bytes))


def _nt_dot(a, b):
    return lax.dot_general(a, b, (((1,), (1,)), ((), ())), preferred_element_type=F32)


def _dot(a, b):
    return jnp.dot(a, b, preferred_element_type=F32)


def _rms(x, g):
    return x * lax.rsqrt(jnp.mean(x * x, axis=-1, keepdims=True) + EPS) * g


def _mod_kernel(c_ref, w_ref, b_ref, o_ref):
    c = c_ref[...]
    s = c * jax.nn.sigmoid(c)
    o_ref[...] = _dot(s.astype(BF16), w_ref[...].astype(BF16)) + b_ref[...]


def _modulation(c, w_ada, b_ada):
    bsz = c.shape[0]
    rows = -(-bsz // V7X_SUBLANES) * V7X_SUBLANES
    cp = jnp.zeros((rows, D_MODEL), F32).at[:bsz].set(c)
    n_out = w_ada.shape[1]
    tile = D_MODEL
    out = pl.pallas_call(
        _mod_kernel,
        grid=(n_out // tile,),
        in_specs=[pl.BlockSpec((rows, D_MODEL), lambda j: (0, 0)),
                  pl.BlockSpec((D_MODEL, tile), lambda j: (0, j)),
                  pl.BlockSpec((1, tile), lambda j: (0, j))],
        out_specs=pl.BlockSpec((rows, tile), lambda j: (0, j)),
        out_shape=jax.ShapeDtypeStruct((rows, n_out), F32),
        compiler_params=_params(("parallel",), D_MODEL * tile * 4),
        name="adaln_modulation",
    )(cp, w_ada, b_ada.reshape(1, n_out))
    return out[:bsz].reshape(bsz, 1, n_out)


def _inproj_kernel(x_ref, mod_ref, g_ref, w_ref, oa_ref, og_ref, or_ref):
    h = _rms(x_ref[...], g_ref[...])
    h = h * (1.0 + mod_ref[:, D_MODEL:2 * D_MODEL]) + mod_ref[:, 0:D_MODEL]
    p = _dot(h.astype(BF16), w_ref[...])
    oa_ref[...] = p[:, :ATTN_COLS]
    og_ref[...] = p[:, ATTN_COLS:ATTN_COLS + GLA_COLS]
    or_ref[...] = p[:, ATTN_COLS + GLA_COLS:]


def _in_projection(x, mod, norm_g, w_in_bf16):
    bsz, seq, _ = x.shape
    tb = TOK_TILE
    tok = lambda cols: pl.BlockSpec((None, tb, cols), lambda b, i: (b, i, 0))
    return pl.pallas_call(
        _inproj_kernel,
        grid=(bsz, seq // tb),
        in_specs=[tok(D_MODEL),
                  pl.BlockSpec((None, 1, 6 * D_MODEL), lambda b, i: (b, 0, 0)),
                  pl.BlockSpec((1, D_MODEL), lambda b, i: (0, 0)),
                  pl.BlockSpec((D_MODEL, D_IN), lambda b, i: (0, 0))],
        out_specs=[tok(ATTN_COLS), tok(GLA_COLS), tok(GATE_COLS)],
        out_shape=[jax.ShapeDtypeStruct((bsz, seq, ATTN_COLS), F32),
                   jax.ShapeDtypeStruct((bsz, seq, GLA_COLS), F32),
                   jax.ShapeDtypeStruct((bsz, seq, GATE_COLS), F32)],
        compiler_params=_params(("parallel", "parallel"),
                                tb * D_MODEL * 4 + D_MODEL * D_IN * 2 + 2 * tb * D_IN * 4),
        name="norm1_in_projection",
    )(x, mod, norm_g.reshape(1, D_MODEL), w_in_bf16)


def _t5_buckets(rel):
    nb = N_BUCKETS // 2
    ret = (rel > 0).astype(np.int32) * nb
    n = np.abs(rel)
    max_exact = nb // 2
    large = max_exact + (np.log(np.maximum(n, 1) / max_exact) / math.log(MAX_DISTANCE / max_exact)
                         * (nb - max_exact)).astype(np.int32)
    large = np.minimum(large, nb - 1)
    return (ret + np.where(n < max_exact, n, large)).astype(np.int32)


def _band_bias(rel_bias):
    span = 4 * BLOCK
    offsets = np.arange(span) - (2 * BLOCK - 1)
    per_offset = rel_bias[_t5_buckets(offsets)].astype(F32).T
    shifted = jnp.roll(per_offset, -(BLOCK - 1), axis=1)
    skew = jnp.tile(shifted, (1, BLOCK))[:, :BLOCK * (span - 1)].reshape(-1, BLOCK, span - 1)
    rel = np.arange(3 * BLOCK)[None, :] - BLOCK - np.arange(BLOCK)[:, None]
    band = jnp.asarray(np.abs(rel) <= WINDOW)
    return jnp.where(band[None], skew[:, :, :3 * BLOCK], NEG_BIG)


def _attn_kernel(sink_ref, q_ref, kvp_ref, kvc_ref, kvn_ref, bias_ref, qg_ref, kg_ref, beta_ref, o_ref):
    n = pl.program_id(1)
    last = pl.num_programs(1) - 1
    q = q_ref[...]
    kv = jnp.concatenate([kvp_ref[...], kvc_ref[...], kvn_ref[...]], axis=0)
    j = lax.broadcasted_iota(jnp.int32, (BLOCK, 3 * BLOCK), 1)
    valid = jnp.logical_and(jnp.logical_or(j >= BLOCK, n > 0), jnp.logical_or(j < 2 * BLOCK, n < last))
    scale = ATTN_HEAD_DIM ** -0.5
    outs = []
    for hk in range(ATTN_KV_HEADS):
        kh = _rms(kv[:, hk * ATTN_HEAD_DIM:(hk + 1) * ATTN_HEAD_DIM], kg_ref[...]).astype(BF16)
        vh = kv[:, ATTN_KV + hk * ATTN_HEAD_DIM:ATTN_KV + (hk + 1) * ATTN_HEAD_DIM].astype(BF16)
        for g in range(ATTN_GROUP):
            h = hk * ATTN_GROUP + g
            qh = _rms(q[:, h * ATTN_HEAD_DIM:(h + 1) * ATTN_HEAD_DIM], qg_ref[...]).astype(BF16)
            s = _nt_dot(qh, kh) * scale
            s = jnp.where(valid, s + bias_ref[h], NEG_BIG)
            sink = sink_ref[h]
            m = jnp.maximum(jnp.max(s, axis=-1, keepdims=True), sink)
            p = jnp.exp(s - m)
            den = jnp.sum(p, axis=-1, keepdims=True) + jnp.exp(sink - m)
            o = _dot(p.astype(BF16), vh) / den
            outs.append(_rms(o, beta_ref[:, h * ATTN_HEAD_DIM:(h + 1) * ATTN_HEAD_DIM]))
    o_ref[...] = jnp.concatenate(outs, axis=-1)


def _windowed_attention(qkv, bias, sink, q_norm_g, k_norm_g, beta_attn):
    bsz, seq, _ = qkv.shape
    nb = seq // BLOCK
    kv_cols = 2 * ATTN_KV
    kv_blk = ATTN_Q // kv_cols
    kv_spec = lambda f: pl.BlockSpec((None, BLOCK, kv_cols), lambda b, n: (b, f(n), kv_blk))
    const = lambda shape: pl.BlockSpec(shape, lambda b, n: tuple(0 for _ in shape))
    return pl.pallas_call(
        _attn_kernel,
        grid=(bsz, nb),
        in_specs=[pl.BlockSpec(memory_space=pltpu.SMEM),
                  pl.BlockSpec((None, BLOCK, ATTN_Q), lambda b, n: (b, n, 0)),
                  kv_spec(lambda n: jnp.maximum(n - 1, 0)),
                  kv_spec(lambda n: n),
                  kv_spec(lambda n: jnp.minimum(n + 1, nb - 1)),
                  const((ATTN_HEADS, BLOCK, 3 * BLOCK)),
                  const((1, ATTN_HEAD_DIM)), const((1, ATTN_HEAD_DIM)), const((1, ATTN_Q))],
        out_specs=pl.BlockSpec((None, BLOCK, ATTN_Q), lambda b, n: (b, n, 0)),
        out_shape=jax.ShapeDtypeStruct((bsz, seq, ATTN_Q), F32),
        compiler_params=_params(("parallel", "parallel"), ATTN_HEADS * BLOCK * 3 * BLOCK * 4 + 8 * BLOCK * ATTN_Q * 4),
        name="windowed_attention",
    )(sink, qkv, qkv, qkv, qkv, bias, q_norm_g.reshape(1, -1), k_norm_g.reshape(1, -1), beta_attn.reshape(1, -1))


def _split3(x):
    hi = x.astype(BF16)
    r = x - hi.astype(F32)
    mid = r.astype(BF16)
    lo = (r - mid.astype(F32)).astype(BF16)
    return hi, mid, lo


def _gla_direction(qk_ref, v_ref, gr, up_ref, gb_ref, tri_ref, o_ref, st_ref, b_ref, reverse):
    c = GLA_CHUNK
    nch = GLA_TILE // c
    z = _dot(gr.astype(BF16), up_ref[...]) + gb_ref[...]
    log_a = (jnp.minimum(z, 0.0) - jnp.log1p(jnp.exp(-jnp.abs(z)))) / GLA_GATE_NORM
    hi, mid, lo = _split3(log_a)
    tri = tri_ref[...]
    b_ref[...] = _dot(tri, hi) + _dot(tri, mid) + _dot(tri, lo)

    row = lax.broadcasted_iota(jnp.int32, (c, GLA_QK), 0)
    lane = lax.broadcasted_iota(jnp.int32, (c, GLA_QK), 1)
    key_in_head = lane % GLA_KEY_DIM
    causal = (key_in_head >= row) if reverse else (key_in_head <= row)
    qk_head = lane // GLA_KEY_DIM
    v_head = lax.broadcasted_iota(jnp.int32, (c, GLA_V), 1) // GLA_VAL_DIM
    st_rows = lax.broadcasted_iota(jnp.int32, (GLA_V, GLA_QK), 0) // GLA_VAL_DIM
    st_cols = lax.broadcasted_iota(jnp.int32, (GLA_V, GLA_QK), 1) // GLA_KEY_DIM
    same_head = st_rows == st_cols

    for ci in range(nch):
        blk = nch - 1 - ci if reverse else ci
        rows = slice(blk * c, (blk + 1) * c)
        b = b_ref[rows, :]
        b_last = b[0:1, :] if reverse else b[c - 1:c, :]
        q = qk_ref[rows, 0:GLA_QK]
        k = qk_ref[rows, GLA_QK:2 * GLA_QK]
        v = v_ref[rows, :]
        q_dec = (q * ((GLA_KEY_DIM ** -0.5) * jnp.exp(b))).astype(BF16)
        k_dec = k * jnp.exp(-b)
        k_end = (k * jnp.exp(b_last - b)).astype(BF16)
        decay = jnp.exp(b_last)
        k_heads = jnp.concatenate([jnp.where(qk_head == h, k_dec, 0.0) for h in range(GLA_HEADS)], axis=0)
        v_heads = jnp.concatenate([jnp.where(v_head == h, v, 0.0) for h in range(GLA_HEADS)], axis=0)
        a = jnp.where(causal, _nt_dot(q_dec, k_heads.astype(BF16)), 0.0)
        st = st_ref[...]
        o_ref[rows, :] = _dot(a.astype(BF16), v_heads.astype(BF16)) + _nt_dot(q_dec, st.astype(BF16))
        kv_t = lax.dot_general(v.astype(BF16), k_end, (((0,), (0,)), ((), ())), preferred_element_type=F32)
        st_ref[...] = st * decay + jnp.where(same_head, kv_t, 0.0)


def _gla_kernel(qkf_ref, vf_ref, grf_ref, qkb_ref, vb_ref, grb_ref, upf_ref, bf_ref, upb_ref, bb_ref,
                trif_ref, trib_ref, of_ref, ob_ref, stf_ref, stb_ref, cumf_ref, cumb_ref):
    @pl.when(pl.program_id(1) == 0)
    def _():
        stf_ref[...] = jnp.zeros_like(stf_ref)
        stb_ref[...] = jnp.zeros_like(stb_ref)

    _gla_direction(qkf_ref, vf_ref, grf_ref[:, 0:GLA_GATE_RANK], upf_ref, bf_ref, trif_ref,
                   of_ref, stf_ref, cumf_ref, False)
    _gla_direction(qkb_ref, vb_ref, grb_ref[:, GLA_GATE_RANK:2 * GLA_GATE_RANK], upb_ref, bb_ref, trib_ref,
                   ob_ref, stb_ref, cumb_ref, True)


def _gla(gla_in, gates, up_f, b_f, up_b, b_b):
    bsz, seq, _ = gla_in.shape
    tb = GLA_TILE
    nblk = seq // tb
    fwd = lambda cols, cb: pl.BlockSpec((None, tb, cols), lambda b, i: (b, i, cb))
    bwd = lambda cols, cb: pl.BlockSpec((None, tb, cols), lambda b, i: (b, nblk - 1 - i, cb))
    const = lambda shape: pl.BlockSpec(shape, lambda b, i: tuple(0 for _ in shape))
    chunk_of = np.arange(tb) // GLA_CHUNK
    same_chunk = chunk_of[:, None] == chunk_of[None, :]
    pos = np.arange(tb)
    tri_f = jnp.asarray(same_chunk & (pos[None, :] <= pos[:, None]), BF16)
    tri_b = jnp.asarray(same_chunk & (pos[None, :] >= pos[:, None]), BF16)
    state = pltpu.VMEM((GLA_V, GLA_QK), F32)
    cum = pltpu.VMEM((tb, GLA_QK), F32)
    return pl.pallas_call(
        _gla_kernel,
        grid=(bsz, nblk),
        in_specs=[fwd(2 * GLA_QK, 0), fwd(GLA_V, 1), fwd(GATE_COLS, 0),
                  bwd(2 * GLA_QK, 0), bwd(GLA_V, 1), bwd(GATE_COLS, 0),
                  const((GLA_GATE_RANK, GLA_QK)), const((1, GLA_QK)),
                  const((GLA_GATE_RANK, GLA_QK)), const((1, GLA_QK)),
                  const((tb, tb)), const((tb, tb))],
        out_specs=[pl.BlockSpec((None, tb, GLA_V), lambda b, i: (b, i, 0)),
                   pl.BlockSpec((None, tb, GLA_V), lambda b, i: (b, nblk - 1 - i, 0))],
        out_shape=[jax.ShapeDtypeStruct((bsz, seq, GLA_V), F32)] * 2,
        scratch_shapes=[state, state, cum, cum],
        compiler_params=_params(("parallel", "arbitrary"), 2 * tb * (2 * GLA_QK + 2 * GLA_V + V7X_LANES) * 4),
        name="gla_bidirectional",
    )(gla_in, gla_in, gates, gla_in, gla_in, gates,
      up_f.astype(BF16), b_f.reshape(1, -1), up_b.astype(BF16), b_b.reshape(1, -1), tri_f, tri_b)


def _outproj_kernel(att_ref, of_ref, ob_ref, og_ref, x_ref, mod_ref, beta_ref, w_ref, g2_ref, x1_ref, h2_ref):
    o = of_ref[...] + ob_ref[...]
    gate = og_ref[...]
    gate = gate * jax.nn.sigmoid(gate)
    parts = [att_ref[...].astype(BF16)]
    for h in range(GLA_HEADS):
        sl = slice(h * GLA_VAL_DIM, (h + 1) * GLA_VAL_DIM)
        parts.append((_rms(o[:, sl], beta_ref[:, sl]) * gate[:, sl]).astype(BF16))
    mix = _dot(jnp.concatenate(parts, axis=-1), w_ref[...])
    d = D_MODEL
    x1 = x_ref[...] + mod_ref[:, 2 * d:3 * d] * mix
    x1_ref[...] = x1
    h2 = _rms(x1, g2_ref[...]) * (1.0 + mod_ref[:, 4 * d:5 * d]) + mod_ref[:, 3 * d:4 * d]
    h2_ref[...] = h2.astype(BF16)


def _out_projection(att, o_f, o_b, gla_in, x, mod, beta_gla, w_out_bf16, norm2_g):
    bsz, seq, _ = x.shape
    tb = TOK_TILE
    tok = lambda cols, cb=0: pl.BlockSpec((None, tb, cols), lambda b, i: (b, i, cb))
    const = lambda shape: pl.BlockSpec(shape, lambda b, i: tuple(0 for _ in shape))
    return pl.pallas_call(
        _outproj_kernel,
        grid=(bsz, seq // tb),
        in_specs=[tok(ATTN_Q), tok(GLA_V), tok(GLA_V), tok(GLA_V, (2 * GLA_QK + GLA_V) // GLA_V), tok(D_MODEL),
                  pl.BlockSpec((None, 1, 6 * D_MODEL), lambda b, i: (b, 0, 0)),
                  const((1, GLA_V)), const((D_MODEL, D_MODEL)), const((1, D_MODEL))],
        out_specs=[tok(D_MODEL), tok(D_MODEL)],
        out_shape=[jax.ShapeDtypeStruct((bsz, seq, D_MODEL), F32),
                   jax.ShapeDtypeStruct((bsz, seq, D_MODEL), BF16)],
        compiler_params=_params(("parallel", "parallel"), tb * (4 * GLA_V + 3 * D_MODEL) * 4 + D_MODEL * D_MODEL * 2),
        name="out_projection_norm2",
    )(att, o_f, o_b, gla_in, x, mod, beta_gla.reshape(1, -1), w_out_bf16, norm2_g.reshape(1, -1))


def _sort_network(n):
    def merge(lo, hi, r):
        step = 2 * r
        if step < hi - lo:
            yield from merge(lo, hi, step)
            yield from merge(lo + r, hi, step)
            yield from ((i, i + r) for i in range(lo + r, hi - r, step))
        else:
            yield (lo, lo + r)

    def sort(lo, hi):
        if hi > lo:
            mid = lo + (hi - lo) // 2
            yield from sort(lo, mid)
            yield from sort(mid + 1, hi)
            yield from merge(lo, hi, 1)

    return tuple(sort(0, n - 1))


def _selection_network(n, keep):
    pairs = [(i, j) for i, j in _sort_network(1 << (n - 1).bit_length()) if j < n]
    needed = set(range(keep))
    net = []
    for i, j in reversed(pairs):
        if i in needed or j in needed:
            net.append((i, j, i in needed, j in needed))
            needed |= {i, j}
    return tuple(reversed(net))


CAND_NETWORK = _selection_network(len(CAND_PAIRS), N_RANKS)


def _top_values(x, out_ref, head):
    n = N_KEYS // V7X_SUBLANES
    assert n == PEER_TOPK
    tiles =[x[i * V7X_SUBLANES:(i + 1) * V7X_SUBLANES, :] for i in range(n)]
    srt = list(tiles)
    for i, j in _sort_network(n):
        srt[i], srt[j] = jnp.maximum(srt[i], srt[j]), jnp.minimum(srt[i], srt[j])
    shift = V7X_SUBLANES // 2
    while shift >= 1:
        other = [pltpu.roll(t, shift, 0) for t in srt]
        srt = [jnp.maximum(srt[i], other[n - 1 - i]) for i in range(n)]
        stride = n // 2
        while stride >= 1:
            for i in range(n):
                if i & stride == 0:
                    srt[i], srt[i + stride] = (jnp.maximum(srt[i], srt[i + stride]),
                                               jnp.minimum(srt[i], srt[i + stride]))
            stride //= 2
        shift //= 2
    for r in range(PEER_TOPK):
        out_ref[r, pl.ds(head, 1), :] = srt[r][0:1, :]
    below = [jnp.where(t < srt[PEER_TOPK - 1], t, -jnp.inf) for t in tiles]
    out_ref[PEER_TOPK, pl.ds(head, 1), :] = jnp.max(functools.reduce(jnp.maximum, below), axis=0, keepdims=True)


def _peer_route_kernel(h2_ref, wq_ref, keys_ref, h2t_ref, s2_ref, e2_ref, thr_ref, c_ref, sc_ref, a_ref, b_ref):
    h2 = h2_ref[...]
    h2t_ref[...] = h2.astype(F32).T.astype(BF16)
    q = _dot(h2, wq_ref[...]).astype(BF16)
    for h in range(PEER_HEADS):
        for p in range(2):
            blk = h * 2 + p
            s = _nt_dot(keys_ref[h, p], q[:, blk * PEER_HALF:(blk + 1) * PEER_HALF])
            sc_ref[blk] = s
            _top_values(s, a_ref if p == 0 else b_ref, h)
    thr_cols, scale_cols = [], []
    for c0 in range(0, h2.shape[0], V7X_LANES):
        cols = slice(c0, c0 + V7X_LANES)
        vals = [a_ref[i, :, cols] + b_ref[j, :, cols] for (i, j) in CAND_PAIRS]
        for i, j, need_max, need_min in CAND_NETWORK:
            vals[i], vals[j] = (jnp.maximum(vals[i], vals[j]) if need_max else None,
                                jnp.minimum(vals[i], vals[j]) if need_min else None)
        thr_cols.append(0.5 * (vals[PEER_TOPK - 1] + vals[PEER_TOPK]))
        z = functools.reduce(jnp.add, [jnp.exp(v - vals[0]) for v in vals[:PEER_TOPK]])
        scale_cols.append(0.5 * (1.0 / z))
    thr = jnp.concatenate(thr_cols, axis=1)
    half_inv_z = jnp.concatenate(scale_cols, axis=1)
    a0 = a_ref[0]
    b0 = b_ref[0]
    for h in range(PEER_HEADS):
        s1 = sc_ref[2 * h]
        s2 = sc_ref[2 * h + 1]
        thr_ref[h] = thr[h:h + 1, :] - s1
        c_ref[h] = jnp.exp(s1 - a0[h:h + 1, :]) * half_inv_z[h:h + 1, :]
        s2_ref[h] = s2
        e2_ref[h] = jnp.exp(s2 - b0[h:h + 1, :])


def _peer_route(h2, wq_bf16, keys_bf16):
    bsz, seq, _ = h2.shape
    tb = TOK_TILE
    per_head = lambda: pl.BlockSpec((None, PEER_HEADS, N_KEYS, tb), lambda b, i: (b, 0, 0, i))
    head_shape = jax.ShapeDtypeStruct((bsz, PEER_HEADS, N_KEYS, seq), F32)
    qcols = PEER_HEADS * PEER_KEY_DIM
    return pl.pallas_call(
        _peer_route_kernel,
        grid=(bsz, seq // tb),
        in_specs=[pl.BlockSpec((None, tb, D_MODEL), lambda b, i: (b, i, 0)),
                  pl.BlockSpec((D_MODEL, qcols), lambda b, i: (0, 0)),
                  pl.BlockSpec((PEER_HEADS, 2, N_KEYS, PEER_HALF), lambda b, i: (0, 0, 0, 0))],
        out_specs=[pl.BlockSpec((None, None, D_MODEL, tb), lambda b, i: (b, i, 0, 0)),
                   per_head(), per_head(), per_head(), per_head()],
        out_shape=[jax.ShapeDtypeStruct((bsz, seq // tb, D_MODEL, tb), BF16),
                   head_shape, head_shape, head_shape, head_shape],
        scratch_shapes=[pltpu.VMEM((2 * PEER_HEADS, N_KEYS, tb), F32),
                        pltpu.VMEM((N_RANKS, PEER_HEADS, tb), F32),
                        pltpu.VMEM((N_RANKS, PEER_HEADS, tb), F32)],
        compiler_params=_params(("parallel", "parallel"),
                                D_MODEL * qcols * 2 + 5 * PEER_HEADS * N_KEYS * tb * 4 + tb * qcols * 4),
        name="peer_routing",
    )(h2, wq_bf16, keys_bf16)


def _gelu_x2(x):
    return x * (1.0 + lax.erf(x * (2.0 ** -0.5)))


def _routing_block(s2_ref, e2_ref, thr_ref, c_ref, row0, act_ref, wg_ref, col0, k0):
    cols = slice(col0, col0 + V7X_LANES)
    local = slice(col0 % PEER_PASS_TOK, col0 % PEER_PASS_TOK + V7X_LANES)
    for kk in range(0, KEY_CHUNK, KEY_BLOCK):
        keys = pl.ds(pl.multiple_of(k0 + kk, KEY_BLOCK), KEY_BLOCK)
        w = [None] * ROWS_PER_TILE
        for h in range(PEER_HEADS):
            s2 = s2_ref[h, keys, cols]
            e2 = e2_ref[h, keys, cols]
            for r in range(ROWS_PER_TILE):
                thr = thr_ref[h, row0 + r:row0 + r + 1, cols]
                coef = c_ref[h, row0 + r:row0 + r + 1, cols]
                term = jnp.where(s2 >= thr, e2 * coef, 0.0)
                w[r] = term if h == 0 else w[r] + term
        for r in range(ROWS_PER_TILE):
            rows = pl.ds(pl.multiple_of(r * N_KEYS + k0 + kk, KEY_BLOCK), KEY_BLOCK)
            wg_ref[rows, local] = (w[r] * _gelu_x2(act_ref[rows, local])).astype(BF16)


def _peer_dense_kernel(h2t_ref, u_ref, vt_ref, s2_ref, e2_ref, thr0_ref, c0_ref, thr1_ref, c1_ref, x1_ref, mod_ref,
                       y_ref, *scratch):
    g = pl.program_id(2)
    n_pass = PEER_TOK_TILE // PEER_PASS_TOK
    act_refs = [scratch[k * n_pass:(k + 1) * n_pass] for k in range(2)]
    wg_refs = [scratch[(2 + k) * n_pass:(3 + k) * n_pass] for k in range(2)]
    acc_refs = scratch[4 * n_pass:5 * n_pass]

    @pl.when(g == 0)
    def _():
        for ref in scratch:
            ref[...] = jnp.zeros_like(ref)

    for k in range(2):
        thr_ref, c_ref, row0 = (thr0_ref, c0_ref, ROWS_PER_TILE) if k == 0 else (thr1_ref, c1_ref, 0)
        tile = slice(k * EXPERT_TILE, (k + 1) * EXPERT_TILE)
        has_tile = (g > 0) if k == 0 else (g < pl.num_programs(2) - 1)
        for p in range(n_pass):
            def slab(c, carry, k=k, p=p, thr_ref=thr_ref, c_ref=c_ref, row0=row0, tile=tile, with_stage_b=True):
                a0 = pl.multiple_of(c * ACT_SLAB, ACT_SLAB)
                act_refs[k][p][pl.ds(a0, ACT_SLAB), :] = _dot(u_ref[pl.ds(k * EXPERT_TILE + a0, ACT_SLAB), :],
                                                             h2t_ref[p])
                rows = pl.ds(pl.multiple_of(c * OUT_SLAB, OUT_SLAB), OUT_SLAB)
                acc_refs[p][rows, :] += _dot(vt_ref[rows, tile], wg_refs[k][p][...])
                if with_stage_b:
                    k0 = pl.multiple_of(c * KEY_CHUNK, KEY_CHUNK)
                    for t in range(PEER_PASS_TOK // V7X_LANES):
                        _routing_block(s2_ref, e2_ref, thr_ref, c_ref, row0, act_refs[1 - k][p], wg_refs[1 - k][p],
                                       p * PEER_PASS_TOK + t * V7X_LANES, k0)
                return carry

            @pl.when(has_tile)
            def _(slab=slab):
                lax.fori_loop(0, CHUNKS_PER_TILE, slab, 0)

            @pl.when(jnp.logical_not(has_tile))
            def _(slab=slab):
                lax.fori_loop(0, CHUNKS_PER_TILE, functools.partial(slab, with_stage_b=False), 0)

    @pl.when(g == pl.num_programs(2) - 1)
    def _():
        gate = mod_ref[:, 5 * D_MODEL:6 * D_MODEL]
        for p in range(n_pass):
            rows = slice(p * PEER_PASS_TOK, (p + 1) * PEER_PASS_TOK)
            y_ref[rows, :] = x1_ref[rows, :] + gate * acc_refs[p][...].T


def _transposed_blocks(v):
    blk = 2 * EXPERT_TILE
    return jnp.transpose(v.reshape(v.shape[0] // blk, blk, v.shape[1]), (0, 2, 1))


def _peer_dense(h2t, u, vt, s2, e2, thr, coef, x1, mod):
    bsz, seq, _ = x1.shape
    tb = PEER_TOK_TILE
    n_pass = tb // PEER_PASS_TOK
    n_tiles = u.shape[0] // EXPERT_TILE
    n_pairs = n_tiles // 2
    once = pl.Buffered(1)
    per_head = lambda: pl.BlockSpec((None, PEER_HEADS, N_KEYS, tb), lambda b, i, g: (b, 0, 0, i), pipeline_mode=once)
    row_block = lambda f: pl.BlockSpec((None, PEER_HEADS, ROW_BLOCK, tb),
                                       lambda b, i, g: (b, 0, jnp.clip(f(g), 0, n_pairs - 1), i))
    pass_f32 = pltpu.VMEM((EXPERT_TILE, PEER_PASS_TOK), F32)
    pass_bf16 = pltpu.VMEM((EXPERT_TILE, PEER_PASS_TOK), BF16)
    return pl.pallas_call(
        _peer_dense_kernel,
        grid=(bsz, seq // tb, n_pairs + 1),
        in_specs=[pl.BlockSpec((None, n_pass, D_MODEL, PEER_PASS_TOK), lambda b, i, g: (b, i, 0, 0), pipeline_mode=once),
                  pl.BlockSpec((2 * EXPERT_TILE, D_MODEL), lambda b, i, g: (jnp.minimum(g, n_pairs - 1), 0)),
                  pl.BlockSpec((None, D_MODEL, 2 * EXPERT_TILE), lambda b, i, g: (jnp.maximum(g - 1, 0), 0, 0)),
                  per_head(), per_head(),
                  row_block(lambda g: g - 1), row_block(lambda g: g - 1),
                  row_block(lambda g: g), row_block(lambda g: g),
                  pl.BlockSpec((None, tb, D_MODEL), lambda b, i, g: (b, i, 0), pipeline_mode=once),
                  pl.BlockSpec((None, 1, 6 * D_MODEL), lambda b, i, g: (b, 0, 0))],
        out_specs=pl.BlockSpec((None, tb, D_MODEL), lambda b, i, g: (b, i, 0)),
        out_shape=jax.ShapeDtypeStruct((bsz, seq, D_MODEL), F32),
        scratch_shapes=([pass_f32] * (2 * n_pass) + [pass_bf16] * (2 * n_pass)
                        + [pltpu.VMEM((D_MODEL, PEER_PASS_TOK), F32)] * n_pass),
        compiler_params=_params(("parallel", "parallel", "arbitrary"),
                                2 * PEER_HEADS * N_KEYS * tb * 4 + 2 * 2 * EXPERT_TILE * D_MODEL * 4
                                + 3 * tb * D_MODEL * 4 + 4 * EXPERT_TILE * tb * 4),
        name="peer_dense_experts",
    )(h2t, u, vt, s2, e2, thr, coef, thr, coef, x1, mod)


def _layer(x, c, w_ada, b_ada, norm1_g, norm2_g, w_in, q_norm_g, k_norm_g, bias, sink,
           up_f, b_f, up_b, b_b, beta_attn, beta_gla, w_out, wq, keys, u, vt):
    mod = _modulation(c, w_ada, b_ada)
    qkv, gla_in, gates = _in_projection(x, mod, norm1_g, w_in)
    att = _windowed_attention(qkv, bias, sink, q_norm_g, k_norm_g, beta_attn)
    o_f, o_b = _gla(gla_in, gates, up_f, b_f, up_b, b_b)
    x1, h2 = _out_projection(att, o_f, o_b, gla_in, x, mod, beta_gla, w_out, norm2_g)
    h2t, s2, e2, thr, coef = _peer_route(h2, wq, keys)
    return _peer_dense(h2t, u, vt, s2, e2, thr, coef, x1, mod)


def kernel(x_prompt, x_sample, c_prompt, c_sample, w_ada, b_ada, norm1_g, norm2_g, w_in, q_norm_g, k_norm_g,
           rel_bias, sink, gk_up_fwd, gk_bias_fwd, gk_up_bwd, gk_bias_bwd, beta_attn, beta_gla, w_out,
           peer_query, peer_subkeys, peer_u, peer_v):
    depth = w_ada.shape[0]
    bias = _band_bias(rel_bias)
    y_prompt, y_sample = x_prompt, x_sample
    for l in range(depth):
        params = (w_ada[l], b_ada[l], norm1_g[l], norm2_g[l], w_in[l].astype(BF16), q_norm_g[l], k_norm_g[l],
                  bias, sink[l], gk_up_fwd[l], gk_bias_fwd[l], gk_up_bwd[l], gk_bias_bwd[l],
                  beta_attn[l], beta_gla[l], w_out[l].astype(BF16), peer_query[l].astype(BF16),
                  peer_subkeys[l].astype(BF16), peer_u[l], _transposed_blocks(peer_v[l]))
        y_prompt = _layer(y_prompt, c_prompt, *params)
        y_sample = _layer(y_sample, c_sample, *params)
    return (y_prompt, y_sample)
```

```python
import functools
import math

import numpy as np
import jax
import jax.numpy as jnp
from jax import lax
from jax.experimental import pallas as pl
from jax.experimental.pallas import tpu as pltpu

F32 = jnp.float32
BF16 = jnp.bfloat16

D_MODEL = 1024
ATTN_HEADS = 8
ATTN_KV_HEADS = 2
ATTN_HEAD_DIM = 64
ATTN_GROUP = ATTN_HEADS // ATTN_KV_HEADS
WINDOW = 128
BLOCK = 128
N_BUCKETS = 32
MAX_DISTANCE = 128
GLA_HEADS = 4
GLA_KEY_DIM = 64
GLA_VAL_DIM = 128
GLA_GATE_RANK = 16
GLA_GATE_NORM = 16.0
GLA_CHUNK = 64
PEER_HEADS = 8
N_KEYS = 128
PEER_KEY_DIM = 256
PEER_HALF = PEER_KEY_DIM // 2
PEER_TOPK = 16
EPS = 1e-6

ATTN_Q = ATTN_HEADS * ATTN_HEAD_DIM
ATTN_KV = ATTN_KV_HEADS * ATTN_HEAD_DIM
GLA_QK = GLA_HEADS * GLA_KEY_DIM
GLA_V = GLA_HEADS * GLA_VAL_DIM
ATTN_COLS = ATTN_Q + 2 * ATTN_KV
GLA_COLS = 2 * GLA_QK + 2 * GLA_V
GATE_COLS = 2 * GLA_GATE_RANK
D_IN = ATTN_COLS + GLA_COLS + GATE_COLS

V7X_LANES = 128
V7X_SUBLANES = 8
V7X_VMEM_LIMIT_CAP = 60 * 1024 * 1024

NEG_BIG = -1e30

TOK_TILE = 512
GLA_TILE = 512
PEER_TOK_TILE = 1024
PEER_PASS_TOK = TOK_TILE
EXPERT_TILE = 512
ROWS_PER_TILE = EXPERT_TILE // N_KEYS
ROW_BLOCK = 2 * ROWS_PER_TILE
KEY_CHUNK = 64
KEY_BLOCK = 32
CHUNKS_PER_TILE = N_KEYS // KEY_CHUNK
ACT_SLAB = EXPERT_TILE // CHUNKS_PER_TILE
OUT_SLAB = D_MODEL // CHUNKS_PER_TILE
N_RANKS = PEER_TOPK + 1
CAND_PAIRS = tuple((i, j) for i in range(N_RANKS) for j in range(N_RANKS) if (i + 1) * (j + 1) <= N_RANKS)


def _vmem_limit(block_bytes):
    return int(min(2 * block_bytes + (16 << 20), V7X_VMEM_LIMIT_CAP))


def _params(semantics, block_bytes):
    return pltpu.CompilerParams(dimension_semantics=semantics, vmem_limit_bytes=_vmem_limit(block_bytes))


def _nt_dot(a, b):
    return lax.dot_general(a, b, (((1,), (1,)), ((), ())), preferred_element_type=F32)


def _dot(a, b):
    return jnp.dot(a, b, preferred_element_type=F32)


def _rms(x, g):
    return x * lax.rsqrt(jnp.mean(x * x, axis=-1, keepdims=True) + EPS) * g


def _mod_kernel(c_ref, w_ref, b_ref, o_ref):
    c = c_ref[...]
    s = c * jax.nn.sigmoid(c)
    o_ref[...] = _dot(s.astype(BF16), w_ref[...].astype(BF16)) + b_ref[...]


def _modulation(c, w_ada, b_ada):
    bsz = c.shape[0]
    rows = -(-bsz // V7X_SUBLANES) * V7X_SUBLANES
    cp = jnp.zeros((rows, D_MODEL), F32).at[:bsz].set(c)
    n_out = w_ada.shape[1]
    tile = D_MODEL
    out = pl.pallas_call(
        _mod_kernel,
        grid=(n_out // tile,),
        in_specs=[pl.BlockSpec((rows, D_MODEL), lambda j: (0, 0)),
                  pl.BlockSpec((D_MODEL, tile), lambda j: (0, j)),
                  pl.BlockSpec((1, tile), lambda j: (0, j))],
        out_specs=pl.BlockSpec((rows, tile), lambda j: (0, j)),
        out_shape=jax.ShapeDtypeStruct((rows, n_out), F32),
        compiler_params=_params(("parallel",), D_MODEL * tile * 4),
        name="adaln_modulation",
    )(cp, w_ada, b_ada.reshape(1, n_out))
    return out[:bsz].reshape(bsz, 1, n_out)


def _inproj_kernel(x_ref, mod_ref, g_ref, w_ref, oa_ref, og_ref, or_ref):
    h = _rms(x_ref[...], g_ref[...])
    h = h * (1.0 + mod_ref[:, D_MODEL:2 * D_MODEL]) + mod_ref[:, 0:D_MODEL]
    p = _dot(h.astype(BF16), w_ref[...])
    oa_ref[...] = p[:, :ATTN_COLS]
    og_ref[...] = p[:, ATTN_COLS:ATTN_COLS + GLA_COLS]
    or_ref[...] = p[:, ATTN_COLS + GLA_COLS:]


def _in_projection(x, mod, norm_g, w_in_bf16):
    bsz, seq, _ = x.shape
    tb = TOK_TILE
    tok = lambda cols: pl.BlockSpec((None, tb, cols), lambda b, i: (b, i, 0))
    return pl.pallas_call(
        _inproj_kernel,
        grid=(bsz, seq // tb),
        in_specs=[tok(D_MODEL),
                  pl.BlockSpec((None, 1, 6 * D_MODEL), lambda b, i: (b, 0, 0)),
                  pl.BlockSpec((1, D_MODEL), lambda b, i: (0, 0)),
                  pl.BlockSpec((D_MODEL, D_IN), lambda b, i: (0, 0))],
        out_specs=[tok(ATTN_COLS), tok(GLA_COLS), tok(GATE_COLS)],
        out_shape=[jax.ShapeDtypeStruct((bsz, seq, ATTN_COLS), F32),
                   jax.ShapeDtypeStruct((bsz, seq, GLA_COLS), F32),
                   jax.ShapeDtypeStruct((bsz, seq, GATE_COLS), F32)],
        compiler_params=_params(("parallel", "parallel"),
                                tb * D_MODEL * 4 + D_MODEL * D_IN * 2 + 2 * tb * D_IN * 4),
        name="norm1_in_projection",
    )(x, mod, norm_g.reshape(1, D_MODEL), w_in_bf16)


def _t5_buckets(rel):
    nb = N_BUCKETS // 2
    ret = (rel > 0).astype(np.int32) * nb
    n = np.abs(rel)
    max_exact = nb // 2
    large = max_exact + (np.log(np.maximum(n, 1) / max_exact) / math.log(MAX_DISTANCE / max_exact)
                         * (nb - max_exact)).astype(np.int32)
    large = np.minimum(large, nb - 1)
    return (ret + np.where(n < max_exact, n, large)).astype(np.int32)


def _band_bias(rel_bias):
    span = 4 * BLOCK
    offsets = np.arange(span) - (2 * BLOCK - 1)
    per_offset = rel_bias[_t5_buckets(offsets)].astype(F32).T
    shifted = jnp.roll(per_offset, -(BLOCK - 1), axis=1)
    skew = jnp.tile(shifted, (1, BLOCK))[:, :BLOCK * (span - 1)].reshape(-1, BLOCK, span - 1)
    rel = np.arange(3 * BLOCK)[None, :] - BLOCK - np.arange(BLOCK)[:, None]
    band = jnp.asarray(np.abs(rel) <= WINDOW)
    return jnp.where(band[None], skew[:, :, :3 * BLOCK], NEG_BIG)


def _attn_kernel(sink_ref, q_ref, kvp_ref, kvc_ref, kvn_ref, bias_ref, qg_ref, kg_ref, beta_ref, o_ref):
    n = pl.program_id(1)
    last = pl.num_programs(1) - 1
    q = q_ref[...]
    kv = jnp.concatenate([kvp_ref[...], kvc_ref[...], kvn_ref[...]], axis=0)
    j = lax.broadcasted_iota(jnp.int32, (BLOCK, 3 * BLOCK), 1)
    valid = jnp.logical_and(jnp.logical_or(j >= BLOCK, n > 0), jnp.logical_or(j < 2 * BLOCK, n < last))
    scale = ATTN_HEAD_DIM ** -0.5
    outs = []
    for hk in range(ATTN_KV_HEADS):
        kh = _rms(kv[:, hk * ATTN_HEAD_DIM:(hk + 1) * ATTN_HEAD_DIM], kg_ref[...]).astype(BF16)
        vh = kv[:, ATTN_KV + hk * ATTN_HEAD_DIM:ATTN_KV + (hk + 1) * ATTN_HEAD_DIM].astype(BF16)
        for g in range(ATTN_GROUP):
            h = hk * ATTN_GROUP + g
            qh = _rms(q[:, h * ATTN_HEAD_DIM:(h + 1) * ATTN_HEAD_DIM], qg_ref[...]).astype(BF16)
            s = _nt_dot(qh, kh) * scale
            s = jnp.where(valid, s + bias_ref[h], NEG_BIG)
            sink = sink_ref[h]
            m = jnp.maximum(jnp.max(s, axis=-1, keepdims=True), sink)
            p = jnp.exp(s - m)
            den = jnp.sum(p, axis=-1, keepdims=True) + jnp.exp(sink - m)
            o = _dot(p.astype(BF16), vh) / den
            outs.append(_rms(o, beta_ref[:, h * ATTN_HEAD_DIM:(h + 1) * ATTN_HEAD_DIM]))
    o_ref[...] = jnp.concatenate(outs, axis=-1)


def _windowed_attention(qkv, bias, sink, q_norm_g, k_norm_g, beta_attn):
    bsz, seq, _ = qkv.shape
    nb = seq // BLOCK
    kv_cols = 2 * ATTN_KV
    kv_blk = ATTN_Q // kv_cols
    kv_spec = lambda f: pl.BlockSpec((None, BLOCK, kv_cols), lambda b, n: (b, f(n), kv_blk))
    const = lambda shape: pl.BlockSpec(shape, lambda b, n: tuple(0 for _ in shape))
    return pl.pallas_call(
        _attn_kernel,
        grid=(bsz, nb),
        in_specs=[pl.BlockSpec(memory_space=pltpu.SMEM),
                  pl.BlockSpec((None, BLOCK, ATTN_Q), lambda b, n: (b, n, 0)),
                  kv_spec(lambda n: jnp.maximum(n - 1, 0)),
                  kv_spec(lambda n: n),
                  kv_spec(lambda n: jnp.minimum(n + 1, nb - 1)),
                  const((ATTN_HEADS, BLOCK, 3 * BLOCK)),
                  const((1, ATTN_HEAD_DIM)), const((1, ATTN_HEAD_DIM)), const((1, ATTN_Q))],
        out_specs=pl.BlockSpec((None, BLOCK, ATTN_Q), lambda b, n: (b, n, 0)),
        out_shape=jax.ShapeDtypeStruct((bsz, seq, ATTN_Q), F32),
        compiler_params=_params(("parallel", "parallel"), ATTN_HEADS * BLOCK * 3 * BLOCK * 4 + 8 * BLOCK * ATTN_Q * 4),
        name="windowed_attention",
    )(sink, qkv, qkv, qkv, qkv, bias, q_norm_g.reshape(1, -1), k_norm_g.reshape(1, -1), beta_attn.reshape(1, -1))


def _split3(x):
    hi = x.astype(BF16)
    r = x - hi.astype(F32)
    mid = r.astype(BF16)
    lo = (r - mid.astype(F32)).astype(BF16)
    return hi, mid, lo


def _gla_direction(qk_ref, v_ref, gr, up_ref, gb_ref, tri_ref, o_ref, st_ref, b_ref, reverse):
    c = GLA_CHUNK
    nch = GLA_TILE // c
    z = _dot(gr.astype(BF16), up_ref[...]) + gb_ref[...]
    log_a = (jnp.minimum(z, 0.0) - jnp.log1p(jnp.exp(-jnp.abs(z)))) / GLA_GATE_NORM
    hi, mid, lo = _split3(log_a)
    tri = tri_ref[...]
    b_ref[...] = _dot(tri, hi) + _dot(tri, mid) + _dot(tri, lo)

    row = lax.broadcasted_iota(jnp.int32, (c, GLA_QK), 0)
    lane = lax.broadcasted_iota(jnp.int32, (c, GLA_QK), 1)
    key_in_head = lane % GLA_KEY_DIM
    causal = (key_in_head >= row) if reverse else (key_in_head <= row)
    qk_head = lane // GLA_KEY_DIM
    v_head = lax.broadcasted_iota(jnp.int32, (c, GLA_V), 1) // GLA_VAL_DIM
    st_rows = lax.broadcasted_iota(jnp.int32, (GLA_V, GLA_QK), 0) // GLA_VAL_DIM
    st_cols = lax.broadcasted_iota(jnp.int32, (GLA_V, GLA_QK), 1) // GLA_KEY_DIM
    same_head = st_rows == st_cols

    for ci in range(nch):
        blk = nch - 1 - ci if reverse else ci
        rows = slice(blk * c, (blk + 1) * c)
        b = b_ref[rows, :]
        b_last = b[0:1, :] if reverse else b[c - 1:c, :]
        q = qk_ref[rows, 0:GLA_QK]
        k = qk_ref[rows, GLA_QK:2 * GLA_QK]
        v = v_ref[rows, :]
        q_dec = (q * ((GLA_KEY_DIM ** -0.5) * jnp.exp(b))).astype(BF16)
        k_dec = k * jnp.exp(-b)
        k_end = (k * jnp.exp(b_last - b)).astype(BF16)
        decay = jnp.exp(b_last)
        k_heads = jnp.concatenate([jnp.where(qk_head == h, k_dec, 0.0) for h in range(GLA_HEADS)], axis=0)
        v_heads = jnp.concatenate([jnp.where(v_head == h, v, 0.0) for h in range(GLA_HEADS)], axis=0)
        a = jnp.where(causal, _nt_dot(q_dec, k_heads.astype(BF16)), 0.0)
        st = st_ref[...]
        o_ref[rows, :] = _dot(a.astype(BF16), v_heads.astype(BF16)) + _nt_dot(q_dec, st.astype(BF16))
        kv_t = lax.dot_general(v.astype(BF16), k_end, (((0,), (0,)), ((), ())), preferred_element_type=F32)
        st_ref[...] = st * decay + jnp.where(same_head, kv_t, 0.0)


def _gla_kernel(qkf_ref, vf_ref, grf_ref, qkb_ref, vb_ref, grb_ref, upf_ref, bf_ref, upb_ref, bb_ref,
                trif_ref, trib_ref, of_ref, ob_ref, stf_ref, stb_ref, cumf_ref, cumb_ref):
    @pl.when(pl.program_id(1) == 0)
    def _():
        stf_ref[...] = jnp.zeros_like(stf_ref)
        stb_ref[...] = jnp.zeros_like(stb_ref)

    _gla_direction(qkf_ref, vf_ref, grf_ref[:, 0:GLA_GATE_RANK], upf_ref, bf_ref, trif_ref,
                   of_ref, stf_ref, cumf_ref, False)
    _gla_direction(qkb_ref, vb_ref, grb_ref[:, GLA_GATE_RANK:2 * GLA_GATE_RANK], upb_ref, bb_ref, trib_ref,
                   ob_ref, stb_ref, cumb_ref, True)


def _gla(gla_in, gates, up_f, b_f, up_b, b_b):
    bsz, seq, _ = gla_in.shape
    tb = GLA_TILE
    nblk = seq // tb
    fwd = lambda cols, cb: pl.BlockSpec((None, tb, cols), lambda b, i: (b, i, cb))
    bwd = lambda cols, cb: pl.BlockSpec((None, tb, cols), lambda b, i: (b, nblk - 1 - i, cb))
    const = lambda shape: pl.BlockSpec(shape, lambda b, i: tuple(0 for _ in shape))
    chunk_of = np.arange(tb) // GLA_CHUNK
    same_chunk = chunk_of[:, None] == chunk_of[None, :]
    pos = np.arange(tb)
    tri_f = jnp.asarray(same_chunk & (pos[None, :] <= pos[:, None]), BF16)
    tri_b = jnp.asarray(same_chunk & (pos[None, :] >= pos[:, None]), BF16)
    state = pltpu.VMEM((GLA_V, GLA_QK), F32)
    cum = pltpu.VMEM((tb, GLA_QK), F32)
    return pl.pallas_call(
        _gla_kernel,
        grid=(bsz, nblk),
        in_specs=[fwd(2 * GLA_QK, 0), fwd(GLA_V, 1), fwd(GATE_COLS, 0),
                  bwd(2 * GLA_QK, 0), bwd(GLA_V, 1), bwd(GATE_COLS, 0),
                  const((GLA_GATE_RANK, GLA_QK)), const((1, GLA_QK)),
                  const((GLA_GATE_RANK, GLA_QK)), const((1, GLA_QK)),
                  const((tb, tb)), const((tb, tb))],
        out_specs=[pl.BlockSpec((None, tb, GLA_V), lambda b, i: (b, i, 0)),
                   pl.BlockSpec((None, tb, GLA_V), lambda b, i: (b, nblk - 1 - i, 0))],
        out_shape=[jax.ShapeDtypeStruct((bsz, seq, GLA_V), F32)] * 2,
        scratch_shapes=[state, state, cum, cum],
        compiler_params=_params(("parallel", "arbitrary"), 2 * tb * (2 * GLA_QK + 2 * GLA_V + V7X_LANES) * 4),
        name="gla_bidirectional",
    )(gla_in, gla_in, gates, gla_in, gla_in, gates,
      up_f.astype(BF16), b_f.reshape(1, -1), up_b.astype(BF16), b_b.reshape(1, -1), tri_f, tri_b)


def _outproj_kernel(att_ref, of_ref, ob_ref, og_ref, x_ref, mod_ref, beta_ref, w_ref, g2_ref, x1_ref, h2_ref):
    o = of_ref[...] + ob_ref[...]
    gate = og_ref[...]
    gate = gate * jax.nn.sigmoid(gate)
    parts = [att_ref[...].astype(BF16)]
    for h in range(GLA_HEADS):
        sl = slice(h * GLA_VAL_DIM, (h + 1) * GLA_VAL_DIM)
        parts.append((_rms(o[:, sl], beta_ref[:, sl]) * gate[:, sl]).astype(BF16))
    mix = _dot(jnp.concatenate(parts, axis=-1), w_ref[...])
    d = D_MODEL
    x1 = x_ref[...] + mod_ref[:, 2 * d:3 * d] * mix
    x1_ref[...] = x1
    h2 = _rms(x1, g2_ref[...]) * (1.0 + mod_ref[:, 4 * d:5 * d]) + mod_ref[:, 3 * d:4 * d]
    h2_ref[...] = h2.astype(BF16)


def _out_projection(att, o_f, o_b, gla_in, x, mod, beta_gla, w_out_bf16, norm2_g):
    bsz, seq, _ = x.shape
    tb = TOK_TILE
    tok = lambda cols, cb=0: pl.BlockSpec((None, tb, cols), lambda b, i: (b, i, cb))
    const = lambda shape: pl.BlockSpec(shape, lambda b, i: tuple(0 for _ in shape))
    return pl.pallas_call(
        _outproj_kernel,
        grid=(bsz, seq // tb),
        in_specs=[tok(ATTN_Q), tok(GLA_V), tok(GLA_V), tok(GLA_V, (2 * GLA_QK + GLA_V) // GLA_V), tok(D_MODEL),
                  pl.BlockSpec((None, 1, 6 * D_MODEL), lambda b, i: (b, 0, 0)),
                  const((1, GLA_V)), const((D_MODEL, D_MODEL)), const((1, D_MODEL))],
        out_specs=[tok(D_MODEL), tok(D_MODEL)],
        out_shape=[jax.ShapeDtypeStruct((bsz, seq, D_MODEL), F32),
                   jax.ShapeDtypeStruct((bsz, seq, D_MODEL), BF16)],
        compiler_params=_params(("parallel", "parallel"), tb * (4 * GLA_V + 3 * D_MODEL) * 4 + D_MODEL * D_MODEL * 2),
        name="out_projection_norm2",
    )(att, o_f, o_b, gla_in, x, mod, beta_gla.reshape(1, -1), w_out_bf16, norm2_g.reshape(1, -1))


def _sort_network(n):
    def merge(lo, hi, r):
        step = 2 * r
        if step < hi - lo:
            yield from merge(lo, hi, step)
            yield from merge(lo + r, hi, step)
            yield from ((i, i + r) for i in range(lo + r, hi - r, step))
        else:
            yield (lo, lo + r)

    def sort(lo, hi):
        if hi > lo:
            mid = lo + (hi - lo) // 2
            yield from sort(lo, mid)
            yield from sort(mid + 1, hi)
            yield from merge(lo, hi, 1)

    return tuple(sort(0, n - 1))


def _selection_network(n, keep):
    pairs = [(i, j) for i, j in _sort_network(1 << (n - 1).bit_length()) if j < n]
    needed = set(range(keep))
    net = []
    for i, j in reversed(pairs):
        if i in needed or j in needed:
            net.append((i, j, i in needed, j in needed))
            needed |= {i, j}
    return tuple(reversed(net))


CAND_NETWORK = _selection_network(len(CAND_PAIRS), N_RANKS)


def _merge_sorted(srt, shift):
    n = len(srt)
    other = [pltpu.roll(t, shift, 0) for t in srt]
    srt = [jnp.maximum(srt[i], other[n - 1 - i]) for i in range(n)]
    stride = n // 2
    while stride >= 1:
        for i in range(n):
            if i & stride == 0:
                srt[i], srt[i + stride] = (jnp.maximum(srt[i], srt[i + stride]),
                                           jnp.minimum(srt[i], srt[i + stride]))
        stride //= 2
    return srt


def _top_values(x, out_ref, head):
    n = N_KEYS // V7X_SUBLANES
    assert n == PEER_TOPK and V7X_SUBLANES == 8
    group = 4 * V7X_LANES
    sub = lax.broadcasted_iota(jnp.int32, (V7X_SUBLANES, V7X_LANES), 0)
    network = _sort_network(n)
    for g0 in range(0, x.shape[1], group):
        tiles = [[x[i * V7X_SUBLANES:(i + 1) * V7X_SUBLANES, g0 + c * V7X_LANES:g0 + (c + 1) * V7X_LANES]
                  for i in range(n)] for c in range(4)]
        merged = []
        for c in range(4):
            srt = list(tiles[c])
            for i, j in network:
                srt[i], srt[j] = jnp.maximum(srt[i], srt[j]), jnp.minimum(srt[i], srt[j])
            merged.append(_merge_sorted(srt, 4))
        packed = [[jnp.where(sub < 4, merged[2 * q][i], merged[2 * q + 1][i]) for i in range(n)] for q in range(2)]
        packed = [_merge_sorted(p, 2) for p in packed]
        keep = (sub & 2) != 0
        quad = [jnp.where(keep, packed[0][i], pltpu.roll(packed[1][i], 6, 0)) for i in range(n)]
        quad = _merge_sorted(quad, 1)
        for c, s in ((0, 3), (1, 7), (2, 1), (3, 5)):
            cols = slice(g0 + c * V7X_LANES, g0 + (c + 1) * V7X_LANES)
            for r in range(PEER_TOPK):
                out_ref[r, pl.ds(head, 1), cols] = quad[r][s:s + 1, :]
            last = quad[PEER_TOPK - 1][s:s + 1, :]
            below = [jnp.where(t < last, t, -jnp.inf) for t in tiles[c]]
            out_ref[PEER_TOPK, pl.ds(head, 1), cols] = jnp.max(functools.reduce(jnp.maximum, below), axis=0,
                                                               keepdims=True)


def _peer_route_kernel(h2_ref, wq_ref, keys_ref, h2t_ref, s2_ref, e2_ref, thr_ref, c_ref, sc_ref, a_ref, b_ref):
    h2 = h2_ref[...]
    h2t_ref[...] = h2.astype(F32).T.astype(BF16)
    q = _dot(h2, wq_ref[...]).astype(BF16)
    for h in range(PEER_HEADS):
        for p in range(2):
            blk = h * 2 + p
            s = _nt_dot(keys_ref[h, p], q[:, blk * PEER_HALF:(blk + 1) * PEER_HALF])
            sc_ref[blk] = s
            _top_values(s, a_ref if p == 0 else b_ref, h)
    thr_cols, scale_cols = [], []
    for c0 in range(0, h2.shape[0], V7X_LANES):
        cols = slice(c0, c0 + V7X_LANES)
        vals = [a_ref[i, :, cols] + b_ref[j, :, cols] for (i, j) in CAND_PAIRS]
        for i, j, need_max, need_min in CAND_NETWORK:
            vals[i], vals[j] = (jnp.maximum(vals[i], vals[j]) if need_max else None,
                                jnp.minimum(vals[i], vals[j]) if need_min else None)
        thr_cols.append(0.5 * (vals[PEER_TOPK - 1] + vals[PEER_TOPK]))
        z = functools.reduce(jnp.add, [jnp.exp(v - vals[0]) for v in vals[:PEER_TOPK]])
        scale_cols.append(0.5 * (1.0 / z))
    thr = jnp.concatenate(thr_cols, axis=1)
    half_inv_z = jnp.concatenate(scale_cols, axis=1)
    a0 = a_ref[0]
    b0 = b_ref[0]
    for h in range(PEER_HEADS):
        s1 = sc_ref[2 * h]
        s2 = sc_ref[2 * h + 1]
        thr_ref[h] = thr[h:h + 1, :] - s1
        c_ref[h] = jnp.exp(s1 - a0[h:h + 1, :]) * half_inv_z[h:h + 1, :]
        s2_ref[h] = s2
        e2_ref[h] = jnp.exp(s2 - b0[h:h + 1, :])


def _peer_route(h2, wq_bf16, keys_bf16):
    bsz, seq, _ = h2.shape
    tb = TOK_TILE
    per_head = lambda: pl.BlockSpec((None, PEER_HEADS, N_KEYS, tb), lambda b, i: (b, 0, 0, i))
    head_shape = jax.ShapeDtypeStruct((bsz, PEER_HEADS, N_KEYS, seq), F32)
    qcols = PEER_HEADS * PEER_KEY_DIM
    return pl.pallas_call(
        _peer_route_kernel,
        grid=(bsz, seq // tb),
        in_specs=[pl.BlockSpec((None, tb, D_MODEL), lambda b, i: (b, i, 0)),
                  pl.BlockSpec((D_MODEL, qcols), lambda b, i: (0, 0)),
                  pl.BlockSpec((PEER_HEADS, 2, N_KEYS, PEER_HALF), lambda b, i: (0, 0, 0, 0))],
        out_specs=[pl.BlockSpec((None, None, D_MODEL, tb), lambda b, i: (b, i, 0, 0)),
                   per_head(), per_head(), per_head(), per_head()],
        out_shape=[jax.ShapeDtypeStruct((bsz, seq // tb, D_MODEL, tb), BF16),
                   head_shape, head_shape, head_shape, head_shape],
        scratch_shapes=[pltpu.VMEM((2 * PEER_HEADS, N_KEYS, tb), F32),
                        pltpu.VMEM((N_RANKS, PEER_HEADS, tb), F32),
                        pltpu.VMEM((N_RANKS, PEER_HEADS, tb), F32)],
        compiler_params=_params(("parallel", "parallel"),
                                D_MODEL * qcols * 2 + 5 * PEER_HEADS * N_KEYS * tb * 4 + tb * qcols * 4),
        name="peer_routing",
    )(h2, wq_bf16, keys_bf16)


def _gelu_x2(x):
    return x * (1.0 + lax.erf(x * (2.0 ** -0.5)))


def _routing_block(s2_ref, e2_ref, thr_ref, c_ref, row0, act_ref, wg_ref, col0, k0):
    cols = slice(col0, col0 + V7X_LANES)
    local = slice(col0 % PEER_PASS_TOK, col0 % PEER_PASS_TOK + V7X_LANES)
    for kk in range(0, KEY_CHUNK, KEY_BLOCK):
        keys = pl.ds(pl.multiple_of(k0 + kk, KEY_BLOCK), KEY_BLOCK)
        w = [None] * ROWS_PER_TILE
        for h in range(PEER_HEADS):
            s2 = s2_ref[h, keys, cols]
            e2 = e2_ref[h, keys, cols]
            for r in range(ROWS_PER_TILE):
                thr = thr_ref[h, row0 + r:row0 + r + 1, cols]
                coef = c_ref[h, row0 + r:row0 + r + 1, cols]
                term = jnp.where(s2 >= thr, e2 * coef, 0.0)
                w[r] = term if h == 0 else w[r] + term
        for r in range(ROWS_PER_TILE):
            rows = pl.ds(pl.multiple_of(r * N_KEYS + k0 + kk, KEY_BLOCK), KEY_BLOCK)
            wg_ref[rows, local] = (w[r] * _gelu_x2(act_ref[rows, local])).astype(BF16)


def _peer_dense_kernel(h2t_ref, u_ref, vt_ref, s2_ref, e2_ref, thr0_ref, c0_ref, thr1_ref, c1_ref, x1_ref, mod_ref,
                       y_ref, *scratch):
    g = pl.program_id(2)
    n_pass = PEER_TOK_TILE // PEER_PASS_TOK
    act_refs = [scratch[k * n_pass:(k + 1) * n_pass] for k in range(2)]
    wg_refs = [scratch[(2 + k) * n_pass:(3 + k) * n_pass] for k in range(2)]
    acc_refs = scratch[4 * n_pass:5 * n_pass]

    @pl.when(g == 0)
    def _():
        for ref in scratch:
            ref[...] = jnp.zeros_like(ref)

    for k in range(2):
        thr_ref, c_ref, row0 = (thr0_ref, c0_ref, ROWS_PER_TILE) if k == 0 else (thr1_ref, c1_ref, 0)
        tile = slice(k * EXPERT_TILE, (k + 1) * EXPERT_TILE)
        has_tile = (g > 0) if k == 0 else (g < pl.num_programs(2) - 1)
        for p in range(n_pass):
            def slab(c, carry, k=k, p=p, thr_ref=thr_ref, c_ref=c_ref, row0=row0, tile=tile, with_stage_b=True):
                a0 = pl.multiple_of(c * ACT_SLAB, ACT_SLAB)
                act_refs[k][p][pl.ds(a0, ACT_SLAB), :] = _dot(u_ref[pl.ds(k * EXPERT_TILE + a0, ACT_SLAB), :],
                                                             h2t_ref[p])
                rows = pl.ds(pl.multiple_of(c * OUT_SLAB, OUT_SLAB), OUT_SLAB)
                acc_refs[p][rows, :] += _dot(vt_ref[rows, tile], wg_refs[k][p][...])
                if with_stage_b:
                    k0 = pl.multiple_of(c * KEY_CHUNK, KEY_CHUNK)
                    for t in range(PEER_PASS_TOK // V7X_LANES):
                        _routing_block(s2_ref, e2_ref, thr_ref, c_ref, row0, act_refs[1 - k][p], wg_refs[1 - k][p],
                                       p * PEER_PASS_TOK + t * V7X_LANES, k0)
                return carry

            @pl.when(has_tile)
            def _(slab=slab):
                lax.fori_loop(0, CHUNKS_PER_TILE, slab, 0)

            @pl.when(jnp.logical_not(has_tile))
            def _(slab=slab):
                lax.fori_loop(0, CHUNKS_PER_TILE, functools.partial(slab, with_stage_b=False), 0)

    @pl.when(g == pl.num_programs(2) - 1)
    def _():
        gate = mod_ref[:, 5 * D_MODEL:6 * D_MODEL]
        for p in range(n_pass):
            rows = slice(p * PEER_PASS_TOK, (p + 1) * PEER_PASS_TOK)
            y_ref[rows, :] = x1_ref[rows, :] + gate * acc_refs[p][...].T


def _transposed_blocks(v):
    blk = 2 * EXPERT_TILE
    return jnp.transpose(v.reshape(v.shape[0] // blk, blk, v.shape[1]), (0, 2, 1))


def _peer_dense(h2t, u, vt, s2, e2, thr, coef, x1, mod):
    bsz, seq, _ = x1.shape
    tb = PEER_TOK_TILE
    n_pass = tb // PEER_PASS_TOK
    n_tiles = u.shape[0] // EXPERT_TILE
    n_pairs = n_tiles // 2
    once = pl.Buffered(1)
    per_head = lambda: pl.BlockSpec((None, PEER_HEADS, N_KEYS, tb), lambda b, i, g: (b, 0, 0, i), pipeline_mode=once)
    row_block = lambda f: pl.BlockSpec((None, PEER_HEADS, ROW_BLOCK, tb),
                                       lambda b, i, g: (b, 0, jnp.clip(f(g), 0, n_pairs - 1), i))
    pass_f32 = pltpu.VMEM((EXPERT_TILE, PEER_PASS_TOK), F32)
    pass_bf16 = pltpu.VMEM((EXPERT_TILE, PEER_PASS_TOK), BF16)
    return pl.pallas_call(
        _peer_dense_kernel,
        grid=(bsz, seq // tb, n_pairs + 1),
        in_specs=[pl.BlockSpec((None, n_pass, D_MODEL, PEER_PASS_TOK), lambda b, i, g: (b, i, 0, 0), pipeline_mode=once),
                  pl.BlockSpec((2 * EXPERT_TILE, D_MODEL), lambda b, i, g: (jnp.minimum(g, n_pairs - 1), 0)),
                  pl.BlockSpec((None, D_MODEL, 2 * EXPERT_TILE), lambda b, i, g: (jnp.maximum(g - 1, 0), 0, 0)),
                  per_head(), per_head(),
                  row_block(lambda g: g - 1), row_block(lambda g: g - 1),
                  row_block(lambda g: g), row_block(lambda g: g),
                  pl.BlockSpec((None, tb, D_MODEL), lambda b, i, g: (b, i, 0), pipeline_mode=once),
                  pl.BlockSpec((None, 1, 6 * D_MODEL), lambda b, i, g: (b, 0, 0))],
        out_specs=pl.BlockSpec((None, tb, D_MODEL), lambda b, i, g: (b, i, 0)),
        out_shape=jax.ShapeDtypeStruct((bsz, seq, D_MODEL), F32),
        scratch_shapes=([pass_f32] * (2 * n_pass) + [pass_bf16] * (2 * n_pass)
                        + [pltpu.VMEM((D_MODEL, PEER_PASS_TOK), F32)] * n_pass),
        compiler_params=_params(("parallel", "parallel", "arbitrary"),
                                2 * PEER_HEADS * N_KEYS * tb * 4 + 2 * 2 * EXPERT_TILE * D_MODEL * 4
                                + 3 * tb * D_MODEL * 4 + 4 * EXPERT_TILE * tb * 4),
        name="peer_dense_experts",
    )(h2t, u, vt, s2, e2, thr, coef, thr, coef, x1, mod)


def _layer(x, c, w_ada, b_ada, norm1_g, norm2_g, w_in, q_norm_g, k_norm_g, bias, sink,
           up_f, b_f, up_b, b_b, beta_attn, beta_gla, w_out, wq, keys, u, vt):
    mod = _modulation(c, w_ada, b_ada)
    qkv, gla_in, gates = _in_projection(x, mod, norm1_g, w_in)
    att = _windowed_attention(qkv, bias, sink, q_norm_g, k_norm_g, beta_attn)
    o_f, o_b = _gla(gla_in, gates, up_f, b_f, up_b, b_b)
    x1, h2 = _out_projection(att, o_f, o_b, gla_in, x, mod, beta_gla, w_out, norm2_g)
    h2t, s2, e2, thr, coef = _peer_route(h2, wq, keys)
    return _peer_dense(h2t, u, vt, s2, e2, thr, coef, x1, mod)


def kernel(x_prompt, x_sample, c_prompt, c_sample, w_ada, b_ada, norm1_g, norm2_g, w_in, q_norm_g, k_norm_g,
           rel_bias, sink, gk_up_fwd, gk_bias_fwd, gk_up_bwd, gk_bias_bwd, beta_attn, beta_gla, w_out,
           peer_query, peer_subkeys, peer_u, peer_v):
    depth = w_ada.shape[0]
    bias = _band_bias(rel_bias)
    y_prompt, y_sample = x_prompt, x_sample
    for l in range(depth):
        params = (w_ada[l], b_ada[l], norm1_g[l], norm2_g[l], w_in[l].astype(BF16), q_norm_g[l], k_norm_g[l],
                  bias, sink[l], gk_up_fwd[l], gk_bias_fwd[l], gk_up_bwd[l], gk_bias_bwd[l],
                  beta_attn[l], beta_gla[l], w_out[l].astype(BF16), peer_query[l].astype(BF16),
                  peer_subkeys[l].astype(BF16), peer_u[l], _transposed_blocks(peer_v[l]))
        y_prompt = _layer(y_prompt, c_prompt, *params)
        y_sample = _layer(y_sample, c_sample, *params)
    return (y_prompt, y_sample)
```

```python
import functools
import math

import numpy as np
import jax
import jax.numpy as jnp
from jax import lax
from jax.experimental import pallas as pl
from jax.experimental.pallas import tpu as pltpu

F32 = jnp.float32
BF16 = jnp.bfloat16

D_MODEL = 1024
ATTN_HEADS = 8
ATTN_KV_HEADS = 2
ATTN_HEAD_DIM = 64
ATTN_GROUP = ATTN_HEADS // ATTN_KV_HEADS
WINDOW = 128
BLOCK = 128
N_BUCKETS = 32
MAX_DISTANCE = 128
GLA_HEADS = 4
GLA_KEY_DIM = 64
GLA_VAL_DIM = 128
GLA_GATE_RANK = 16
GLA_GATE_NORM = 16.0
GLA_CHUNK = 64
PEER_HEADS = 8
N_KEYS = 128
PEER_KEY_DIM = 256
PEER_HALF = PEER_KEY_DIM // 2
PEER_TOPK = 16
EPS = 1e-6

ATTN_Q = ATTN_HEADS * ATTN_HEAD_DIM
ATTN_KV = ATTN_KV_HEADS * ATTN_HEAD_DIM
GLA_QK = GLA_HEADS * GLA_KEY_DIM
GLA_V = GLA_HEADS * GLA_VAL_DIM
ATTN_COLS = ATTN_Q + 2 * ATTN_KV
GLA_COLS = 2 * GLA_QK + 2 * GLA_V
GATE_COLS = 2 * GLA_GATE_RANK
D_IN = ATTN_COLS + GLA_COLS + GATE_COLS

V7X_LANES = 128
V7X_SUBLANES = 8
V7X_VMEM_LIMIT_CAP = 60 * 1024 * 1024

NEG_BIG = -1e30

TOK_TILE = 512
GLA_TILE = 512
PEER_TOK_TILE = 1024
PEER_PASS_TOK = TOK_TILE
EXPERT_TILE = 512
ROWS_PER_TILE = EXPERT_TILE // N_KEYS
ROW_BLOCK = 2 * ROWS_PER_TILE
KEY_CHUNK = 64
KEY_BLOCK = 32
CHUNKS_PER_TILE = N_KEYS // KEY_CHUNK
ACT_SLAB = EXPERT_TILE // CHUNKS_PER_TILE
OUT_SLAB = D_MODEL // CHUNKS_PER_TILE
N_RANKS = PEER_TOPK + 1
CAND_PAIRS = tuple((i, j) for i in range(N_RANKS) for j in range(N_RANKS) if (i + 1) * (j + 1) <= N_RANKS)


def _vmem_limit(block_bytes):
    return int(min(2 * block_bytes + (16 << 20), V7X_VMEM_LIMIT_CAP))


def _params(semantics, block_bytes):
    return pltpu.CompilerParams(dimension_semantics=semantics, vmem_limit_bytes=_vmem_limit(block_bytes))


def _nt_dot(a, b):
    return lax.dot_general(a, b, (((1,), (1,)), ((), ())), preferred_element_type=F32)


def _dot(a, b):
    return jnp.dot(a, b, preferred_element_type=F32)


def _rms(x, g):
    return x * lax.rsqrt(jnp.mean(x * x, axis=-1, keepdims=True) + EPS) * g


def _mod_kernel(c_ref, w_ref, b_ref, o_ref):
    c = c_ref[...]
    s = c * jax.nn.sigmoid(c)
    o_ref[...] = _dot(s.astype(BF16), w_ref[...].astype(BF16)) + b_ref[...]


def _modulation(c, w_ada, b_ada):
    bsz = c.shape[0]
    rows = -(-bsz // V7X_SUBLANES) * V7X_SUBLANES
    cp = jnp.zeros((rows, D_MODEL), F32).at[:bsz].set(c)
    n_out = w_ada.shape[1]
    tile = D_MODEL
    out = pl.pallas_call(
        _mod_kernel,
        grid=(n_out // tile,),
        in_specs=[pl.BlockSpec((rows, D_MODEL), lambda j: (0, 0)),
                  pl.BlockSpec((D_MODEL, tile), lambda j: (0, j)),
                  pl.BlockSpec((1, tile), lambda j: (0, j))],
        out_specs=pl.BlockSpec((rows, tile), lambda j: (0, j)),
        out_shape=jax.ShapeDtypeStruct((rows, n_out), F32),
        compiler_params=_params(("parallel",), D_MODEL * tile * 4),
        name="adaln_modulation",
    )(cp, w_ada, b_ada.reshape(1, n_out))
    return out[:bsz].reshape(bsz, 1, n_out)


def _inproj_kernel(x_ref, mod_ref, g_ref, w_ref, oa_ref, og_ref, or_ref):
    h = _rms(x_ref[...], g_ref[...])
    h = h * (1.0 + mod_ref[:, D_MODEL:2 * D_MODEL]) + mod_ref[:, 0:D_MODEL]
    p = _dot(h.astype(BF16), w_ref[...])
    oa_ref[...] = p[:, :ATTN_COLS]
    og_ref[...] = p[:, ATTN_COLS:ATTN_COLS + GLA_COLS]
    or_ref[...] = p[:, ATTN_COLS + GLA_COLS:]


def _in_projection(x, mod, norm_g, w_in_bf16):
    bsz, seq, _ = x.shape
    tb = TOK_TILE
    tok = lambda cols: pl.BlockSpec((None, tb, cols), lambda b, i: (b, i, 0))
    return pl.pallas_call(
        _inproj_kernel,
        grid=(bsz, seq // tb),
        in_specs=[tok(D_MODEL),
                  pl.BlockSpec((None, 1, 6 * D_MODEL), lambda b, i: (b, 0, 0)),
                  pl.BlockSpec((1, D_MODEL), lambda b, i: (0, 0)),
                  pl.BlockSpec((D_MODEL, D_IN), lambda b, i: (0, 0))],
        out_specs=[tok(ATTN_COLS), tok(GLA_COLS), tok(GATE_COLS)],
        out_shape=[jax.ShapeDtypeStruct((bsz, seq, ATTN_COLS), F32),
                   jax.ShapeDtypeStruct((bsz, seq, GLA_COLS), F32),
                   jax.ShapeDtypeStruct((bsz, seq, GATE_COLS), F32)],
        compiler_params=_params(("parallel", "parallel"),
                                tb * D_MODEL * 4 + D_MODEL * D_IN * 2 + 2 * tb * D_IN * 4),
        name="norm1_in_projection",
    )(x, mod, norm_g.reshape(1, D_MODEL), w_in_bf16)


def _t5_buckets(rel):
    nb = N_BUCKETS // 2
    ret = (rel > 0).astype(np.int32) * nb
    n = np.abs(rel)
    max_exact = nb // 2
    large = max_exact + (np.log(np.maximum(n, 1) / max_exact) / math.log(MAX_DISTANCE / max_exact)
                         * (nb - max_exact)).astype(np.int32)
    large = np.minimum(large, nb - 1)
    return (ret + np.where(n < max_exact, n, large)).astype(np.int32)


def _band_bias(rel_bias):
    span = 4 * BLOCK
    offsets = np.arange(span) - (2 * BLOCK - 1)
    per_offset = rel_bias[_t5_buckets(offsets)].astype(F32).T
    shifted = jnp.roll(per_offset, -(BLOCK - 1), axis=1)
    skew = jnp.tile(shifted, (1, BLOCK))[:, :BLOCK * (span - 1)].reshape(-1, BLOCK, span - 1)
    rel = np.arange(3 * BLOCK)[None, :] - BLOCK - np.arange(BLOCK)[:, None]
    band = jnp.asarray(np.abs(rel) <= WINDOW)
    return jnp.where(band[None], skew[:, :, :3 * BLOCK], NEG_BIG)


def _attn_kernel(sink_ref, q_ref, kvp_ref, kvc_ref, kvn_ref, bias_ref, qg_ref, kg_ref, beta_ref, o_ref):
    n = pl.program_id(1)
    last = pl.num_programs(1) - 1
    q = q_ref[...]
    kv = jnp.concatenate([kvp_ref[...], kvc_ref[...], kvn_ref[...]], axis=0)
    j = lax.broadcasted_iota(jnp.int32, (BLOCK, 3 * BLOCK), 1)
    valid = jnp.logical_and(jnp.logical_or(j >= BLOCK, n > 0), jnp.logical_or(j < 2 * BLOCK, n < last))
    scale = ATTN_HEAD_DIM ** -0.5
    outs = []
    for hk in range(ATTN_KV_HEADS):
        kh = _rms(kv[:, hk * ATTN_HEAD_DIM:(hk + 1) * ATTN_HEAD_DIM], kg_ref[...]).astype(BF16)
        vh = kv[:, ATTN_KV + hk * ATTN_HEAD_DIM:ATTN_KV + (hk + 1) * ATTN_HEAD_DIM].astype(BF16)
        for g in range(ATTN_GROUP):
            h = hk * ATTN_GROUP + g
            qh = _rms(q[:, h * ATTN_HEAD_DIM:(h + 1) * ATTN_HEAD_DIM], qg_ref[...]).astype(BF16)
            s = _nt_dot(qh, kh) * scale
            s = jnp.where(valid, s + bias_ref[h], NEG_BIG)
            sink = sink_ref[h]
            m = jnp.maximum(jnp.max(s, axis=-1, keepdims=True), sink)
            p = jnp.exp(s - m)
            den = jnp.sum(p, axis=-1, keepdims=True) + jnp.exp(sink - m)
            o = _dot(p.astype(BF16), vh) / den
            outs.append(_rms(o, beta_ref[:, h * ATTN_HEAD_DIM:(h + 1) * ATTN_HEAD_DIM]))
    o_ref[...] = jnp.concatenate(outs, axis=-1)


def _windowed_attention(qkv, bias, sink, q_norm_g, k_norm_g, beta_attn):
    bsz, seq, _ = qkv.shape
    nb = seq // BLOCK
    kv_cols = 2 * ATTN_KV
    kv_blk = ATTN_Q // kv_cols
    kv_spec = lambda f: pl.BlockSpec((None, BLOCK, kv_cols), lambda b, n: (b, f(n), kv_blk))
    const = lambda shape: pl.BlockSpec(shape, lambda b, n: tuple(0 for _ in shape))
    return pl.pallas_call(
        _attn_kernel,
        grid=(bsz, nb),
        in_specs=[pl.BlockSpec(memory_space=pltpu.SMEM),
                  pl.BlockSpec((None, BLOCK, ATTN_Q), lambda b, n: (b, n, 0)),
                  kv_spec(lambda n: jnp.maximum(n - 1, 0)),
                  kv_spec(lambda n: n),
                  kv_spec(lambda n: jnp.minimum(n + 1, nb - 1)),
                  const((ATTN_HEADS, BLOCK, 3 * BLOCK)),
                  const((1, ATTN_HEAD_DIM)), const((1, ATTN_HEAD_DIM)), const((1, ATTN_Q))],
        out_specs=pl.BlockSpec((None, BLOCK, ATTN_Q), lambda b, n: (b, n, 0)),
        out_shape=jax.ShapeDtypeStruct((bsz, seq, ATTN_Q), F32),
        compiler_params=_params(("parallel", "parallel"), ATTN_HEADS * BLOCK * 3 * BLOCK * 4 + 8 * BLOCK * ATTN_Q * 4),
        name="windowed_attention",
    )(sink, qkv, qkv, qkv, qkv, bias, q_norm_g.reshape(1, -1), k_norm_g.reshape(1, -1), beta_attn.reshape(1, -1))


def _split3(x):
    hi = x.astype(BF16)
    r = x - hi.astype(F32)
    mid = r.astype(BF16)
    lo = (r - mid.astype(F32)).astype(BF16)
    return hi, mid, lo


def _gla_direction(qk_ref, v_ref, gr, up_ref, gb_ref, tri_ref, o_ref, st_ref, b_ref, reverse):
    c = GLA_CHUNK
    nch = GLA_TILE // c
    z = _dot(gr.astype(BF16), up_ref[...]) + gb_ref[...]
    log_a = (jnp.minimum(z, 0.0) - jnp.log1p(jnp.exp(-jnp.abs(z)))) / GLA_GATE_NORM
    hi, mid, lo = _split3(log_a)
    tri = tri_ref[...]
    b_ref[...] = _dot(tri, hi) + _dot(tri, mid) + _dot(tri, lo)

    row = lax.broadcasted_iota(jnp.int32, (c, GLA_QK), 0)
    lane = lax.broadcasted_iota(jnp.int32, (c, GLA_QK), 1)
    key_in_head = lane % GLA_KEY_DIM
    causal = (key_in_head >= row) if reverse else (key_in_head <= row)
    qk_head = lane // GLA_KEY_DIM
    v_head = lax.broadcasted_iota(jnp.int32, (c, GLA_V), 1) // GLA_VAL_DIM
    st_rows = lax.broadcasted_iota(jnp.int32, (GLA_V, GLA_QK), 0) // GLA_VAL_DIM
    st_cols = lax.broadcasted_iota(jnp.int32, (GLA_V, GLA_QK), 1) // GLA_KEY_DIM
    same_head = st_rows == st_cols

    for ci in range(nch):
        blk = nch - 1 - ci if reverse else ci
        rows = slice(blk * c, (blk + 1) * c)
        b = b_ref[rows, :]
        b_last = b[0:1, :] if reverse else b[c - 1:c, :]
        q = qk_ref[rows, 0:GLA_QK]
        k = qk_ref[rows, GLA_QK:2 * GLA_QK]
        v = v_ref[rows, :]
        q_dec = (q * ((GLA_KEY_DIM ** -0.5) * jnp.exp(b))).astype(BF16)
        k_dec = k * jnp.exp(-b)
        k_end = (k * jnp.exp(b_last - b)).astype(BF16)
        decay = jnp.exp(b_last)
        k_heads = jnp.concatenate([jnp.where(qk_head == h, k_dec, 0.0) for h in range(GLA_HEADS)], axis=0)
        v_heads = jnp.concatenate([jnp.where(v_head == h, v, 0.0) for h in range(GLA_HEADS)], axis=0)
        a = jnp.where(causal, _nt_dot(q_dec, k_heads.astype(BF16)), 0.0)
        st = st_ref[...]
        o_ref[rows, :] = _dot(a.astype(BF16), v_heads.astype(BF16)) + _nt_dot(q_dec, st.astype(BF16))
        kv_t = lax.dot_general(v.astype(BF16), k_end, (((0,), (0,)), ((), ())), preferred_element_type=F32)
        st_ref[...] = st * decay + jnp.where(same_head, kv_t, 0.0)


def _gla_kernel(qkf_ref, vf_ref, grf_ref, qkb_ref, vb_ref, grb_ref, upf_ref, bf_ref, upb_ref, bb_ref,
                trif_ref, trib_ref, of_ref, ob_ref, stf_ref, stb_ref, cumf_ref, cumb_ref):
    @pl.when(pl.program_id(1) == 0)
    def _():
        stf_ref[...] = jnp.zeros_like(stf_ref)
        stb_ref[...] = jnp.zeros_like(stb_ref)

    _gla_direction(qkf_ref, vf_ref, grf_ref[:, 0:GLA_GATE_RANK], upf_ref, bf_ref, trif_ref,
                   of_ref, stf_ref, cumf_ref, False)
    _gla_direction(qkb_ref, vb_ref, grb_ref[:, GLA_GATE_RANK:2 * GLA_GATE_RANK], upb_ref, bb_ref, trib_ref,
                   ob_ref, stb_ref, cumb_ref, True)


def _gla(gla_in, gates, up_f, b_f, up_b, b_b):
    bsz, seq, _ = gla_in.shape
    tb = GLA_TILE
    nblk = seq // tb
    fwd = lambda cols, cb: pl.BlockSpec((None, tb, cols), lambda b, i: (b, i, cb))
    bwd = lambda cols, cb: pl.BlockSpec((None, tb, cols), lambda b, i: (b, nblk - 1 - i, cb))
    const = lambda shape: pl.BlockSpec(shape, lambda b, i: tuple(0 for _ in shape))
    chunk_of = np.arange(tb) // GLA_CHUNK
    same_chunk = chunk_of[:, None] == chunk_of[None, :]
    pos = np.arange(tb)
    tri_f = jnp.asarray(same_chunk & (pos[None, :] <= pos[:, None]), BF16)
    tri_b = jnp.asarray(same_chunk & (pos[None, :] >= pos[:, None]), BF16)
    state = pltpu.VMEM((GLA_V, GLA_QK), F32)
    cum = pltpu.VMEM((tb, GLA_QK), F32)
    return pl.pallas_call(
        _gla_kernel,
        grid=(bsz, nblk),
        in_specs=[fwd(2 * GLA_QK, 0), fwd(GLA_V, 1), fwd(GATE_COLS, 0),
                  bwd(2 * GLA_QK, 0), bwd(GLA_V, 1), bwd(GATE_COLS, 0),
                  const((GLA_GATE_RANK, GLA_QK)), const((1, GLA_QK)),
                  const((GLA_GATE_RANK, GLA_QK)), const((1, GLA_QK)),
                  const((tb, tb)), const((tb, tb))],
        out_specs=[pl.BlockSpec((None, tb, GLA_V), lambda b, i: (b, i, 0)),
                   pl.BlockSpec((None, tb, GLA_V), lambda b, i: (b, nblk - 1 - i, 0))],
        out_shape=[jax.ShapeDtypeStruct((bsz, seq, GLA_V), F32)] * 2,
        scratch_shapes=[state, state, cum, cum],
        compiler_params=_params(("parallel", "arbitrary"), 2 * tb * (2 * GLA_QK + 2 * GLA_V + V7X_LANES) * 4),
        name="gla_bidirectional",
    )(gla_in, gla_in, gates, gla_in, gla_in, gates,
      up_f.astype(BF16), b_f.reshape(1, -1), up_b.astype(BF16), b_b.reshape(1, -1), tri_f, tri_b)


def _outproj_kernel(att_ref, of_ref, ob_ref, og_ref, x_ref, mod_ref, beta_ref, w_ref, g2_ref, x1_ref, h2_ref):
    o = of_ref[...] + ob_ref[...]
    gate = og_ref[...]
    gate = gate * jax.nn.sigmoid(gate)
    parts = [att_ref[...].astype(BF16)]
    for h in range(GLA_HEADS):
        sl = slice(h * GLA_VAL_DIM, (h + 1) * GLA_VAL_DIM)
        parts.append((_rms(o[:, sl], beta_ref[:, sl]) * gate[:, sl]).astype(BF16))
    mix = _dot(jnp.concatenate(parts, axis=-1), w_ref[...])
    d = D_MODEL
    x1 = x_ref[...] + mod_ref[:, 2 * d:3 * d] * mix
    x1_ref[...] = x1
    h2 = _rms(x1, g2_ref[...]) * (1.0 + mod_ref[:, 4 * d:5 * d]) + mod_ref[:, 3 * d:4 * d]
    h2_ref[...] = h2.astype(BF16)


def _out_projection(att, o_f, o_b, gla_in, x, mod, beta_gla, w_out_bf16, norm2_g):
    bsz, seq, _ = x.shape
    tb = TOK_TILE
    tok = lambda cols, cb=0: pl.BlockSpec((None, tb, cols), lambda b, i: (b, i, cb))
    const = lambda shape: pl.BlockSpec(shape, lambda b, i: tuple(0 for _ in shape))
    return pl.pallas_call(
        _outproj_kernel,
        grid=(bsz, seq // tb),
        in_specs=[tok(ATTN_Q), tok(GLA_V), tok(GLA_V), tok(GLA_V, (2 * GLA_QK + GLA_V) // GLA_V), tok(D_MODEL),
                  pl.BlockSpec((None, 1, 6 * D_MODEL), lambda b, i: (b, 0, 0)),
                  const((1, GLA_V)), const((D_MODEL, D_MODEL)), const((1, D_MODEL))],
        out_specs=[tok(D_MODEL), tok(D_MODEL)],
        out_shape=[jax.ShapeDtypeStruct((bsz, seq, D_MODEL), F32),
                   jax.ShapeDtypeStruct((bsz, seq, D_MODEL), BF16)],
        compiler_params=_params(("parallel", "parallel"), tb * (4 * GLA_V + 3 * D_MODEL) * 4 + D_MODEL * D_MODEL * 2),
        name="out_projection_norm2",
    )(att, o_f, o_b, gla_in, x, mod, beta_gla.reshape(1, -1), w_out_bf16, norm2_g.reshape(1, -1))


def _sort_network(n):
    def merge(lo, hi, r):
        step = 2 * r
        if step < hi - lo:
            yield from merge(lo, hi, step)
            yield from merge(lo + r, hi, step)
            yield from ((i, i + r) for i in range(lo + r, hi - r, step))
        else:
            yield (lo, lo + r)

    def sort(lo, hi):
        if hi > lo:
            mid = lo + (hi - lo) // 2
            yield from sort(lo, mid)
            yield from sort(mid + 1, hi)
            yield from merge(lo, hi, 1)

    return tuple(sort(0, n - 1))


def _selection_network(n, keep):
    pairs = [(i, j) for i, j in _sort_network(1 << (n - 1).bit_length()) if j < n]
    needed = set(range(keep))
    net = []
    for i, j in reversed(pairs):
        if i in needed or j in needed:
            net.append((i, j, i in needed, j in needed))
            needed |= {i, j}
    return tuple(reversed(net))


CAND_NETWORK = _selection_network(len(CAND_PAIRS), N_RANKS)


def _merge_sorted(srt, shift):
    n = len(srt)
    other = [pltpu.roll(t, shift, 0) for t in srt]
    srt = [jnp.maximum(srt[i], other[n - 1 - i]) for i in range(n)]
    stride = n // 2
    while stride >= 1:
        for i in range(n):
            if i & stride == 0:
                srt[i], srt[i + stride] = (jnp.maximum(srt[i], srt[i + stride]),
                                           jnp.minimum(srt[i], srt[i + stride]))
        stride //= 2
    return srt


def _top_values(x, out_ref, head):
    n = N_KEYS // V7X_SUBLANES
    assert n == PEER_TOPK and V7X_SUBLANES == 8
    group = 4 * V7X_LANES
    sub = lax.broadcasted_iota(jnp.int32, (V7X_SUBLANES, V7X_LANES), 0)
    network = _sort_network(n)
    for g0 in range(0, x.shape[1], group):
        tiles = [[x[i * V7X_SUBLANES:(i + 1) * V7X_SUBLANES, g0 + c * V7X_LANES:g0 + (c + 1) * V7X_LANES]
                  for i in range(n)] for c in range(4)]
        merged = []
        for c in range(4):
            srt = list(tiles[c])
            for i, j in network:
                srt[i], srt[j] = jnp.maximum(srt[i], srt[j]), jnp.minimum(srt[i], srt[j])
            merged.append(_merge_sorted(srt, 4))
        packed = [[jnp.where(sub < 4, merged[2 * q][i], merged[2 * q + 1][i]) for i in range(n)] for q in range(2)]
        packed = [_merge_sorted(p, 2) for p in packed]
        keep = (sub & 2) != 0
        quad = [jnp.where(keep, packed[0][i], pltpu.roll(packed[1][i], 6, 0)) for i in range(n)]
        quad = _merge_sorted(quad, 1)
        for c, s in ((0, 3), (1, 7), (2, 1), (3, 5)):
            cols = slice(g0 + c * V7X_LANES, g0 + (c + 1) * V7X_LANES)
            for r in range(PEER_TOPK):
                out_ref[r, pl.ds(head, 1), cols] = quad[r][s:s + 1, :]
            last = quad[PEER_TOPK - 1][s:s + 1, :]
            below = [jnp.where(t < last, t, -jnp.inf) for t in tiles[c]]
            out_ref[PEER_TOPK, pl.ds(head, 1), cols] = jnp.max(functools.reduce(jnp.maximum, below), axis=0,
                                                               keepdims=True)


def _peer_route_kernel(h2_next_ref, h2_ref, wq_ref, keys_ref, h2t_ref, s2_ref, e2_ref, thr_ref, c_ref,
                       q0_ref, q1_ref, sc_ref, a_ref, b_ref):
    @pl.when(pl.program_id(0) == 0)
    def _():
        q1_ref[...] = jnp.zeros_like(q1_ref)

    for parity, (q_new_ref, q_ref) in enumerate(((q0_ref, q1_ref), (q1_ref, q0_ref))):
        @pl.when(pl.program_id(0) % 2 == parity)
        def _(q_new_ref=q_new_ref, q_ref=q_ref):
            def project(h, q_new_ref=q_new_ref):
                cols = slice(h * 2 * PEER_HALF, (h + 1) * 2 * PEER_HALF)
                q_new_ref[:, cols] = _dot(h2_next_ref[...], wq_ref[:, cols]).astype(BF16)

            _route_tile(project, h2_ref, q_ref, keys_ref, h2t_ref, s2_ref, e2_ref, thr_ref, c_ref, sc_ref, a_ref, b_ref)


def _route_tile(project, h2_ref, q_ref, keys_ref, h2t_ref, s2_ref, e2_ref, thr_ref, c_ref, sc_ref, a_ref, b_ref):
    h2 = h2_ref[...]
    h2t_ref[...] = h2.astype(F32).T.astype(BF16)
    for h in range(PEER_HEADS):
        project(h)
        for p in range(2):
            blk = h * 2 + p
            s = _nt_dot(keys_ref[h, p], q_ref[:, blk * PEER_HALF:(blk + 1) * PEER_HALF])
            sc_ref[blk] = s
            _top_values(s, a_ref if p == 0 else b_ref, h)
    thr_cols, scale_cols = [], []
    for c0 in range(0, h2.shape[0], V7X_LANES):
        cols = slice(c0, c0 + V7X_LANES)
        vals = [a_ref[i, :, cols] + b_ref[j, :, cols] for (i, j) in CAND_PAIRS]
        for i, j, need_max, need_min in CAND_NETWORK:
            vals[i], vals[j] = (jnp.maximum(vals[i], vals[j]) if need_max else None,
                                jnp.minimum(vals[i], vals[j]) if need_min else None)
        thr_cols.append(0.5 * (vals[PEER_TOPK - 1] + vals[PEER_TOPK]))
        z = functools.reduce(jnp.add, [jnp.exp(v - vals[0]) for v in vals[:PEER_TOPK]])
        scale_cols.append(0.5 * (1.0 / z))
    thr = jnp.concatenate(thr_cols, axis=1)
    half_inv_z = jnp.concatenate(scale_cols, axis=1)
    a0 = a_ref[0]
    b0 = b_ref[0]
    for h in range(PEER_HEADS):
        s1 = sc_ref[2 * h]
        s2 = sc_ref[2 * h + 1]
        thr_ref[h] = thr[h:h + 1, :] - s1
        c_ref[h] = jnp.exp(s1 - a0[h:h + 1, :]) * half_inv_z[h:h + 1, :]
        s2_ref[h] = s2
        e2_ref[h] = jnp.exp(s2 - b0[h:h + 1, :])


def _peer_route(h2, wq_bf16, keys_bf16):
    bsz, seq, _ = h2.shape
    tb = TOK_TILE
    per_row = seq // tb
    n_tiles = bsz * per_row
    routed = lambda i: jnp.maximum(i - 1, 0)
    per_head = lambda: pl.BlockSpec((None, PEER_HEADS, N_KEYS, tb),
                                    lambda i: (routed(i) // per_row, 0, 0, routed(i) % per_row))
    head_shape = jax.ShapeDtypeStruct((bsz, PEER_HEADS, N_KEYS, seq), F32)
    qcols = PEER_HEADS * PEER_KEY_DIM
    h2_tiles = h2.reshape(n_tiles, tb, D_MODEL)
    h2t, *routing = pl.pallas_call(
        _peer_route_kernel,
        grid=(n_tiles + 1,),
        in_specs=[pl.BlockSpec((None, tb, D_MODEL), lambda i: (jnp.minimum(i, n_tiles - 1), 0, 0)),
                  pl.BlockSpec((None, tb, D_MODEL), lambda i: (routed(i), 0, 0)),
                  pl.BlockSpec((D_MODEL, qcols), lambda i: (0, 0)),
                  pl.BlockSpec((PEER_HEADS, 2, N_KEYS, PEER_HALF), lambda i: (0, 0, 0, 0))],
        out_specs=[pl.BlockSpec((None, D_MODEL, tb), lambda i: (routed(i), 0, 0)),
                   per_head(), per_head(), per_head(), per_head()],
        out_shape=[jax.ShapeDtypeStruct((n_tiles, D_MODEL, tb), BF16),
                   head_shape, head_shape, head_shape, head_shape],
        scratch_shapes=[pltpu.VMEM((tb, qcols), BF16),
                        pltpu.VMEM((tb, qcols), BF16),
                        pltpu.VMEM((2 * PEER_HEADS, N_KEYS, tb), F32),
                        pltpu.VMEM((N_RANKS, PEER_HEADS, tb), F32),
                        pltpu.VMEM((N_RANKS, PEER_HEADS, tb), F32)],
        compiler_params=_params(("arbitrary",),
                                D_MODEL * qcols * 2 + 5 * PEER_HEADS * N_KEYS * tb * 4 + tb * qcols * 4),
        name="peer_routing",
    )(h2_tiles, h2_tiles, wq_bf16, keys_bf16)
    return (h2t.reshape(bsz, per_row, D_MODEL, tb), *routing)


def _gelu_x2(x):
    return x * (1.0 + lax.erf(x * (2.0 ** -0.5)))


def _routing_block(s2_ref, e2_ref, thr_ref, c_ref, row0, act_ref, wg_ref, col0, k0):
    cols = slice(col0, col0 + V7X_LANES)
    local = slice(col0 % PEER_PASS_TOK, col0 % PEER_PASS_TOK + V7X_LANES)
    for kk in range(0, KEY_CHUNK, KEY_BLOCK):
        keys = pl.ds(pl.multiple_of(k0 + kk, KEY_BLOCK), KEY_BLOCK)
        w = [None] * ROWS_PER_TILE
        for h in range(PEER_HEADS):
            s2 = s2_ref[h, keys, cols]
            e2 = e2_ref[h, keys, cols]
            for r in range(ROWS_PER_TILE):
                thr = thr_ref[h, row0 + r:row0 + r + 1, cols]
                coef = c_ref[h, row0 + r:row0 + r + 1, cols]
                term = jnp.where(s2 >= thr, e2 * coef, 0.0)
                w[r] = term if h == 0 else w[r] + term
        for r in range(ROWS_PER_TILE):
            rows = pl.ds(pl.multiple_of(r * N_KEYS + k0 + kk, KEY_BLOCK), KEY_BLOCK)
            wg_ref[rows, local] = (w[r] * _gelu_x2(act_ref[rows, local])).astype(BF16)


def _peer_dense_kernel(h2t_ref, u_ref, vt_ref, s2_ref, e2_ref, thr0_ref, c0_ref, thr1_ref, c1_ref, x1_ref, mod_ref,
                       y_ref, *scratch):
    g = pl.program_id(2)
    n_pass = PEER_TOK_TILE // PEER_PASS_TOK
    act_refs = [scratch[k * n_pass:(k + 1) * n_pass] for k in range(2)]
    wg_refs = [scratch[(2 + k) * n_pass:(3 + k) * n_pass] for k in range(2)]
    acc_refs = scratch[4 * n_pass:5 * n_pass]

    @pl.when(g == 0)
    def _():
        for ref in scratch:
            ref[...] = jnp.zeros_like(ref)

    for k in range(2):
        thr_ref, c_ref, row0 = (thr0_ref, c0_ref, ROWS_PER_TILE) if k == 0 else (thr1_ref, c1_ref, 0)
        tile = slice(k * EXPERT_TILE, (k + 1) * EXPERT_TILE)
        has_tile = (g > 0) if k == 0 else (g < pl.num_programs(2) - 1)
        for p in range(n_pass):
            def slab(c, carry, k=k, p=p, thr_ref=thr_ref, c_ref=c_ref, row0=row0, tile=tile, with_stage_b=True):
                a0 = pl.multiple_of(c * ACT_SLAB, ACT_SLAB)
                act_refs[k][p][pl.ds(a0, ACT_SLAB), :] = _dot(u_ref[pl.ds(k * EXPERT_TILE + a0, ACT_SLAB), :],
                                                             h2t_ref[p])
                rows = pl.ds(pl.multiple_of(c * OUT_SLAB, OUT_SLAB), OUT_SLAB)
                acc_refs[p][rows, :] += _dot(vt_ref[rows, tile], wg_refs[k][p][...])
                if with_stage_b:
                    k0 = pl.multiple_of(c * KEY_CHUNK, KEY_CHUNK)
                    for t in range(PEER_PASS_TOK // V7X_LANES):
                        _routing_block(s2_ref, e2_ref, thr_ref, c_ref, row0, act_refs[1 - k][p], wg_refs[1 - k][p],
                                       p * PEER_PASS_TOK + t * V7X_LANES, k0)
                return carry

            @pl.when(has_tile)
            def _(slab=slab):
                lax.fori_loop(0, CHUNKS_PER_TILE, slab, 0)

            @pl.when(jnp.logical_not(has_tile))
            def _(slab=slab):
                lax.fori_loop(0, CHUNKS_PER_TILE, functools.partial(slab, with_stage_b=False), 0)

    @pl.when(g == pl.num_programs(2) - 1)
    def _():
        gate = mod_ref[:, 5 * D_MODEL:6 * D_MODEL]
        for p in range(n_pass):
            rows = slice(p * PEER_PASS_TOK, (p + 1) * PEER_PASS_TOK)
            y_ref[rows, :] = x1_ref[rows, :] + gate * acc_refs[p][...].T


def _transposed_blocks(v):
    blk = 2 * EXPERT_TILE
    return jnp.transpose(v.reshape(v.shape[0] // blk, blk, v.shape[1]), (0, 2, 1))


def _peer_dense(h2t, u, vt, s2, e2, thr, coef, x1, mod):
    bsz, seq, _ = x1.shape
    tb = PEER_TOK_TILE
    n_pass = tb // PEER_PASS_TOK
    n_tiles = u.shape[0] // EXPERT_TILE
    n_pairs = n_tiles // 2
    once = pl.Buffered(1)
    per_head = lambda: pl.BlockSpec((None, PEER_HEADS, N_KEYS, tb), lambda b, i, g: (b, 0, 0, i), pipeline_mode=once)
    row_block = lambda f: pl.BlockSpec((None, PEER_HEADS, ROW_BLOCK, tb),
                                       lambda b, i, g: (b, 0, jnp.clip(f(g), 0, n_pairs - 1), i))
    pass_f32 = pltpu.VMEM((EXPERT_TILE, PEER_PASS_TOK), F32)
    pass_bf16 = pltpu.VMEM((EXPERT_TILE, PEER_PASS_TOK), BF16)
    return pl.pallas_call(
        _peer_dense_kernel,
        grid=(bsz, seq // tb, n_pairs + 1),
        in_specs=[pl.BlockSpec((None, n_pass, D_MODEL, PEER_PASS_TOK), lambda b, i, g: (b, i, 0, 0), pipeline_mode=once),
                  pl.BlockSpec((2 * EXPERT_TILE, D_MODEL), lambda b, i, g: (jnp.minimum(g, n_pairs - 1), 0)),
                  pl.BlockSpec((None, D_MODEL, 2 * EXPERT_TILE), lambda b, i, g: (jnp.maximum(g - 1, 0), 0, 0)),
                  per_head(), per_head(),
                  row_block(lambda g: g - 1), row_block(lambda g: g - 1),
                  row_block(lambda g: g), row_block(lambda g: g),
                  pl.BlockSpec((None, tb, D_MODEL), lambda b, i, g: (b, i, 0), pipeline_mode=once),
                  pl.BlockSpec((None, 1, 6 * D_MODEL), lambda b, i, g: (b, 0, 0))],
        out_specs=pl.BlockSpec((None, tb, D_MODEL), lambda b, i, g: (b, i, 0)),
        out_shape=jax.ShapeDtypeStruct((bsz, seq, D_MODEL), F32),
        scratch_shapes=([pass_f32] * (2 * n_pass) + [pass_bf16] * (2 * n_pass)
                        + [pltpu.VMEM((D_MODEL, PEER_PASS_TOK), F32)] * n_pass),
        compiler_params=_params(("parallel", "parallel", "arbitrary"),
                                2 * PEER_HEADS * N_KEYS * tb * 4 + 2 * 2 * EXPERT_TILE * D_MODEL * 4
                                + 3 * tb * D_MODEL * 4 + 4 * EXPERT_TILE * tb * 4),
        name="peer_dense_experts",
    )(h2t, u, vt, s2, e2, thr, coef, thr, coef, x1, mod)


def _layer(x, c, w_ada, b_ada, norm1_g, norm2_g, w_in, q_norm_g, k_norm_g, bias, sink,
           up_f, b_f, up_b, b_b, beta_attn, beta_gla, w_out, wq, keys, u, vt):
    mod = _modulation(c, w_ada, b_ada)
    qkv, gla_in, gates = _in_projection(x, mod, norm1_g, w_in)
    att = _windowed_attention(qkv, bias, sink, q_norm_g, k_norm_g, beta_attn)
    o_f, o_b = _gla(gla_in, gates, up_f, b_f, up_b, b_b)
    x1, h2 = _out_projection(att, o_f, o_b, gla_in, x, mod, beta_gla, w_out, norm2_g)
    h2t, s2, e2, thr, coef = _peer_route(h2, wq, keys)
    return _peer_dense(h2t, u, vt, s2, e2, thr, coef, x1, mod)


def kernel(x_prompt, x_sample, c_prompt, c_sample, w_ada, b_ada, norm1_g, norm2_g, w_in, q_norm_g, k_norm_g,
           rel_bias, sink, gk_up_fwd, gk_bias_fwd, gk_up_bwd, gk_bias_bwd, beta_attn, beta_gla, w_out,
           peer_query, peer_subkeys, peer_u, peer_v):
    depth = w_ada.shape[0]
    bias = _band_bias(rel_bias)
    y_prompt, y_sample = x_prompt, x_sample
    for l in range(depth):
        params = (w_ada[l], b_ada[l], norm1_g[l], norm2_g[l], w_in[l].astype(BF16), q_norm_g[l], k_norm_g[l],
                  bias, sink[l], gk_up_fwd[l], gk_bias_fwd[l], gk_up_bwd[l], gk_bias_bwd[l],
                  beta_attn[l], beta_gla[l], w_out[l].astype(BF16), peer_query[l].astype(BF16),
                  peer_subkeys[l].astype(BF16), peer_u[l], _transposed_blocks(peer_v[l]))
        y_prompt = _layer(y_prompt, c_prompt, *params)
        y_sample = _layer(y_sample, c_sample, *params)
    return (y_prompt, y_sample)
```

```python
import functools
import math

import numpy as np
import jax
import jax.numpy as jnp
from jax import lax
from jax.experimental import pallas as pl
from jax.experimental.pallas import tpu as pltpu

F32 = jnp.float32
BF16 = jnp.bfloat16

D_MODEL = 1024
ATTN_HEADS = 8
ATTN_KV_HEADS = 2
ATTN_HEAD_DIM = 64
ATTN_GROUP = ATTN_HEADS // ATTN_KV_HEADS
WINDOW = 128
BLOCK = 128
N_BUCKETS = 32
MAX_DISTANCE = 128
GLA_HEADS = 4
GLA_KEY_DIM = 64
GLA_VAL_DIM = 128
GLA_GATE_RANK = 16
GLA_GATE_NORM = 16.0
GLA_CHUNK = 64
PEER_HEADS = 8
N_KEYS = 128
PEER_KEY_DIM = 256
PEER_HALF = PEER_KEY_DIM // 2
PEER_TOPK = 16
EPS = 1e-6

ATTN_Q = ATTN_HEADS * ATTN_HEAD_DIM
ATTN_KV = ATTN_KV_HEADS * ATTN_HEAD_DIM
GLA_QK = GLA_HEADS * GLA_KEY_DIM
GLA_V = GLA_HEADS * GLA_VAL_DIM
ATTN_COLS = ATTN_Q + 2 * ATTN_KV
GLA_COLS = 2 * GLA_QK + 2 * GLA_V
GATE_COLS = 2 * GLA_GATE_RANK
D_IN = ATTN_COLS + GLA_COLS + GATE_COLS

V7X_LANES = 128
V7X_SUBLANES = 8
V7X_VMEM_LIMIT_CAP = 60 * 1024 * 1024

NEG_BIG = -1e30

TOK_TILE = 512
GLA_TILE = 512
PEER_TOK_TILE = 1024
PEER_PASS_TOK = TOK_TILE
EXPERT_TILE = 512
ROWS_PER_TILE = EXPERT_TILE // N_KEYS
ROW_BLOCK = 2 * ROWS_PER_TILE
KEY_CHUNK = 64
KEY_BLOCK = 32
CHUNKS_PER_TILE = N_KEYS // KEY_CHUNK
ACT_SLAB = EXPERT_TILE // CHUNKS_PER_TILE
OUT_SLAB = D_MODEL // CHUNKS_PER_TILE
N_RANKS = PEER_TOPK + 1
CAND_PAIRS = tuple((i, j) for i in range(N_RANKS) for j in range(N_RANKS) if (i + 1) * (j + 1) <= N_RANKS)


def _vmem_limit(block_bytes):
    return int(min(2 * block_bytes + (16 << 20), V7X_VMEM_LIMIT_CAP))


def _params(semantics, block_bytes):
    return pltpu.CompilerParams(dimension_semantics=semantics, vmem_limit_bytes=_vmem_limit(block_bytes))


def _nt_dot(a, b):
    return lax.dot_general(a, b, (((1,), (1,)), ((), ())), preferred_element_type=F32)


def _dot(a, b):
    return jnp.dot(a, b, preferred_element_type=F32)


def _rms(x, g):
    return x * lax.rsqrt(jnp.mean(x * x, axis=-1, keepdims=True) + EPS) * g


def _mod_kernel(c_ref, w_ref, b_ref, o_ref):
    c = c_ref[...]
    s = c * jax.nn.sigmoid(c)
    o_ref[...] = _dot(s.astype(BF16), w_ref[...].astype(BF16)) + b_ref[...]


def _modulation(c, w_ada, b_ada):
    bsz = c.shape[0]
    rows = -(-bsz // V7X_SUBLANES) * V7X_SUBLANES
    cp = jnp.zeros((rows, D_MODEL), F32).at[:bsz].set(c)
    n_out = w_ada.shape[1]
    tile = D_MODEL
    out = pl.pallas_call(
        _mod_kernel,
        grid=(n_out // tile,),
        in_specs=[pl.BlockSpec((rows, D_MODEL), lambda j: (0, 0)),
                  pl.BlockSpec((D_MODEL, tile), lambda j: (0, j)),
                  pl.BlockSpec((1, tile), lambda j: (0, j))],
        out_specs=pl.BlockSpec((rows, tile), lambda j: (0, j)),
        out_shape=jax.ShapeDtypeStruct((rows, n_out), F32),
        compiler_params=_params(("parallel",), D_MODEL * tile * 4),
        name="adaln_modulation",
    )(cp, w_ada, b_ada.reshape(1, n_out))
    return out[:bsz].reshape(bsz, 1, n_out)


def _inproj_kernel(x_ref, mod_ref, g_ref, w_ref, oa_ref, og_ref, or_ref):
    h = _rms(x_ref[...], g_ref[...])
    h = h * (1.0 + mod_ref[:, D_MODEL:2 * D_MODEL]) + mod_ref[:, 0:D_MODEL]
    p = _dot(h.astype(BF16), w_ref[...])
    oa_ref[...] = p[:, :ATTN_COLS]
    og_ref[...] = p[:, ATTN_COLS:ATTN_COLS + GLA_COLS]
    or_ref[...] = p[:, ATTN_COLS + GLA_COLS:]


def _in_projection(x, mod, norm_g, w_in_bf16):
    bsz, seq, _ = x.shape
    tb = TOK_TILE
    tok = lambda cols: pl.BlockSpec((None, tb, cols), lambda b, i: (b, i, 0))
    return pl.pallas_call(
        _inproj_kernel,
        grid=(bsz, seq // tb),
        in_specs=[tok(D_MODEL),
                  pl.BlockSpec((None, 1, 6 * D_MODEL), lambda b, i: (b, 0, 0)),
                  pl.BlockSpec((1, D_MODEL), lambda b, i: (0, 0)),
                  pl.BlockSpec((D_MODEL, D_IN), lambda b, i: (0, 0))],
        out_specs=[tok(ATTN_COLS), tok(GLA_COLS), tok(GATE_COLS)],
        out_shape=[jax.ShapeDtypeStruct((bsz, seq, ATTN_COLS), F32),
                   jax.ShapeDtypeStruct((bsz, seq, GLA_COLS), F32),
                   jax.ShapeDtypeStruct((bsz, seq, GATE_COLS), F32)],
        compiler_params=_params(("parallel", "parallel"),
                                tb * D_MODEL * 4 + D_MODEL * D_IN * 2 + 2 * tb * D_IN * 4),
        name="norm1_in_projection",
    )(x, mod, norm_g.reshape(1, D_MODEL), w_in_bf16)


def _t5_buckets(rel):
    nb = N_BUCKETS // 2
    ret = (rel > 0).astype(np.int32) * nb
    n = np.abs(rel)
    max_exact = nb // 2
    large = max_exact + (np.log(np.maximum(n, 1) / max_exact) / math.log(MAX_DISTANCE / max_exact)
                         * (nb - max_exact)).astype(np.int32)
    large = np.minimum(large, nb - 1)
    return (ret + np.where(n < max_exact, n, large)).astype(np.int32)


def _band_bias(rel_bias):
    span = 4 * BLOCK
    offsets = np.arange(span) - (2 * BLOCK - 1)
    per_offset = rel_bias[_t5_buckets(offsets)].astype(F32).T
    shifted = jnp.roll(per_offset, -(BLOCK - 1), axis=1)
    skew = jnp.tile(shifted, (1, BLOCK))[:, :BLOCK * (span - 1)].reshape(-1, BLOCK, span - 1)
    rel = np.arange(3 * BLOCK)[None, :] - BLOCK - np.arange(BLOCK)[:, None]
    band = jnp.asarray(np.abs(rel) <= WINDOW)
    return jnp.where(band[None], skew[:, :, :3 * BLOCK], NEG_BIG)


def _attn_kernel(sink_ref, q_ref, kvp_ref, kvc_ref, kvn_ref, bias_ref, qg_ref, kg_ref, beta_ref, o_ref):
    n = pl.program_id(1)
    last = pl.num_programs(1) - 1
    q = q_ref[...]
    kv = jnp.concatenate([kvp_ref[...], kvc_ref[...], kvn_ref[...]], axis=0)
    j = lax.broadcasted_iota(jnp.int32, (BLOCK, 3 * BLOCK), 1)
    valid = jnp.logical_and(jnp.logical_or(j >= BLOCK, n > 0), jnp.logical_or(j < 2 * BLOCK, n < last))
    scale = ATTN_HEAD_DIM ** -0.5
    outs = []
    for hk in range(ATTN_KV_HEADS):
        kh = _rms(kv[:, hk * ATTN_HEAD_DIM:(hk + 1) * ATTN_HEAD_DIM], kg_ref[...]).astype(BF16)
        vh = kv[:, ATTN_KV + hk * ATTN_HEAD_DIM:ATTN_KV + (hk + 1) * ATTN_HEAD_DIM].astype(BF16)
        for g in range(ATTN_GROUP):
            h = hk * ATTN_GROUP + g
            qh = _rms(q[:, h * ATTN_HEAD_DIM:(h + 1) * ATTN_HEAD_DIM], qg_ref[...]).astype(BF16)
            s = _nt_dot(qh, kh) * scale
            s = jnp.where(valid, s + bias_ref[h], NEG_BIG)
            sink = sink_ref[h]
            m = jnp.maximum(jnp.max(s, axis=-1, keepdims=True), sink)
            p = jnp.exp(s - m)
            den = jnp.sum(p, axis=-1, keepdims=True) + jnp.exp(sink - m)
            o = _dot(p.astype(BF16), vh) / den
            outs.append(_rms(o, beta_ref[:, h * ATTN_HEAD_DIM:(h + 1) * ATTN_HEAD_DIM]))
    o_ref[...] = jnp.concatenate(outs, axis=-1)


def _windowed_attention(qkv, bias, sink, q_norm_g, k_norm_g, beta_attn):
    bsz, seq, _ = qkv.shape
    nb = seq // BLOCK
    kv_cols = 2 * ATTN_KV
    kv_blk = ATTN_Q // kv_cols
    kv_spec = lambda f: pl.BlockSpec((None, BLOCK, kv_cols), lambda b, n: (b, f(n), kv_blk))
    const = lambda shape: pl.BlockSpec(shape, lambda b, n: tuple(0 for _ in shape))
    return pl.pallas_call(
        _attn_kernel,
        grid=(bsz, nb),
        in_specs=[pl.BlockSpec(memory_space=pltpu.SMEM),
                  pl.BlockSpec((None, BLOCK, ATTN_Q), lambda b, n: (b, n, 0)),
                  kv_spec(lambda n: jnp.maximum(n - 1, 0)),
                  kv_spec(lambda n: n),
                  kv_spec(lambda n: jnp.minimum(n + 1, nb - 1)),
                  const((ATTN_HEADS, BLOCK, 3 * BLOCK)),
                  const((1, ATTN_HEAD_DIM)), const((1, ATTN_HEAD_DIM)), const((1, ATTN_Q))],
        out_specs=pl.BlockSpec((None, BLOCK, ATTN_Q), lambda b, n: (b, n, 0)),
        out_shape=jax.ShapeDtypeStruct((bsz, seq, ATTN_Q), F32),
        compiler_params=_params(("parallel", "parallel"), ATTN_HEADS * BLOCK * 3 * BLOCK * 4 + 8 * BLOCK * ATTN_Q * 4),
        name="windowed_attention",
    )(sink, qkv, qkv, qkv, qkv, bias, q_norm_g.reshape(1, -1), k_norm_g.reshape(1, -1), beta_attn.reshape(1, -1))


def _split3(x):
    hi = x.astype(BF16)
    r = x - hi.astype(F32)
    mid = r.astype(BF16)
    lo = (r - mid.astype(F32)).astype(BF16)
    return hi, mid, lo


def _gla_direction(qk_ref, v_ref, gr, up_ref, gb_ref, tri_ref, o_ref, st_ref, b_ref, reverse):
    c = GLA_CHUNK
    nch = GLA_TILE // c
    z = _dot(gr.astype(BF16), up_ref[...]) + gb_ref[...]
    log_a = (jnp.minimum(z, 0.0) - jnp.log1p(jnp.exp(-jnp.abs(z)))) / GLA_GATE_NORM
    hi, mid, lo = _split3(log_a)
    tri = tri_ref[...]
    b_ref[...] = _dot(tri, hi) + _dot(tri, mid) + _dot(tri, lo)

    row = lax.broadcasted_iota(jnp.int32, (c, GLA_QK), 0)
    lane = lax.broadcasted_iota(jnp.int32, (c, GLA_QK), 1)
    key_in_head = lane % GLA_KEY_DIM
    causal = (key_in_head >= row) if reverse else (key_in_head <= row)
    qk_head = lane // GLA_KEY_DIM
    v_head = lax.broadcasted_iota(jnp.int32, (c, GLA_V), 1) // GLA_VAL_DIM
    st_rows = lax.broadcasted_iota(jnp.int32, (GLA_V, GLA_QK), 0) // GLA_VAL_DIM
    st_cols = lax.broadcasted_iota(jnp.int32, (GLA_V, GLA_QK), 1) // GLA_KEY_DIM
    same_head = st_rows == st_cols

    for ci in range(nch):
        blk = nch - 1 - ci if reverse else ci
        rows = slice(blk * c, (blk + 1) * c)
        b = b_ref[rows, :]
        b_last = b[0:1, :] if reverse else b[c - 1:c, :]
        q = qk_ref[rows, 0:GLA_QK]
        k = qk_ref[rows, GLA_QK:2 * GLA_QK]
        v = v_ref[rows, :]
        q_dec = (q * ((GLA_KEY_DIM ** -0.5) * jnp.exp(b))).astype(BF16)
        k_dec = k * jnp.exp(-b)
        k_end = (k * jnp.exp(b_last - b)).astype(BF16)
        decay = jnp.exp(b_last)
        k_heads = jnp.concatenate([jnp.where(qk_head == h, k_dec, 0.0) for h in range(GLA_HEADS)], axis=0)
        v_heads = jnp.concatenate([jnp.where(v_head == h, v, 0.0) for h in range(GLA_HEADS)], axis=0)
        a = jnp.where(causal, _nt_dot(q_dec, k_heads.astype(BF16)), 0.0)
        st = st_ref[...]
        o_ref[rows, :] = _dot(a.astype(BF16), v_heads.astype(BF16)) + _nt_dot(q_dec, st.astype(BF16))
        kv_t = lax.dot_general(v.astype(BF16), k_end, (((0,), (0,)), ((), ())), preferred_element_type=F32)
        st_ref[...] = st * decay + jnp.where(same_head, kv_t, 0.0)


def _gla_kernel(qkf_ref, vf_ref, grf_ref, qkb_ref, vb_ref, grb_ref, upf_ref, bf_ref, upb_ref, bb_ref,
                trif_ref, trib_ref, of_ref, ob_ref, stf_ref, stb_ref, cumf_ref, cumb_ref):
    @pl.when(pl.program_id(1) == 0)
    def _():
        stf_ref[...] = jnp.zeros_like(stf_ref)
        stb_ref[...] = jnp.zeros_like(stb_ref)

    _gla_direction(qkf_ref, vf_ref, grf_ref[:, 0:GLA_GATE_RANK], upf_ref, bf_ref, trif_ref,
                   of_ref, stf_ref, cumf_ref, False)
    _gla_direction(qkb_ref, vb_ref, grb_ref[:, GLA_GATE_RANK:2 * GLA_GATE_RANK], upb_ref, bb_ref, trib_ref,
                   ob_ref, stb_ref, cumb_ref, True)


def _gla(gla_in, gates, up_f, b_f, up_b, b_b):
    bsz, seq, _ = gla_in.shape
    tb = GLA_TILE
    nblk = seq // tb
    fwd = lambda cols, cb: pl.BlockSpec((None, tb, cols), lambda b, i: (b, i, cb))
    bwd = lambda cols, cb: pl.BlockSpec((None, tb, cols), lambda b, i: (b, nblk - 1 - i, cb))
    const = lambda shape: pl.BlockSpec(shape, lambda b, i: tuple(0 for _ in shape))
    chunk_of = np.arange(tb) // GLA_CHUNK
    same_chunk = chunk_of[:, None] == chunk_of[None, :]
    pos = np.arange(tb)
    tri_f = jnp.asarray(same_chunk & (pos[None, :] <= pos[:, None]), BF16)
    tri_b = jnp.asarray(same_chunk & (pos[None, :] >= pos[:, None]), BF16)
    state = pltpu.VMEM((GLA_V, GLA_QK), F32)
    cum = pltpu.VMEM((tb, GLA_QK), F32)
    return pl.pallas_call(
        _gla_kernel,
        grid=(bsz, nblk),
        in_specs=[fwd(2 * GLA_QK, 0), fwd(GLA_V, 1), fwd(GATE_COLS, 0),
                  bwd(2 * GLA_QK, 0), bwd(GLA_V, 1), bwd(GATE_COLS, 0),
                  const((GLA_GATE_RANK, GLA_QK)), const((1, GLA_QK)),
                  const((GLA_GATE_RANK, GLA_QK)), const((1, GLA_QK)),
                  const((tb, tb)), const((tb, tb))],
        out_specs=[pl.BlockSpec((None, tb, GLA_V), lambda b, i: (b, i, 0)),
                   pl.BlockSpec((None, tb, GLA_V), lambda b, i: (b, nblk - 1 - i, 0))],
        out_shape=[jax.ShapeDtypeStruct((bsz, seq, GLA_V), F32)] * 2,
        scratch_shapes=[state, state, cum, cum],
        compiler_params=_params(("parallel", "arbitrary"), 2 * tb * (2 * GLA_QK + 2 * GLA_V + V7X_LANES) * 4),
        name="gla_bidirectional",
    )(gla_in, gla_in, gates, gla_in, gla_in, gates,
      up_f.astype(BF16), b_f.reshape(1, -1), up_b.astype(BF16), b_b.reshape(1, -1), tri_f, tri_b)


def _outproj_kernel(att_ref, of_ref, ob_ref, og_ref, x_ref, mod_ref, beta_ref, w_ref, g2_ref, x1_ref, h2_ref):
    o = of_ref[...] + ob_ref[...]
    gate = og_ref[...]
    gate = gate * jax.nn.sigmoid(gate)
    parts = [att_ref[...].astype(BF16)]
    for h in range(GLA_HEADS):
        sl = slice(h * GLA_VAL_DIM, (h + 1) * GLA_VAL_DIM)
        parts.append((_rms(o[:, sl], beta_ref[:, sl]) * gate[:, sl]).astype(BF16))
    mix = _dot(jnp.concatenate(parts, axis=-1), w_ref[...])
    d = D_MODEL
    x1 = x_ref[...] + mod_ref[:, 2 * d:3 * d] * mix
    x1_ref[...] = x1
    h2 = _rms(x1, g2_ref[...]) * (1.0 + mod_ref[:, 4 * d:5 * d]) + mod_ref[:, 3 * d:4 * d]
    h2_ref[...] = h2.astype(BF16)


def _sort_network(n):
    def merge(lo, hi, r):
        step = 2 * r
        if step < hi - lo:
            yield from merge(lo, hi, step)
            yield from merge(lo + r, hi, step)
            yield from ((i, i + r) for i in range(lo + r, hi - r, step))
        else:
            yield (lo, lo + r)

    def sort(lo, hi):
        if hi > lo:
            mid = lo + (hi - lo) // 2
            yield from sort(lo, mid)
            yield from sort(mid + 1, hi)
            yield from merge(lo, hi, 1)

    return tuple(sort(0, n - 1))


def _selection_network(n, keep):
    pairs = [(i, j) for i, j in _sort_network(1 << (n - 1).bit_length()) if j < n]
    needed = set(range(keep))
    net = []
    for i, j in reversed(pairs):
        if i in needed or j in needed:
            net.append((i, j, i in needed, j in needed))
            needed |= {i, j}
    return tuple(reversed(net))


CAND_NETWORK = _selection_network(len(CAND_PAIRS), N_RANKS)


def _merge_sorted(srt, shift):
    n = len(srt)
    other = [pltpu.roll(t, shift, 0) for t in srt]
    srt = [jnp.maximum(srt[i], other[n - 1 - i]) for i in range(n)]
    stride = n // 2
    while stride >= 1:
        for i in range(n):
            if i & stride == 0:
                srt[i], srt[i + stride] = (jnp.maximum(srt[i], srt[i + stride]),
                                           jnp.minimum(srt[i], srt[i + stride]))
        stride //= 2
    return srt


def _top_values(x, out_ref, head):
    n = N_KEYS // V7X_SUBLANES
    assert n == PEER_TOPK and V7X_SUBLANES == 8
    group = 4 * V7X_LANES
    sub = lax.broadcasted_iota(jnp.int32, (V7X_SUBLANES, V7X_LANES), 0)
    network = _sort_network(n)
    for g0 in range(0, x.shape[1], group):
        tiles = [[x[i * V7X_SUBLANES:(i + 1) * V7X_SUBLANES, g0 + c * V7X_LANES:g0 + (c + 1) * V7X_LANES]
                  for i in range(n)] for c in range(4)]
        merged = []
        for c in range(4):
            srt = list(tiles[c])
            for i, j in network:
                srt[i], srt[j] = jnp.maximum(srt[i], srt[j]), jnp.minimum(srt[i], srt[j])
            merged.append(_merge_sorted(srt, 4))
        packed = [[jnp.where(sub < 4, merged[2 * q][i], merged[2 * q + 1][i]) for i in range(n)] for q in range(2)]
        packed = [_merge_sorted(p, 2) for p in packed]
        keep = (sub & 2) != 0
        quad = [jnp.where(keep, packed[0][i], pltpu.roll(packed[1][i], 6, 0)) for i in range(n)]
        quad = _merge_sorted(quad, 1)
        for c, s in ((0, 3), (1, 7), (2, 1), (3, 5)):
            cols = slice(g0 + c * V7X_LANES, g0 + (c + 1) * V7X_LANES)
            for r in range(PEER_TOPK):
                out_ref[r, pl.ds(head, 1), cols] = quad[r][s:s + 1, :]
            last = quad[PEER_TOPK - 1][s:s + 1, :]
            below = [jnp.where(t < last, t, -jnp.inf) for t in tiles[c]]
            out_ref[PEER_TOPK, pl.ds(head, 1), cols] = jnp.max(functools.reduce(jnp.maximum, below), axis=0,
                                                               keepdims=True)


def _peer_route_kernel(h2_ref, wq_ref, keys_ref, h2t_ref, s2_ref, e2_ref, thr_ref, c_ref, sc_ref, a_ref, b_ref):
    h2 = h2_ref[...]
    h2t_ref[...] = h2.astype(F32).T.astype(BF16)
    q = _dot(h2, wq_ref[...]).astype(BF16)
    for h in range(PEER_HEADS):
        for p in range(2):
            blk = h * 2 + p
            s = _nt_dot(keys_ref[h, p], q[:, blk * PEER_HALF:(blk + 1) * PEER_HALF])
            sc_ref[blk] = s
            _top_values(s, a_ref if p == 0 else b_ref, h)
    thr_cols, scale_cols = [], []
    for c0 in range(0, h2.shape[0], V7X_LANES):
        cols = slice(c0, c0 + V7X_LANES)
        vals = [a_ref[i, :, cols] + b_ref[j, :, cols] for (i, j) in CAND_PAIRS]
        for i, j, need_max, need_min in CAND_NETWORK:
            vals[i], vals[j] = (jnp.maximum(vals[i], vals[j]) if need_max else None,
                                jnp.minimum(vals[i], vals[j]) if need_min else None)
        thr_cols.append(0.5 * (vals[PEER_TOPK - 1] + vals[PEER_TOPK]))
        z = functools.reduce(jnp.add, [jnp.exp(v - vals[0]) for v in vals[:PEER_TOPK]])
        scale_cols.append(0.5 * (1.0 / z))
    thr = jnp.concatenate(thr_cols, axis=1)
    half_inv_z = jnp.concatenate(scale_cols, axis=1)
    a0 = a_ref[0]
    b0 = b_ref[0]
    for h in range(PEER_HEADS):
        s1 = sc_ref[2 * h]
        s2 = sc_ref[2 * h + 1]
        thr_ref[h] = thr[h:h + 1, :] - s1
        c_ref[h] = jnp.exp(s1 - a0[h:h + 1, :]) * half_inv_z[h:h + 1, :]
        s2_ref[h] = s2
        e2_ref[h] = jnp.exp(s2 - b0[h:h + 1, :])


def _outproj_route_kernel(att_ref, of_ref, ob_ref, og_ref, x_ref, mod_ref, beta_ref, w_ref, g2_ref, wq_ref, keys_ref,
                          x1_ref, h2t_ref, s2_ref, e2_ref, thr_ref, c_ref, h2_ref, sc_ref, a_ref, b_ref):
    _outproj_kernel(att_ref, of_ref, ob_ref, og_ref, x_ref, mod_ref, beta_ref, w_ref, g2_ref, x1_ref, h2_ref)
    _peer_route_kernel(h2_ref, wq_ref, keys_ref, h2t_ref, s2_ref, e2_ref, thr_ref, c_ref, sc_ref, a_ref, b_ref)


def _out_projection_route(att, o_f, o_b, gla_in, x, mod, beta_gla, w_out_bf16, norm2_g, wq_bf16, keys_bf16):
    bsz, seq, _ = x.shape
    tb = TOK_TILE
    tok = lambda cols, cb=0: pl.BlockSpec((None, tb, cols), lambda b, i: (b, i, cb))
    once = pl.Buffered(1)
    const = lambda shape: pl.BlockSpec(shape, lambda b, i: tuple(0 for _ in shape), pipeline_mode=once)
    per_head = lambda: pl.BlockSpec((None, PEER_HEADS, N_KEYS, tb), lambda b, i: (b, 0, 0, i))
    head_shape = jax.ShapeDtypeStruct((bsz, PEER_HEADS, N_KEYS, seq), F32)
    qcols = PEER_HEADS * PEER_KEY_DIM
    return pl.pallas_call(
        _outproj_route_kernel,
        grid=(bsz, seq // tb),
        in_specs=[tok(ATTN_Q), tok(GLA_V), tok(GLA_V), tok(GLA_V, (2 * GLA_QK + GLA_V) // GLA_V), tok(D_MODEL),
                  pl.BlockSpec((None, 1, 6 * D_MODEL), lambda b, i: (b, 0, 0)),
                  const((1, GLA_V)), const((D_MODEL, D_MODEL)), const((1, D_MODEL)),
                  const((D_MODEL, qcols)), const((PEER_HEADS, 2, N_KEYS, PEER_HALF))],
        out_specs=[tok(D_MODEL),
                   pl.BlockSpec((None, None, D_MODEL, tb), lambda b, i: (b, i, 0, 0)),
                   per_head(), per_head(), per_head(), per_head()],
        out_shape=[jax.ShapeDtypeStruct((bsz, seq, D_MODEL), F32),
                   jax.ShapeDtypeStruct((bsz, seq // tb, D_MODEL, tb), BF16),
                   head_shape, head_shape, head_shape, head_shape],
        scratch_shapes=[pltpu.VMEM((tb, D_MODEL), BF16),
                        pltpu.VMEM((2 * PEER_HEADS, N_KEYS, tb), F32),
                        pltpu.VMEM((N_RANKS, PEER_HEADS, tb), F32),
                        pltpu.VMEM((N_RANKS, PEER_HEADS, tb), F32)],
        compiler_params=_params(("parallel", "parallel"),
                                tb * (4 * GLA_V + 3 * D_MODEL) * 4 + D_MODEL * D_MODEL * 2
                                + D_MODEL * qcols * 2 + 5 * PEER_HEADS * N_KEYS * tb * 4 + tb * qcols * 4),
        name="out_projection_peer_routing",
    )(att, o_f, o_b, gla_in, x, mod, beta_gla.reshape(1, -1), w_out_bf16, norm2_g.reshape(1, -1),
      wq_bf16, keys_bf16)


def _gelu_x2(x):
    return x * (1.0 + lax.erf(x * (2.0 ** -0.5)))


def _routing_block(s2_ref, e2_ref, thr_ref, c_ref, row0, act_ref, wg_ref, col0, k0):
    cols = slice(col0, col0 + V7X_LANES)
    local = slice(col0 % PEER_PASS_TOK, col0 % PEER_PASS_TOK + V7X_LANES)
    for kk in range(0, KEY_CHUNK, KEY_BLOCK):
        keys = pl.ds(pl.multiple_of(k0 + kk, KEY_BLOCK), KEY_BLOCK)
        w = [None] * ROWS_PER_TILE
        for h in range(PEER_HEADS):
            s2 = s2_ref[h, keys, cols]
            e2 = e2_ref[h, keys, cols]
            for r in range(ROWS_PER_TILE):
                thr = thr_ref[h, row0 + r:row0 + r + 1, cols]
                coef = c_ref[h, row0 + r:row0 + r + 1, cols]
                term = jnp.where(s2 >= thr, e2 * coef, 0.0)
                w[r] = term if h == 0 else w[r] + term
        for r in range(ROWS_PER_TILE):
            rows = pl.ds(pl.multiple_of(r * N_KEYS + k0 + kk, KEY_BLOCK), KEY_BLOCK)
            wg_ref[rows, local] = (w[r] * _gelu_x2(act_ref[rows, local])).astype(BF16)


def _peer_dense_kernel(h2t_ref, u_ref, vt_ref, s2_ref, e2_ref, thr0_ref, c0_ref, thr1_ref, c1_ref, x1_ref, mod_ref,
                       y_ref, *scratch):
    g = pl.program_id(2)
    n_pass = PEER_TOK_TILE // PEER_PASS_TOK
    act_refs = [scratch[k * n_pass:(k + 1) * n_pass] for k in range(2)]
    wg_refs = [scratch[(2 + k) * n_pass:(3 + k) * n_pass] for k in range(2)]
    acc_refs = scratch[4 * n_pass:5 * n_pass]

    @pl.when(g == 0)
    def _():
        for ref in scratch:
            ref[...] = jnp.zeros_like(ref)

    for k in range(2):
        thr_ref, c_ref, row0 = (thr0_ref, c0_ref, ROWS_PER_TILE) if k == 0 else (thr1_ref, c1_ref, 0)
        tile = slice(k * EXPERT_TILE, (k + 1) * EXPERT_TILE)
        has_tile = (g > 0) if k == 0 else (g < pl.num_programs(2) - 1)
        for p in range(n_pass):
            def slab(c, carry, k=k, p=p, thr_ref=thr_ref, c_ref=c_ref, row0=row0, tile=tile, with_stage_b=True):
                a0 = pl.multiple_of(c * ACT_SLAB, ACT_SLAB)
                act_refs[k][p][pl.ds(a0, ACT_SLAB), :] = _dot(u_ref[pl.ds(k * EXPERT_TILE + a0, ACT_SLAB), :],
                                                             h2t_ref[p])
                rows = pl.ds(pl.multiple_of(c * OUT_SLAB, OUT_SLAB), OUT_SLAB)
                acc_refs[p][rows, :] += _dot(vt_ref[rows, tile], wg_refs[k][p][...])
                if with_stage_b:
                    k0 = pl.multiple_of(c * KEY_CHUNK, KEY_CHUNK)
                    for t in range(PEER_PASS_TOK // V7X_LANES):
                        _routing_block(s2_ref, e2_ref, thr_ref, c_ref, row0, act_refs[1 - k][p], wg_refs[1 - k][p],
                                       p * PEER_PASS_TOK + t * V7X_LANES, k0)
                return carry

            @pl.when(has_tile)
            def _(slab=slab):
                lax.fori_loop(0, CHUNKS_PER_TILE, slab, 0)

            @pl.when(jnp.logical_not(has_tile))
            def _(slab=slab):
                lax.fori_loop(0, CHUNKS_PER_TILE, functools.partial(slab, with_stage_b=False), 0)

    @pl.when(g == pl.num_programs(2) - 1)
    def _():
        gate = mod_ref[:, 5 * D_MODEL:6 * D_MODEL]
        for p in range(n_pass):
            rows = slice(p * PEER_PASS_TOK, (p + 1) * PEER_PASS_TOK)
            y_ref[rows, :] = x1_ref[rows, :] + gate * acc_refs[p][...].T


def _transposed_blocks(v):
    blk = 2 * EXPERT_TILE
    return jnp.transpose(v.reshape(v.shape[0] // blk, blk, v.shape[1]), (0, 2, 1))


def _peer_dense(h2t, u, vt, s2, e2, thr, coef, x1, mod):
    bsz, seq, _ = x1.shape
    tb = PEER_TOK_TILE
    n_pass = tb // PEER_PASS_TOK
    n_tiles = u.shape[0] // EXPERT_TILE
    n_pairs = n_tiles // 2
    once = pl.Buffered(1)
    per_head = lambda: pl.BlockSpec((None, PEER_HEADS, N_KEYS, tb), lambda b, i, g: (b, 0, 0, i), pipeline_mode=once)
    row_block = lambda f: pl.BlockSpec((None, PEER_HEADS, ROW_BLOCK, tb),
                                       lambda b, i, g: (b, 0, jnp.clip(f(g), 0, n_pairs - 1), i))
    pass_f32 = pltpu.VMEM((EXPERT_TILE, PEER_PASS_TOK), F32)
    pass_bf16 = pltpu.VMEM((EXPERT_TILE, PEER_PASS_TOK), BF16)
    return pl.pallas_call(
        _peer_dense_kernel,
        grid=(bsz, seq // tb, n_pairs + 1),
        in_specs=[pl.BlockSpec((None, n_pass, D_MODEL, PEER_PASS_TOK), lambda b, i, g: (b, i, 0, 0), pipeline_mode=once),
                  pl.BlockSpec((2 * EXPERT_TILE, D_MODEL), lambda b, i, g: (jnp.minimum(g, n_pairs - 1), 0)),
                  pl.BlockSpec((None, D_MODEL, 2 * EXPERT_TILE), lambda b, i, g: (jnp.maximum(g - 1, 0), 0, 0)),
                  per_head(), per_head(),
                  row_block(lambda g: g - 1), row_block(lambda g: g - 1),
                  row_block(lambda g: g), row_block(lambda g: g),
                  pl.BlockSpec((None, tb, D_MODEL), lambda b, i, g: (b, i, 0), pipeline_mode=once),
                  pl.BlockSpec((None, 1, 6 * D_MODEL), lambda b, i, g: (b, 0, 0))],
        out_specs=pl.BlockSpec((None, tb, D_MODEL), lambda b, i, g: (b, i, 0)),
        out_shape=jax.ShapeDtypeStruct((bsz, seq, D_MODEL), F32),
        scratch_shapes=([pass_f32] * (2 * n_pass) + [pass_bf16] * (2 * n_pass)
                        + [pltpu.VMEM((D_MODEL, PEER_PASS_TOK), F32)] * n_pass),
        compiler_params=_params(("parallel", "parallel", "arbitrary"),
                                2 * PEER_HEADS * N_KEYS * tb * 4 + 2 * 2 * EXPERT_TILE * D_MODEL * 4
                                + 3 * tb * D_MODEL * 4 + 4 * EXPERT_TILE * tb * 4),
        name="peer_dense_experts",
    )(h2t, u, vt, s2, e2, thr, coef, thr, coef, x1, mod)


def _layer(x, c, w_ada, b_ada, norm1_g, norm2_g, w_in, q_norm_g, k_norm_g, bias, sink,
           up_f, b_f, up_b, b_b, beta_attn, beta_gla, w_out, wq, keys, u, vt):
    mod = _modulation(c, w_ada, b_ada)
    qkv, gla_in, gates = _in_projection(x, mod, norm1_g, w_in)
    att = _windowed_attention(qkv, bias, sink, q_norm_g, k_norm_g, beta_attn)
    o_f, o_b = _gla(gla_in, gates, up_f, b_f, up_b, b_b)
    x1, h2t, s2, e2, thr, coef = _out_projection_route(att, o_f, o_b, gla_in, x, mod, beta_gla, w_out, norm2_g,
                                                       wq, keys)
    return _peer_dense(h2t, u, vt, s2, e2, thr, coef, x1, mod)


def kernel(x_prompt, x_sample, c_prompt, c_sample, w_ada, b_ada, norm1_g, norm2_g, w_in, q_norm_g, k_norm_g,
           rel_bias, sink, gk_up_fwd, gk_bias_fwd, gk_up_bwd, gk_bias_bwd, beta_attn, beta_gla, w_out,
           peer_query, peer_subkeys, peer_u, peer_v):
    depth = w_ada.shape[0]
    bias = _band_bias(rel_bias)
    y_prompt, y_sample = x_prompt, x_sample
    for l in range(depth):
        params = (w_ada[l], b_ada[l], norm1_g[l], norm2_g[l], w_in[l].astype(BF16), q_norm_g[l], k_norm_g[l],
                  bias, sink[l], gk_up_fwd[l], gk_bias_fwd[l], gk_up_bwd[l], gk_bias_bwd[l],
                  beta_attn[l], beta_gla[l], w_out[l].astype(BF16), peer_query[l].astype(BF16),
                  peer_subkeys[l].astype(BF16), peer_u[l], _transposed_blocks(peer_v[l]))
        y_prompt = _layer(y_prompt, c_prompt, *params)
        y_sample = _layer(y_sample, c_sample, *params)
    return (y_prompt, y_sample)
```
